```python
import jax, jax.numpy as jnp
from jax import lax
import numpy as np

D_MODEL = 2048
BATCH = 16
SEQ = 2048
DEPTH = 2

GRID_W = 64
CTX_LEN = 256
EPS = 1e-6
NEG_INF = -1e30
NA_HEADS = 16
NA_HEAD_DIM = 64
NA_W = NA_HEADS * NA_HEAD_DIM
NA_WIN_ROWS = 8
NA_WIN_COLS = 16
NA_BAND_COLS = 2 * NA_WIN_COLS
MLA_HEADS = 16
MLA_NOPE = 128
MLA_ROPE = 64
MLA_QK = MLA_NOPE + MLA_ROPE
MLA_V = 128
MLA_Q_LORA = 768
MLA_KV_LORA = 512
ROPE_THETA = 10000.0
Q_BLOCK = 128
CONV_CH = 1024
CONV_WIDTH = 31
N_EXPERTS = 64
TOP_K = 8
D_EXPERT = 512
D_SHARED = 512
ROUTED_SCALE = 2.5
MOE_BLOCK = 256
IN_SPLITS = (NA_W, NA_W, NA_W, MLA_Q_LORA, MLA_KV_LORA, MLA_ROPE, CONV_CH, CONV_CH, D_MODEL, D_MODEL, D_MODEL)
D_IN = sum(IN_SPLITS)

kernel_name = 'hybrid_natten_mla_conformer_moe_dit'


def rms_norm(x, g):
    xf = x.astype(jnp.float32)
    y = xf * lax.rsqrt(jnp.mean(xf * xf, axis=-1, keepdims=True) + EPS)
    return (y * g.astype(jnp.float32)).astype(x.dtype)


def layer_norm(x, g, b):
    xf = x.astype(jnp.float32)
    mu = jnp.mean(xf, axis=-1, keepdims=True)
    var = jnp.mean(jnp.square(xf - mu), axis=-1, keepdims=True)
    return ((xf - mu) * lax.rsqrt(var + EPS) * g.astype(jnp.float32) + b.astype(jnp.float32)).astype(x.dtype)


def _heads(t, n, d):
    return t.reshape(t.shape[0], t.shape[1], n, d)


def _split_in(t):
    cuts = [int(v) for v in np.cumsum(IN_SPLITS)[:-1]]
    return jnp.split(t, cuts, axis=-1)


def rope_2d(x, pos_row, pos_col):
    half = x.shape[-1] // 2
    nf = half // 2
    inv = ROPE_THETA ** (-jnp.arange(nf, dtype=jnp.float32) / nf)

    def rot(xa, pos):
        ang = pos.astype(jnp.float32)[:, None] * inv
        cos = jnp.cos(ang)[None, :, None, :]
        sin = jnp.sin(ang)[None, :, None, :]
        x1 = xa[..., :nf].astype(jnp.float32)
        x2 = xa[..., nf:].astype(jnp.float32)
        return jnp.concatenate([x1 * cos - x2 * sin, x2 * cos + x1 * sin], axis=-1)

    out = jnp.concatenate([rot(x[..., :half], pos_row), rot(x[..., half:], pos_col)], axis=-1)
    return out.astype(x.dtype)


def block_attention(q, k, v):
    B, Lq, H, dq = q.shape
    nb = Lq // Q_BLOCK
    scale = dq ** -0.5
    qb = jnp.moveaxis(q.reshape(B, nb, Q_BLOCK, H, dq), 1, 0)

    def one_block(qi):
        s = jnp.einsum('bqhd,bshd->bhqs', qi, k).astype(jnp.float32) * scale
        p = jax.nn.softmax(s, axis=-1).astype(v.dtype)
        return jnp.einsum('bhqs,bshd->bqhd', p, v)

    o = lax.map(one_block, qb)
    return jnp.moveaxis(o, 0, 1).reshape(B, Lq, H * v.shape[-1])


def neighbourhood_attention(q, k, v, kc, vc, rpb):
    B, L, H, d = q.shape
    rows = L // GRID_W
    wr = min(NA_WIN_ROWS, rows)
    ncb = GRID_W // NA_WIN_COLS
    scale = d ** -0.5
    qg = q.reshape(B, rows, ncb, NA_WIN_COLS, H, d)
    kg = k.reshape(B, rows, GRID_W, H, d)
    vg = v.reshape(B, rows, GRID_W, H, d)
    row_start = jnp.clip(jnp.arange(rows) - wr // 2, 0, rows - wr)
    jb = np.arange(ncb)
    band_start = np.clip(jb * NA_WIN_COLS - NA_WIN_COLS // 2, 0, GRID_W - NA_BAND_COLS)
    band_cols = band_start[:, None] + np.arange(NA_BAND_COLS)
    q_cols = jb[:, None] * NA_WIN_COLS + np.arange(NA_WIN_COLS)
    c0 = np.clip(q_cols - NA_WIN_COLS // 2, 0, GRID_W - NA_WIN_COLS)
    kcol = band_cols[:, None, :]
    col_ok = (kcol >= c0[..., None]) & (kcol < c0[..., None] + NA_WIN_COLS)
    dc_idx = np.clip(kcol - q_cols[..., None] + NA_WIN_COLS - 1, 0, 2 * NA_WIN_COLS - 2)
    n_win = wr * NA_BAND_COLS

    def one_row(r):
        r0 = row_start[r]
        q_r = qg[:, r]
        k_rows = lax.dynamic_slice_in_dim(kg, r0, wr, axis=1)
        v_rows = lax.dynamic_slice_in_dim(vg, r0, wr, axis=1)
        k_band = k_rows[:, :, band_cols]
        v_band = v_rows[:, :, band_cols]
        dr_idx = r0 + jnp.arange(wr) - r + NA_WIN_ROWS - 1
        bias = rpb[:, dr_idx[:, None, None, None], dc_idx[None]]
        bias = jnp.transpose(bias, (0, 2, 3, 1, 4)).astype(jnp.float32)
        s_win = jnp.einsum('bjqhd,bwjkhd->bhjqwk', q_r, k_band).astype(jnp.float32) * scale + bias
        s_win = jnp.where(col_ok[None, None, :, :, None, :], s_win, NEG_INF)
        s_ctx = jnp.einsum('bjqhd,bchd->bhjqc', q_r, kc).astype(jnp.float32) * scale
        s = jnp.concatenate([s_win.reshape(B, H, ncb, NA_WIN_COLS, n_win), s_ctx], axis=-1)
        p = jax.nn.softmax(s, axis=-1).astype(v.dtype)
        p_win = p[..., :n_win].reshape(B, H, ncb, NA_WIN_COLS, wr, NA_BAND_COLS)
        p_ctx = p[..., n_win:]
        o = (jnp.einsum('bhjqwk,bwjkhd->bjqhd', p_win, v_band)
             + jnp.einsum('bhjqc,bchd->bjqhd', p_ctx, vc))
        return o.reshape(B, GRID_W, H, d)

    out = lax.map(one_row, jnp.arange(rows))
    return jnp.moveaxis(out, 0, 1).reshape(B, L, H * d)


def mla_project(cq, ckv, kr, p, pos_row, pos_col):
    B, L, _ = cq.shape
    q = _heads(rms_norm(cq, p['mla_q_lora_g']) @ p['mla_w_uq'], MLA_HEADS, MLA_QK)
    kv = _heads(rms_norm(ckv, p['mla_kv_lora_g']) @ p['mla_w_ukv'], MLA_HEADS, MLA_NOPE + MLA_V)
    k_rope = jnp.broadcast_to(kr[:, :, None, :], (B, L, MLA_HEADS, MLA_ROPE))
    k = jnp.concatenate([kv[..., :MLA_NOPE], k_rope], axis=-1)
    v = kv[..., MLA_NOPE:]
    q = rms_norm(q, p['mla_q_g'])
    k = rms_norm(k, p['mla_k_g'])
    if pos_row is not None:
        q = jnp.concatenate([q[..., :MLA_NOPE], rope_2d(q[..., MLA_NOPE:], pos_row, pos_col)], axis=-1)
        k = jnp.concatenate([k[..., :MLA_NOPE], rope_2d(k[..., MLA_NOPE:], pos_row, pos_col)], axis=-1)
    return q, k, v


def conformer_conv(a, b, p):
    u = a * jax.nn.sigmoid(b)
    u = lax.conv_general_dilated(
        u, p['conv_w_dw'][:, None, :].astype(u.dtype), window_strides=(1,),
        padding=((CONV_WIDTH // 2, CONV_WIDTH // 2),),
        dimension_numbers=('NWC', 'WIO', 'NWC'), feature_group_count=CONV_CH) + p['conv_b_dw']
    u = jax.nn.silu(layer_norm(u, p['conv_ln_g'], p['conv_ln_b']))
    return u @ p['w_conv_o']


def gated_merge(o_na, o_mla, y_cv, g_na, g_mla, g_cv, p):
    m = (jax.nn.sigmoid(g_na) * (o_na @ p['w_na_o'])
         + jax.nn.sigmoid(g_mla) * (o_mla @ p['w_mla_o'])
         + jax.nn.sigmoid(g_cv) * y_cv)
    return m @ p['w_out']


def hybrid_mixer(hx, hc, p, pos_row, pos_col, need_ctx):
    B, L, _ = hx.shape
    (xq, xk, xv, xcq, xckv, xkr, xa, xb, xg_na, xg_mla, xg_cv) = _split_in(hx @ p['w_in'])
    (cq, ck, cv, ccq, cckv, ckr, ca, cb, cg_na, cg_mla, cg_cv) = _split_in(hc @ p['w_in'])
    qa_x = rms_norm(_heads(xq, NA_HEADS, NA_HEAD_DIM), p['na_q_g'])
    ka_x = rms_norm(_heads(xk, NA_HEADS, NA_HEAD_DIM), p['na_k_g'])
    va_x = _heads(xv, NA_HEADS, NA_HEAD_DIM)
    ka_c = rms_norm(_heads(ck, NA_HEADS, NA_HEAD_DIM), p['na_k_g'])
    va_c = _heads(cv, NA_HEADS, NA_HEAD_DIM)
    o_na_x = neighbourhood_attention(qa_x, ka_x, va_x, ka_c, va_c, p['na_rpb'])
    qb_x, kb_x, vb_x = mla_project(xcq, xckv, xkr, p, pos_row, pos_col)
    qb_c, kb_c, vb_c = mla_project(ccq, cckv, ckr, p, None, None)
    o_mla_x = block_attention(qb_x, jnp.concatenate([kb_c, kb_x], axis=1), jnp.concatenate([vb_c, vb_x], axis=1))
    y_cv_x = conformer_conv(xa, xb, p)
    y_x = gated_merge(o_na_x, o_mla_x, y_cv_x, xg_na, xg_mla, xg_cv, p)
    if not need_ctx:
        return y_x, None
    qa_c = rms_norm(_heads(cq, NA_HEADS, NA_HEAD_DIM), p['na_q_g'])
    o_na_c = block_attention(qa_c, ka_c, va_c)
    o_mla_c = block_attention(qb_c, kb_c, vb_c)
    y_cv_c = conformer_conv(ca, cb, p)
    y_c = gated_merge(o_na_c, o_mla_c, y_cv_c, cg_na, cg_mla, cg_cv, p)
    return y_x, y_c


def swiglu(h, w_g, w_u, w_d):
    return (jax.nn.silu(h @ w_g) * (h @ w_u)) @ w_d


def moe_ffn(h, router_w, router_b, w_g, w_u, w_d, ws_g, ws_u, ws_d):
    T, D = h.shape
    n_assign = T * TOP_K
    n_blocks = -(-n_assign // MOE_BLOCK) + N_EXPERTS
    scores = jax.nn.sigmoid((h @ router_w).astype(jnp.float32))
    _, idx = lax.top_k(scores + router_b.astype(jnp.float32), TOP_K)
    sel = jnp.take_along_axis(scores, idx, axis=-1)
    gates = sel / jnp.sum(sel, axis=-1, keepdims=True) * ROUTED_SCALE
    e_flat = idx.reshape(-1)
    tok_flat = jnp.broadcast_to(jnp.arange(T, dtype=jnp.int32)[:, None], (T, TOP_K)).reshape(-1)
    order = jnp.argsort(e_flat)
    e_s = e_flat[order]
    tok_s = tok_flat[order]
    g_s = gates.reshape(-1)[order]
    counts = jnp.bincount(e_flat, length=N_EXPERTS)
    padded = (counts + MOE_BLOCK - 1) // MOE_BLOCK * MOE_BLOCK
    pad_end = jnp.cumsum(padded)
    pad_start = pad_end - padded
    start = jnp.cumsum(counts) - counts
    dest = pad_start[e_s] + jnp.arange(n_assign) - start[e_s]
    slot_tok = jnp.zeros((n_blocks * MOE_BLOCK,), jnp.int32).at[dest].set(tok_s)
    slot_gate = jnp.zeros((n_blocks * MOE_BLOCK,), h.dtype).at[dest].set(g_s.astype(h.dtype))
    block_exp = jnp.minimum(jnp.searchsorted(pad_end, jnp.arange(n_blocks) * MOE_BLOCK, side='right'), N_EXPERTS - 1)

    def add_block(acc, blk):
        tok_b, gate_b, e = blk
        y = swiglu(h[tok_b], w_g[e], w_u[e], w_d[e])
        return acc.at[tok_b].add(y * gate_b[:, None]), None

    routed, _ = lax.scan(add_block, jnp.zeros_like(h),
                         (slot_tok.reshape(n_blocks, MOE_BLOCK), slot_gate.reshape(n_blocks, MOE_BLOCK), block_exp))
    return routed + swiglu(h, ws_g, ws_u, ws_d)


def setup_inputs(seed: int = 0) -> dict:
    key = jax.random.key(seed)
    ks = iter(jax.random.split(key, 40))

    def nrm(shape, std):
        return std * jax.random.normal(next(ks), shape, jnp.float32)

    def gain(shape):
        return 1.0 + nrm(shape, 0.02)

    D, NL = D_MODEL, DEPTH
    return {
        'x': nrm((BATCH, SEQ, D), 1.0),
        'c': nrm((BATCH, D), 1.0),
        'ctx': nrm((BATCH, CTX_LEN, D), 1.0),
        'c_ctx': nrm((D,), 1.0),
        'ada_w': nrm((NL, D, 6 * D), 0.25 * D ** -0.5),
        'ada_b': nrm((NL, 6 * D), 0.02),
        'norm1_g': gain((NL, D)),
        'norm2_g': gain((NL, D)),
        'w_in': nrm((NL, D, D_IN), D ** -0.5),
        'na_q_g': gain((NL, NA_HEAD_DIM)),
        'na_k_g': gain((NL, NA_HEAD_DIM)),
        'na_rpb': nrm((NL, NA_HEADS, 2 * NA_WIN_ROWS - 1, 2 * NA_WIN_COLS - 1), 0.1),
        'w_na_o': nrm((NL, NA_W, D), NA_W ** -0.5),
        'mla_q_lora_g': gain((NL, MLA_Q_LORA)),
        'mla_w_uq': nrm((NL, MLA_Q_LORA, MLA_HEADS * MLA_QK), MLA_Q_LORA ** -0.5),
        'mla_kv_lora_g': gain((NL, MLA_KV_LORA)),
        'mla_w_ukv': nrm((NL, MLA_KV_LORA, MLA_HEADS * (MLA_NOPE + MLA_V)), MLA_KV_LORA ** -0.5),
        'mla_q_g': gain((NL, MLA_QK)),
        'mla_k_g': gain((NL, MLA_QK)),
        'w_mla_o': nrm((NL, MLA_HEADS * MLA_V, D), (MLA_HEADS * MLA_V) ** -0.5),
        'conv_w_dw': nrm((NL, CONV_WIDTH, CONV_CH), CONV_WIDTH ** -0.5),
        'conv_b_dw': nrm((NL, CONV_CH), 0.02),
        'conv_ln_g': gain((NL, CONV_CH)),
        'conv_ln_b': nrm((NL, CONV_CH), 0.02),
        'w_conv_o': nrm((NL, CONV_CH, D), CONV_CH ** -0.5),
        'w_out': nrm((NL, D, D), D ** -0.5),
        'router_w': nrm((NL, D, N_EXPERTS), D ** -0.5),
        'router_b': nrm((NL, N_EXPERTS), 0.01),
        'exp_w_gate': nrm((NL, N_EXPERTS, D, D_EXPERT), D ** -0.5),
        'exp_w_up': nrm((NL, N_EXPERTS, D, D_EXPERT), D ** -0.5),
        'exp_w_down': nrm((NL, N_EXPERTS, D_EXPERT, D), D_EXPERT ** -0.5),
        'sh_w_gate': nrm((NL, D, D_SHARED), D ** -0.5),
        'sh_w_up': nrm((NL, D, D_SHARED), D ** -0.5),
        'sh_w_down': nrm((NL, D_SHARED, D), D_SHARED ** -0.5),
    }


def reference(x, c, ctx, c_ctx, ada_w, ada_b, norm1_g, norm2_g, w_in, na_q_g, na_k_g, na_rpb, w_na_o,
              mla_q_lora_g, mla_w_uq, mla_kv_lora_g, mla_w_ukv, mla_q_g, mla_k_g, w_mla_o,
              conv_w_dw, conv_b_dw, conv_ln_g, conv_ln_b, w_conv_o, w_out,
              router_w, router_b, exp_w_gate, exp_w_up, exp_w_down, sh_w_gate, sh_w_up, sh_w_down):
    B, L, D = x.shape
    Lc = ctx.shape[1]
    pos = jnp.arange(L, dtype=jnp.int32)
    pos_row = pos // GRID_W
    pos_col = pos % GRID_W
    z = ctx
    for l in range(DEPTH):
        need_ctx = l < DEPTH - 1
        p = {
            'w_in': w_in[l], 'na_q_g': na_q_g[l], 'na_k_g': na_k_g[l], 'na_rpb': na_rpb[l], 'w_na_o': w_na_o[l],
            'mla_q_lora_g': mla_q_lora_g[l], 'mla_w_uq': mla_w_uq[l], 'mla_kv_lora_g': mla_kv_lora_g[l],
            'mla_w_ukv': mla_w_ukv[l], 'mla_q_g': mla_q_g[l], 'mla_k_g': mla_k_g[l], 'w_mla_o': w_mla_o[l],
            'conv_w_dw': conv_w_dw[l], 'conv_b_dw': conv_b_dw[l], 'conv_ln_g': conv_ln_g[l],
            'conv_ln_b': conv_ln_b[l], 'w_conv_o': w_conv_o[l], 'w_out': w_out[l],
        }
        mod_x = (jax.nn.silu(c) @ ada_w[l] + ada_b[l])[:, None, :]
        mod_c = (jax.nn.silu(c_ctx) @ ada_w[l] + ada_b[l])[None, None, :]
        shx1, scx1, gx1, shx2, scx2, gx2 = jnp.split(mod_x, 6, axis=-1)
        shc1, scc1, gc1, shc2, scc2, gc2 = jnp.split(mod_c, 6, axis=-1)
        hx = rms_norm(x, norm1_g[l]) * (1 + scx1) + shx1
        hc = rms_norm(z, norm1_g[l]) * (1 + scc1) + shc1
        yx, yc = hybrid_mixer(hx, hc, p, pos_row, pos_col, need_ctx)
        x = x + gx1 * yx
        hx2 = rms_norm(x, norm2_g[l]) * (1 + scx2) + shx2
        if need_ctx:
            z = z + gc1 * yc
            hc2 = rms_norm(z, norm2_g[l]) * (1 + scc2) + shc2
            tokens = jnp.concatenate([hc2.reshape(B * Lc, D), hx2.reshape(B * L, D)], axis=0)
            f = moe_ffn(tokens, router_w[l], router_b[l], exp_w_gate[l], exp_w_up[l], exp_w_down[l],
                        sh_w_gate[l], sh_w_up[l], sh_w_down[l])
            z = z + gc2 * f[:B * Lc].reshape(B, Lc, D)
            x = x + gx2 * f[B * Lc:].reshape(B, L, D)
        else:
            f = moe_ffn(hx2.reshape(B * L, D), router_w[l], router_b[l], exp_w_gate[l], exp_w_up[l],
                        exp_w_down[l], sh_w_gate[l], sh_w_up[l], sh_w_down[l])
            x = x + gx2 * f.reshape(B, L, D)
    return x
```

```python
import functools

import numpy as np
import jax
import jax.numpy as jnp
from jax import lax
from jax.experimental import pallas as pl
from jax.experimental.pallas import tpu as pltpu

F32 = jnp.float32
BF16 = jnp.bfloat16

GRID_W = 64
EPS = 1e-6
NEG_INF = -1e30
NA_HEADS = 16
NA_HEAD_DIM = 64
NA_W = NA_HEADS * NA_HEAD_DIM
NA_WIN_ROWS = 8
NA_WIN_COLS = 16
MLA_HEADS = 16
MLA_NOPE = 128
MLA_ROPE = 64
MLA_QK = MLA_NOPE + MLA_ROPE
MLA_V = 128
MLA_Q_LORA = 768
MLA_KV_LORA = 512
ROPE_THETA = 10000.0
CONV_CH = 1024
CONV_WIDTH = 31
CONV_HALO = 16
N_EXPERTS = 64
TOP_K = 8
D_EXPERT = 512
ROUTED_SCALE = 2.5
MOE_BLOCK = 256
LANES = 128
MLA_HEAD_PAD = 2 * LANES
VMEM_LIMIT = 56 * 1024 * 1024

P1_Q, P1_K, P1_V, P1_A, P1_B, P1_GNA, P1_GMLA, P1_GCV = 0, 1024, 2048, 3072, 4096, 5120, 7168, 9216
P1_W = 11264
P2_CQ, P2_KR, P2_KRS, P2_CKV = 0, 768, 896, 1024
P2_W = 1536


def _cparams(n_axes):
    return pltpu.CompilerParams(dimension_semantics=("arbitrary",) * n_axes, vmem_limit_bytes=VMEM_LIMIT)


def _dot(a, b):
    return jnp.dot(a, b, preferred_element_type=F32)


def _dot_nt(a, b):
    return lax.dot_general(a, b, (((1,), (1,)), ((), ())), preferred_element_type=F32)


def _dot_hilo(x, m):
    hi = x.astype(BF16)
    lo = (x - hi.astype(F32)).astype(BF16)
    return _dot(hi, m) + _dot(lo, m)


def _ada_kernel(c_ref, w_ref, b_ref, o_ref):
    c = c_ref[...]
    a = (c * jax.nn.sigmoid(c)).astype(BF16)
    o_ref[...] = _dot(a, w_ref[...].astype(BF16)) + b_ref[...]


def _ada(cc, ada_w, ada_b):
    nl, d, n = ada_w.shape
    r = cc.shape[0]
    tn = 1024
    return pl.pallas_call(
        _ada_kernel,
        out_shape=jax.ShapeDtypeStruct((nl, r, n), F32),
        grid=(nl, n // tn),
        in_specs=[pl.BlockSpec((r, d), lambda l, j: (0, 0)),
                  pl.BlockSpec((None, d, tn), lambda l, j: (l, 0, j)),
                  pl.BlockSpec((None, 1, tn), lambda l, j: (l, 0, j))],
        out_specs=pl.BlockSpec((None, r, tn), lambda l, j: (l, 0, j)),
        compiler_params=_cparams(2), name="ada_mod",
    )(cc, ada_w, ada_b.reshape(nl, 1, n))


def _norm_mod_kernel(x_ref, g_ref, sc_ref, sh_ref, o_ref):
    x = x_ref[...]
    y = x * lax.rsqrt(jnp.mean(x * x, axis=-1, keepdims=True) + EPS) * g_ref[...]
    o_ref[...] = (y * (1.0 + sc_ref[...]) + sh_ref[...]).astype(o_ref.dtype)


def _norm_mod(x, g, mod3, mod_row, k_sc, k_sh, tm):
    m, d = x.shape
    return pl.pallas_call(
        _norm_mod_kernel,
        out_shape=jax.ShapeDtypeStruct((m, d), BF16),
        grid=(m // tm,),
        in_specs=[pl.BlockSpec((tm, d), lambda i: (i, 0)),
                  pl.BlockSpec((1, d), lambda i: (0, 0)),
                  pl.BlockSpec((None, 1, d), lambda i: (mod_row(i) * 6 + k_sc, 0, 0)),
                  pl.BlockSpec((None, 1, d), lambda i: (mod_row(i) * 6 + k_sh, 0, 0))],
        out_specs=pl.BlockSpec((tm, d), lambda i: (i, 0)),
        compiler_params=_cparams(1), name="norm_mod",
    )(x, g.reshape(1, d), mod3, mod3)


def _mm_kernel(a_ref, w_ref, o_ref):
    o_ref[...] = _dot(a_ref[...], w_ref[...]).astype(o_ref.dtype)


def _matmul(a, w, out_dtype, tm, tn, name):
    m, k = a.shape
    n = w.shape[1]
    return pl.pallas_call(
        _mm_kernel,
        out_shape=jax.ShapeDtypeStruct((m, n), out_dtype),
        grid=(m // tm, n // tn),
        in_specs=[pl.BlockSpec((tm, k), lambda i, j: (i, 0)),
                  pl.BlockSpec((k, tn), lambda i, j: (0, j))],
        out_specs=pl.BlockSpec((tm, tn), lambda i, j: (i, j)),
        compiler_params=_cparams(2), name=name,
    )(a, w)


def _headnorm_kernel(x_ref, g_ref, bd_ref, o_ref):
    bd = bd_ref[...]
    for c in range(x_ref.shape[1] // LANES):
        sl = slice(c * LANES, (c + 1) * LANES)
        x = x_ref[:, sl].astype(F32)
        ss = _dot_hilo(x * x, bd)
        y = x * lax.rsqrt(ss * (1.0 / NA_HEAD_DIM) + EPS) * g_ref[:, sl]
        o_ref[:, sl] = y.astype(o_ref.dtype)


def _headnorm(p1, g_row, bd, tm):
    m = p1.shape[0]
    w = g_row.shape[1]
    return pl.pallas_call(
        _headnorm_kernel,
        out_shape=jax.ShapeDtypeStruct((m, w), BF16),
        grid=(m // tm,),
        in_specs=[pl.BlockSpec((tm, w), lambda i: (i, 0)),
                  pl.BlockSpec((1, w), lambda i: (0, 0)),
                  pl.BlockSpec((LANES, LANES), lambda i: (0, 0))],
        out_specs=pl.BlockSpec((tm, w), lambda i: (i, 0)),
        compiler_params=_cparams(1), name="na_headnorm",
    )(p1, g_row, bd)


def _softmax_pv(scores, values):
    m = scores[0].max(axis=-1, keepdims=True)
    for s in scores[1:]:
        m = jnp.maximum(m, s.max(axis=-1, keepdims=True))
    l = None
    o = None
    for s, v in zip(scores, values):
        p = jnp.exp(s - m)
        ps = p.sum(axis=-1, keepdims=True)
        po = _dot(p.astype(BF16), v)
        l = ps if l is None else l + ps
        o = po if o is None else o + po
    return o / l


def _na_kernel(q_ref, kx_ref, vx_ref, kc_ref, vc_ref, bias_ref, o_ref):
    rows = kx_ref.shape[0] // GRID_W
    r = pl.program_id(1)
    r0 = jnp.clip(r - NA_WIN_ROWS // 2, 0, rows - NA_WIN_ROWS)
    kstart = pl.multiple_of(r0 * GRID_W, GRID_W)
    nwin = NA_WIN_ROWS * GRID_W
    lo = lax.broadcasted_iota(jnp.int32, (GRID_W, LANES), 1) < NA_HEAD_DIM
    zero = jnp.zeros((GRID_W, LANES), q_ref.dtype)
    for p in range(NA_W // LANES):
        sl = slice(p * LANES, (p + 1) * LANES)
        q2 = q_ref[:, sl]
        kw = kx_ref[pl.ds(kstart, nwin), sl]
        vw = vx_ref[pl.ds(kstart, nwin), sl]
        kc = kc_ref[:, sl]
        vc = vc_ref[:, sl]
        outs = []
        for hh in range(2):
            qm = jnp.where(lo, q2, zero) if hh == 0 else jnp.where(lo, zero, q2)
            s_w = _dot_nt(qm, kw) + bias_ref[2 * p + hh]
            s_c = _dot_nt(qm, kc)
            outs.append(_softmax_pv([s_w, s_c], [vw, vc]))
        o_ref[:, sl] = jnp.where(lo, outs[0], outs[1]).astype(o_ref.dtype)


def _na_attention(qk_x, p1_x, qk_c, p1_c, bias_tab, batch):
    l = qk_x.shape[0] // batch
    lc = qk_c.shape[0] // batch
    rows = l // GRID_W
    nwin = NA_WIN_ROWS * GRID_W

    def bias_idx(b, r):
        r0 = jnp.clip(r - NA_WIN_ROWS // 2, 0, rows - NA_WIN_ROWS)
        return (r - r0, 0, 0, 0)

    return pl.pallas_call(
        _na_kernel,
        out_shape=jax.ShapeDtypeStruct((batch * l, NA_W), BF16),
        grid=(batch, rows),
        in_specs=[pl.BlockSpec((GRID_W, NA_W), lambda b, r: (b * rows + r, 0)),
                  pl.BlockSpec((l, NA_W), lambda b, r: (b, 1)),
                  pl.BlockSpec((l, NA_W), lambda b, r: (b, P1_V // NA_W)),
                  pl.BlockSpec((lc, NA_W), lambda b, r: (b, 1)),
                  pl.BlockSpec((lc, NA_W), lambda b, r: (b, P1_V // NA_W)),
                  pl.BlockSpec((None, NA_HEADS, GRID_W, nwin), bias_idx)],
        out_specs=pl.BlockSpec((GRID_W, NA_W), lambda b, r: (b * rows + r, 0)),
        compiler_params=_cparams(2), name="na_attention",
    )(qk_x, qk_x, p1_x, qk_c, p1_c, bias_tab)


def _na_ctx_kernel(q_ref, k_ref, v_ref, o_ref):
    n = q_ref.shape[0]
    lo = lax.broadcasted_iota(jnp.int32, (n, LANES), 1) < NA_HEAD_DIM
    zero = jnp.zeros((n, LANES), q_ref.dtype)
    for p in range(NA_W // LANES):
        sl = slice(p * LANES, (p + 1) * LANES)
        q2 = q_ref[:, sl]
        k = k_ref[:, sl]
        v = v_ref[:, sl]
        outs = []
        for hh in range(2):
            qm = jnp.where(lo, q2, zero) if hh == 0 else jnp.where(lo, zero, q2)
            outs.append(_softmax_pv([_dot_nt(qm, k)], [v]))
        o_ref[:, sl] = jnp.where(lo, outs[0], outs[1]).astype(o_ref.dtype)


def _na_ctx_attention(qk_c, p1_c, batch):
    lc = qk_c.shape[0] // batch
    return pl.pallas_call(
        _na_ctx_kernel,
        out_shape=jax.ShapeDtypeStruct((batch * lc, NA_W), BF16),
        grid=(batch,),
        in_specs=[pl.BlockSpec((lc, NA_W), lambda b: (b, 0)),
                  pl.BlockSpec((lc, NA_W), lambda b: (b, 1)),
                  pl.BlockSpec((lc, NA_W), lambda b: (b, P1_V // NA_W))],
        out_specs=pl.BlockSpec((lc, NA_W), lambda b: (b, 0)),
        compiler_params=_cparams(1), name="na_ctx_attention",
    )(qk_c, qk_c, p1_c)


def _na_bias_table(rpb):
    cls = np.arange(NA_WIN_ROWS)[:, None, None, None]
    qc = np.arange(GRID_W)[None, :, None, None]
    w = np.arange(NA_WIN_ROWS)[None, None, :, None]
    kc = np.arange(GRID_W)[None, None, None, :]
    c0 = np.clip(qc - NA_WIN_COLS // 2, 0, GRID_W - NA_WIN_COLS)
    ok = (kc >= c0) & (kc < c0 + NA_WIN_COLS)
    dr = np.broadcast_to(w - cls + NA_WIN_ROWS - 1, (NA_WIN_ROWS, GRID_W, NA_WIN_ROWS, GRID_W))
    dc = np.broadcast_to(np.clip(kc - qc + NA_WIN_COLS - 1, 0, 2 * NA_WIN_COLS - 2), dr.shape)
    ok = np.broadcast_to(ok, dr.shape)
    tab = rpb.astype(F32)[:, dr, dc]
    tab = jnp.where(ok[None], tab, NEG_INF)
    return jnp.transpose(tab, (1, 0, 2, 3, 4)).reshape(NA_WIN_ROWS, NA_HEADS, GRID_W, NA_WIN_ROWS * GRID_W)


def _mla_q_kernel(p2_ref, gl_ref, w_ref, gn_ref, cg_ref, sg_ref, ones_ref, e0_ref, e1_ref, o_ref):
    cq = p2_ref[...].astype(F32)
    cqn = cq * lax.rsqrt(jnp.mean(cq * cq, axis=-1, keepdims=True) + EPS) * gl_ref[...]
    y = _dot(cqn.astype(BF16), w_ref[...])
    tm = y.shape[0]
    lo = lax.broadcasted_iota(jnp.int32, (tm, LANES), 1) < MLA_ROPE
    nope_w = MLA_HEADS * MLA_NOPE
    rope_w = MLA_HEADS * MLA_ROPE
    ones, e0, e1 = ones_ref[...], e0_ref[...], e1_ref[...]
    gn = gn_ref[...]
    for p in range(MLA_HEADS // 2):
        yr = y[:, nope_w + p * LANES: nope_w + (p + 1) * LANES]
        yrs = y[:, nope_w + rope_w + p * LANES: nope_w + rope_w + (p + 1) * LANES]
        rot = yr * cg_ref[...] + yrs * sg_ref[...]
        yr2 = yr * yr
        ssr = (_dot_hilo(yr2, e0), _dot_hilo(yr2, e1))
        for hh in range(2):
            h = 2 * p + hh
            nope = y[:, h * MLA_NOPE:(h + 1) * MLA_NOPE]
            tot = _dot_hilo(nope * nope, ones) + ssr[hh]
            scale = lax.rsqrt(tot * (1.0 / MLA_QK) + EPS)
            o_ref[:, h * MLA_HEAD_PAD: h * MLA_HEAD_PAD + LANES] = (nope * scale * gn).astype(o_ref.dtype)
            rs = rot * scale
            rs = jnp.where(lo, rs, 0.0) if hh == 0 else jnp.where(lo, 0.0, rs)
            o_ref[:, h * MLA_HEAD_PAD + LANES: (h + 1) * MLA_HEAD_PAD] = rs.astype(o_ref.dtype)


def _mla_q(p2, g_lora, w_uq, g_nope, cg, sg, mats, seq, tm):
    m = p2.shape[0]
    nt = seq // tm
    ones, e0, e1 = mats
    full = lambda shape: pl.BlockSpec(shape, lambda i: (0,) * len(shape))
    return pl.pallas_call(
        _mla_q_kernel,
        out_shape=jax.ShapeDtypeStruct((m, MLA_HEADS * MLA_HEAD_PAD), BF16),
        grid=(m // tm,),
        in_specs=[pl.BlockSpec((tm, MLA_Q_LORA), lambda i: (i, 0)),
                  full((1, MLA_Q_LORA)),
                  full(w_uq.shape),
                  full((1, LANES)),
                  pl.BlockSpec((tm, LANES), lambda i: (i % nt, 0)),
                  pl.BlockSpec((tm, LANES), lambda i: (i % nt, 0)),
                  full((LANES, LANES)), full((LANES, LANES)), full((LANES, LANES))],
        out_specs=pl.BlockSpec((tm, MLA_HEADS * MLA_HEAD_PAD), lambda i: (i, 0)),
        compiler_params=_cparams(1), name="mla_q_proj",
    )(p2, g_lora, w_uq, g_nope, cg, sg, ones, e0, e1)


def _mla_kv_kernel(ckv_ref, kr_ref, krs_ref, gl_ref, w_ref, gn_ref, cg_ref, sg_ref, ones_ref, e0_ref, k_ref, v_ref):
    ckv = ckv_ref[...].astype(F32)
    cn = ckv * lax.rsqrt(jnp.mean(ckv * ckv, axis=-1, keepdims=True) + EPS) * gl_ref[...]
    y = _dot(cn.astype(BF16), w_ref[...])
    tm = y.shape[0]
    lo = lax.broadcasted_iota(jnp.int32, (tm, LANES), 1) < MLA_ROPE
    nope_w = MLA_HEADS * MLA_NOPE
    v_ref[...] = y[:, nope_w:].astype(v_ref.dtype)
    kr = kr_ref[...].astype(F32)
    rot = kr * cg_ref[...] + krs_ref[...].astype(F32) * sg_ref[...]
    ssr = _dot_hilo(kr * kr, e0_ref[...])
    ones = ones_ref[...]
    gn = gn_ref[...]
    for h in range(MLA_HEADS):
        nope = y[:, h * MLA_NOPE:(h + 1) * MLA_NOPE]
        tot = _dot_hilo(nope * nope, ones) + ssr
        scale = lax.rsqrt(tot * (1.0 / MLA_QK) + EPS)
        k_ref[:, h * MLA_HEAD_PAD: h * MLA_HEAD_PAD + LANES] = (nope * scale * gn).astype(k_ref.dtype)
        rs = rot * scale
        rs = jnp.where(lo, rs, 0.0) if h % 2 == 0 else jnp.where(lo, 0.0, rs)
        k_ref[:, h * MLA_HEAD_PAD + LANES: (h + 1) * MLA_HEAD_PAD] = rs.astype(k_ref.dtype)


def _mla_kv(p2, g_lora, w_ukv, g_nope, cg, sg, mats, seq, tm):
    m = p2.shape[0]
    nt = seq // tm
    ones, e0, _ = mats
    full = lambda shape: pl.BlockSpec(shape, lambda i: (0,) * len(shape))
    return pl.pallas_call(
        _mla_kv_kernel,
        out_shape=(jax.ShapeDtypeStruct((m, MLA_HEADS * MLA_HEAD_PAD), BF16),
                   jax.ShapeDtypeStruct((m, MLA_HEADS * MLA_V), BF16)),
        grid=(m // tm,),
        in_specs=[pl.BlockSpec((tm, MLA_KV_LORA), lambda i: (i, P2_CKV // MLA_KV_LORA)),
                  pl.BlockSpec((tm, LANES), lambda i: (i, P2_KR // LANES)),
                  pl.BlockSpec((tm, LANES), lambda i: (i, P2_KRS // LANES)),
                  full((1, MLA_KV_LORA)),
                  full(w_ukv.shape),
                  full((1, LANES)),
                  pl.BlockSpec((tm, LANES), lambda i: (i % nt, 0)),
                  pl.BlockSpec((tm, LANES), lambda i: (i % nt, 0)),
                  full((LANES, LANES)), full((LANES, LANES))],
        out_specs=(pl.BlockSpec((tm, MLA_HEADS * MLA_HEAD_PAD), lambda i: (i, 0)),
                   pl.BlockSpec((tm, MLA_HEADS * MLA_V), lambda i: (i, 0))),
        compiler_params=_cparams(1), name="mla_kv_proj",
    )(p2, p2, p2, g_lora, w_ukv, g_nope, cg, sg, ones, e0)


def _mla_attn_kernel(*refs, n_kv):
    q_ref = refs[0]
    kv = refs[1:1 + 2 * n_kv]
    o_ref = refs[1 + 2 * n_kv]
    q = q_ref[...]
    scores = [_dot_nt(q, kv[2 * i][...]) for i in range(n_kv)]
    values = [kv[2 * i + 1][...] for i in range(n_kv)]
    o_ref[...] = _softmax_pv(scores, values).astype(o_ref.dtype)


def _mla_attention(q, kvs, batch, tq):
    lq = q.shape[0] // batch
    nq = lq // tq
    in_specs = [pl.BlockSpec((tq, MLA_HEAD_PAD), lambda b, h, j: (b * nq + j, h))]
    args = [q]
    for k, v in kvs:
        s = k.shape[0] // batch
        in_specs.append(pl.BlockSpec((s, MLA_HEAD_PAD), lambda b, h, j: (b, h)))
        in_specs.append(pl.BlockSpec((s, MLA_V), lambda b, h, j: (b, h)))
        args += [k, v]
    return pl.pallas_call(
        functools.partial(_mla_attn_kernel, n_kv=len(kvs)),
        out_shape=jax.ShapeDtypeStruct((batch * lq, MLA_HEADS * MLA_V), BF16),
        grid=(batch, MLA_HEADS, nq),
        in_specs=in_specs,
        out_specs=pl.BlockSpec((tq, MLA_V), lambda b, h, j: (b * nq + j, h)),
        compiler_params=_cparams(3), name="mla_attention",
    )(*args)


def _conv_kernel(ap_ref, a_ref, an_ref, bp_ref, b_ref, bn_ref, w_ref, cb_ref, g_ref, beta_ref, o_ref, u_scr, y_scr,
                 *, n_tiles):
    j = pl.program_id(1)
    tt = a_ref.shape[0]
    glu = lambda a, b: a[...].astype(F32) * jax.nn.sigmoid(b[...].astype(F32))
    u_scr[0:CONV_HALO, :] = jnp.where(j > 0, glu(ap_ref, bp_ref), 0.0)
    u_scr[CONV_HALO:CONV_HALO + tt, :] = glu(a_ref, b_ref)
    u_scr[CONV_HALO + tt:, :] = jnp.where(j < n_tiles - 1, glu(an_ref, bn_ref), 0.0)
    off = CONV_HALO - CONV_WIDTH // 2
    for c in range(CONV_CH // LANES):
        sl = slice(c * LANES, (c + 1) * LANES)
        acc = jnp.zeros((tt, LANES), F32)
        for k in range(CONV_WIDTH):
            acc = acc + u_scr[off + k: off + k + tt, sl] * w_ref[k:k + 1, sl]
        y_scr[:, sl] = acc + cb_ref[:, sl]
    y = y_scr[...]
    mu = jnp.mean(y, axis=-1, keepdims=True)
    d = y - mu
    var = jnp.mean(d * d, axis=-1, keepdims=True)
    z = d * lax.rsqrt(var + EPS) * g_ref[...] + beta_ref[...]
    o_ref[...] = (z * jax.nn.sigmoid(z)).astype(o_ref.dtype)


def _conv_module(p1, w_dw, b_dw, ln_g, ln_b, seq, tt):
    m = p1.shape[0]
    batch = m // seq
    n_tiles = seq // tt
    hb = tt // CONV_HALO
    n_hblk = m // CONV_HALO
    ca, cb = P1_A // CONV_CH, P1_B // CONV_CH

    def prev_idx(c):
        return lambda b, j: (jnp.maximum((b * n_tiles + j) * hb - 1, 0), c)

    def next_idx(c):
        return lambda b, j: (jnp.minimum((b * n_tiles + j + 1) * hb, n_hblk - 1), c)

    cur = lambda c: (lambda b, j: (b * n_tiles + j, c))
    full = lambda shape: pl.BlockSpec(shape, lambda b, j: (0,) * len(shape))
    return pl.pallas_call(
        functools.partial(_conv_kernel, n_tiles=n_tiles),
        out_shape=jax.ShapeDtypeStruct((m, CONV_CH), BF16),
        grid=(batch, n_tiles),
        in_specs=[pl.BlockSpec((CONV_HALO, CONV_CH), prev_idx(ca)),
                  pl.BlockSpec((tt, CONV_CH), cur(ca)),
                  pl.BlockSpec((CONV_HALO, CONV_CH), next_idx(ca)),
                  pl.BlockSpec((CONV_HALO, CONV_CH), prev_idx(cb)),
                  pl.BlockSpec((tt, CONV_CH), cur(cb)),
                  pl.BlockSpec((CONV_HALO, CONV_CH), next_idx(cb)),
                  full((32, CONV_CH)), full((1, CONV_CH)), full((1, CONV_CH)), full((1, CONV_CH))],
        out_specs=pl.BlockSpec((tt, CONV_CH), lambda b, j: (b * n_tiles + j, 0)),
        scratch_shapes=[pltpu.VMEM((tt + 2 * CONV_HALO, CONV_CH), F32), pltpu.VMEM((tt, CONV_CH), F32)],
        compiler_params=_cparams(2), name="conv_module",
    )(p1, p1, p1, p1, p1, p1, w_dw, b_dw, ln_g, ln_b)


def _merge_kernel(ona_ref, omla_ref, u_ref, gna_ref, gmla_ref, gcv_ref, wna_ref, wmla_ref, wcv_ref, o_ref):
    sig = lambda r: jax.nn.sigmoid(r[...].astype(F32))
    m = sig(gna_ref) * _dot(ona_ref[...], wna_ref[...])
    m = m + sig(gmla_ref) * _dot(omla_ref[...], wmla_ref[...])
    m = m + sig(gcv_ref) * _dot(u_ref[...], wcv_ref[...])
    o_ref[...] = m.astype(o_ref.dtype)


def _merge(o_na, o_mla, u, p1, w_na_o, w_mla_o, w_conv_o, tm, tn):
    m = o_na.shape[0]
    d = w_na_o.shape[1]
    tm = min(tm, m)
    row = lambda k: pl.BlockSpec((tm, k), lambda i, j: (i, 0))
    gate = lambda off: pl.BlockSpec((tm, tn), lambda i, j: (i, off // tn + j))
    wcol = lambda k: pl.BlockSpec((k, tn), lambda i, j: (0, j))
    return pl.pallas_call(
        _merge_kernel,
        out_shape=jax.ShapeDtypeStruct((m, d), BF16),
        grid=(m // tm, d // tn),
        in_specs=[row(o_na.shape[1]), row(o_mla.shape[1]), row(u.shape[1]),
                  gate(P1_GNA), gate(P1_GMLA), gate(P1_GCV),
                  wcol(w_na_o.shape[0]), wcol(w_mla_o.shape[0]), wcol(w_conv_o.shape[0])],
        out_specs=pl.BlockSpec((tm, tn), lambda i, j: (i, j)),
        compiler_params=_cparams(2), name="gated_merge",
    )(o_na, o_mla, u, p1, p1, p1, w_na_o, w_mla_o, w_conv_o)


def _out_proj_kernel(m_ref, w_ref, x_ref, gate_ref, g2_ref, sc_ref, sh_ref, xo_ref, ho_ref):
    x = x_ref[...] + gate_ref[...] * _dot(m_ref[...], w_ref[...])
    xo_ref[...] = x
    y = x * lax.rsqrt(jnp.mean(x * x, axis=-1, keepdims=True) + EPS) * g2_ref[...]
    ho_ref[...] = (y * (1.0 + sc_ref[...]) + sh_ref[...]).astype(ho_ref.dtype)


def _out_proj(mrg, w_out, x, mod3, mod_row, g2, tm):
    m, d = x.shape
    modspec = lambda k: pl.BlockSpec((None, 1, d), lambda i: (mod_row(i) * 6 + k, 0, 0))
    return pl.pallas_call(
        _out_proj_kernel,
        out_shape=(jax.ShapeDtypeStruct((m, d), F32), jax.ShapeDtypeStruct((m, d), BF16)),
        grid=(m // tm,),
        in_specs=[pl.BlockSpec((tm, d), lambda i: (i, 0)),
                  pl.BlockSpec((d, d), lambda i: (0, 0)),
                  pl.BlockSpec((tm, d), lambda i: (i, 0)),
                  modspec(2),
                  pl.BlockSpec((1, d), lambda i: (0, 0)),
                  modspec(4), modspec(3)],
        out_specs=(pl.BlockSpec((tm, d), lambda i: (i, 0)), pl.BlockSpec((tm, d), lambda i: (i, 0))),
        compiler_params=_cparams(1), name="out_proj_residual",
    )(mrg, w_out, x, mod3, g2.reshape(1, d), mod3, mod3)


def _router_kernel(h_ref, w_ref, b_ref, idx_ref, gate_ref):
    scores = jax.nn.sigmoid(_dot(h_ref[...], w_ref[...]))
    tm = scores.shape[0]
    lane = lax.broadcasted_iota(jnp.int32, (tm, LANES), 1).astype(F32)
    sel = jnp.where(lane < N_EXPERTS, scores + b_ref[...], -jnp.inf)
    idx_out = jnp.zeros((tm, LANES), F32)
    val_out = jnp.zeros((tm, LANES), F32)
    total = jnp.zeros((tm, 1), F32)
    for k in range(TOP_K):
        mx = sel.max(axis=-1, keepdims=True)
        ix = jnp.where(sel == mx, lane, float(LANES)).min(axis=-1, keepdims=True)
        hit = lane == ix
        val = jnp.where(hit, scores, 0.0).sum(axis=-1, keepdims=True)
        sel = jnp.where(hit, -jnp.inf, sel)
        idx_out = jnp.where(lane == k, ix, idx_out)
        val_out = jnp.where(lane == k, val, val_out)
        total = total + val
    idx_ref[...] = idx_out.astype(jnp.int32)
    gate_ref[...] = val_out / total * ROUTED_SCALE


def _router(h, rw, rb, tm):
    t, d = h.shape
    return pl.pallas_call(
        _router_kernel,
        out_shape=(jax.ShapeDtypeStruct((t, LANES), jnp.int32), jax.ShapeDtypeStruct((t, LANES), F32)),
        grid=(t // tm,),
        in_specs=[pl.BlockSpec((tm, d), lambda i: (i, 0)),
                  pl.BlockSpec((d, LANES), lambda i: (0, 0)),
                  pl.BlockSpec((1, LANES), lambda i: (0, 0))],
        out_specs=(pl.BlockSpec((tm, LANES), lambda i: (i, 0)), pl.BlockSpec((tm, LANES), lambda i: (i, 0))),
        compiler_params=_cparams(1), name="moe_router",
    )(h, rw, rb)


def _expert_kernel(be_ref, nb_ref, x_ref, gate_ref, wg_ref, wu_ref, wd_ref, o_ref):
    i = pl.program_id(0)

    @pl.when(i < nb_ref[0])
    def _():
        x = x_ref[...]
        a = _dot(x, wg_ref[...])
        a = a * jax.nn.sigmoid(a) * _dot(x, wu_ref[...])
        o_ref[...] = (_dot(a.astype(BF16), wd_ref[...]) * gate_ref[...]).astype(o_ref.dtype)

    @pl.when(i >= nb_ref[0])
    def _():
        o_ref[...] = jnp.zeros(o_ref.shape, o_ref.dtype)


def _expert_ffn(xg, gate, block_exp, n_used, wg, wu, wd, tb):
    n, d = xg.shape
    de = wg.shape[2]
    return pl.pallas_call(
        _expert_kernel,
        out_shape=jax.ShapeDtypeStruct((n, d), BF16),
        grid_spec=pltpu.PrefetchScalarGridSpec(
            num_scalar_prefetch=2,
            grid=(n // tb,),
            in_specs=[pl.BlockSpec((tb, d), lambda i, be, nb: (i, 0)),
                      pl.BlockSpec((tb, 1), lambda i, be, nb: (i, 0)),
                      pl.BlockSpec((None, d, de), lambda i, be, nb: (be[i], 0, 0)),
                      pl.BlockSpec((None, d, de), lambda i, be, nb: (be[i], 0, 0)),
                      pl.BlockSpec((None, de, d), lambda i, be, nb: (be[i], 0, 0))],
            out_specs=pl.BlockSpec((tb, d), lambda i, be, nb: (i, 0))),
        compiler_params=_cparams(1), name="expert_ffn",
    )(block_exp, n_used, xg, gate, wg, wu, wd)


def _moe_residual_kernel(x_ref, gate_ref, r_ref, s_ref, o_ref):
    o_ref[...] = x_ref[...] + gate_ref[...] * (r_ref[...] + s_ref[...].astype(F32))


def _moe_residual(x, mod3, mod_row, routed, shared, tm):
    m, d = x.shape
    spec = pl.BlockSpec((tm, d), lambda i: (i, 0))
    return pl.pallas_call(
        _moe_residual_kernel,
        out_shape=jax.ShapeDtypeStruct((m, d), F32),
        grid=(m // tm,),
        in_specs=[spec, pl.BlockSpec((None, 1, d), lambda i: (mod_row(i) * 6 + 5, 0, 0)), spec, spec],
        out_specs=spec,
        compiler_params=_cparams(1), name="moe_residual",
    )(x, mod3, routed, shared)


def _moe(h, rw, rb, wg, wu, wd, sg, su, sd):
    t, d = h.shape
    idx_p, gate_p = _router(h, rw, rb, 512)
    idx = idx_p[:, :TOP_K]
    gates = gate_p[:, :TOP_K]
    n_assign = t * TOP_K
    n_blocks = n_assign // MOE_BLOCK + N_EXPERTS
    onehot = (idx[:, :, None] == jnp.arange(N_EXPERTS, dtype=jnp.int32)).any(axis=1).astype(jnp.int32)
    csum = jnp.cumsum(onehot, axis=0)
    counts = csum[-1]
    rank = jnp.take_along_axis(csum - onehot, idx, axis=1)
    padded = (counts + MOE_BLOCK - 1) // MOE_BLOCK * MOE_BLOCK
    pad_end = jnp.cumsum(padded)
    pad_start = pad_end - padded
    dest = pad_start[idx] + rank
    tok = jnp.broadcast_to(jnp.arange(t, dtype=jnp.int32)[:, None], (t, TOP_K))
    n_slots = n_blocks * MOE_BLOCK
    slot_tok = jnp.zeros((n_slots,), jnp.int32).at[dest.reshape(-1)].set(tok.reshape(-1))
    slot_gate = jnp.zeros((n_slots,), F32).at[dest.reshape(-1)].set(gates.reshape(-1))
    block_exp = jnp.minimum(jnp.searchsorted(pad_end, jnp.arange(n_blocks, dtype=jnp.int32) * MOE_BLOCK,
                                             side='right'), N_EXPERTS - 1).astype(jnp.int32)
    n_used = (pad_end[-1] // MOE_BLOCK).astype(jnp.int32).reshape(1)
    xg = jnp.take(h, slot_tok, axis=0)
    y = _expert_ffn(xg, slot_gate.reshape(n_slots, 1), block_exp, n_used, wg, wu, wd, MOE_BLOCK)
    routed = jnp.take(y, dest.reshape(-1), axis=0).reshape(t, TOP_K, d).astype(F32).sum(axis=1)
    ones_gate = jnp.ones((t, 1), F32)
    zeros_exp = jnp.zeros((t // MOE_BLOCK,), jnp.int32)
    shared = _expert_ffn(h, ones_gate, zeros_exp, jnp.full((1,), t // MOE_BLOCK, jnp.int32),
                         sg[None], su[None], sd[None], MOE_BLOCK)
    return routed, shared


def _rope_tables(seq_len, gain):
    nf = MLA_ROPE // 4
    inv = ROPE_THETA ** (-jnp.arange(nf, dtype=F32) / nf)
    pos = jnp.arange(seq_len, dtype=jnp.int32)
    ang_r = (pos // GRID_W).astype(F32)[:, None] * inv
    ang_c = (pos % GRID_W).astype(F32)[:, None] * inv
    cos = jnp.concatenate([jnp.cos(ang_r)] * 2 + [jnp.cos(ang_c)] * 2, axis=-1)
    sin = jnp.concatenate([-jnp.sin(ang_r), jnp.sin(ang_r), -jnp.sin(ang_c), jnp.sin(ang_c)], axis=-1)
    g = gain.astype(F32)
    cg = cos * g[None]
    sg = sin * g[_ROPE_PARTNER][None]
    return jnp.tile(cg, (1, 2)), jnp.tile(sg, (1, 2))


def _rope_partner():
    nf = MLA_ROPE // 4
    idx = np.arange(MLA_ROPE)
    within = idx % (2 * nf)
    return np.where(within < nf, idx + nf, idx - nf)


_ROPE_PARTNER = _rope_partner()


def _in_splits():
    sizes = (NA_W, NA_W, NA_W, MLA_Q_LORA, MLA_KV_LORA, MLA_ROPE, CONV_CH, CONV_CH)
    offs = np.concatenate([[0], np.cumsum(sizes)])
    return {n: int(o) for n, o in zip(("q", "k", "v", "cq", "ckv", "kr", "a", "b", "g"), offs)}


def _prep_layer(l, dmodel, w_in, na_q_g, na_k_g, na_rpb, w_na_o, mla_q_lora_g, mla_w_uq, mla_kv_lora_g, mla_w_ukv,
                mla_q_g, mla_k_g, w_mla_o, conv_w_dw, conv_b_dw, conv_ln_g, conv_ln_b, w_conv_o, w_out,
                router_w, router_b, exp_w_gate, exp_w_up, exp_w_down, sh_w_gate, sh_w_up, sh_w_down, seq_len):
    o = _in_splits()
    w = w_in[l]
    cols = lambda a, n: w[:, a:a + n]
    p = {}
    p["w_p1"] = jnp.concatenate([cols(o["q"], 3 * NA_W), cols(o["a"], 2 * CONV_CH), cols(o["g"], 3 * dmodel)],
                                axis=1).astype(BF16)
    kr = cols(o["kr"], MLA_ROPE)
    krs = kr[:, _ROPE_PARTNER]
    p["w_p2"] = jnp.concatenate([cols(o["cq"], MLA_Q_LORA), kr, kr, krs, krs, cols(o["ckv"], MLA_KV_LORA)],
                                axis=1).astype(BF16)
    qscale = NA_HEAD_DIM ** -0.5
    p["na_g"] = jnp.concatenate([jnp.tile(na_q_g[l].astype(F32) * qscale, NA_HEADS),
                                 jnp.tile(na_k_g[l].astype(F32), NA_HEADS)]).reshape(1, 2 * NA_W)
    p["na_bias"] = _na_bias_table(na_rpb[l])
    wq = mla_w_uq[l].reshape(MLA_Q_LORA, MLA_HEADS, MLA_QK)
    wq_rope = wq[:, :, MLA_NOPE:]
    flat = lambda a: a.reshape(a.shape[0], -1)
    p["w_uq"] = jnp.concatenate([flat(wq[:, :, :MLA_NOPE]), flat(wq_rope), flat(wq_rope[:, :, _ROPE_PARTNER])],
                                axis=1).astype(BF16)
    wkv = mla_w_ukv[l].reshape(MLA_KV_LORA, MLA_HEADS, MLA_NOPE + MLA_V)
    p["w_ukv"] = jnp.concatenate([flat(wkv[:, :, :MLA_NOPE]), flat(wkv[:, :, MLA_NOPE:])], axis=1).astype(BF16)
    p["q_lora_g"] = mla_q_lora_g[l].astype(F32).reshape(1, -1)
    p["kv_lora_g"] = mla_kv_lora_g[l].astype(F32).reshape(1, -1)
    mscale = MLA_QK ** -0.5
    p["q_gn"] = (mla_q_g[l][:MLA_NOPE].astype(F32) * mscale).reshape(1, -1)
    p["k_gn"] = mla_k_g[l][:MLA_NOPE].astype(F32).reshape(1, -1)
    cgq, sgq = _rope_tables(seq_len, mla_q_g[l][MLA_NOPE:] * mscale)
    cgk, sgk = _rope_tables(seq_len, mla_k_g[l][MLA_NOPE:])
    p["q_tabs"] = (cgq, sgq)
    p["k_tabs"] = (cgk, sgk)
    p["q_tabs_c"] = (jnp.tile((mla_q_g[l][MLA_NOPE:].astype(F32) * mscale)[None], (1, 2)), jnp.zeros((1, LANES), F32))
    p["k_tabs_c"] = (jnp.tile(mla_k_g[l][MLA_NOPE:].astype(F32)[None], (1, 2)), jnp.zeros((1, LANES), F32))
    p["w_na_o"] = w_na_o[l].astype(BF16)
    p["w_mla_o"] = w_mla_o[l].astype(BF16)
    p["w_conv_o"] = w_conv_o[l].astype(BF16)
    p["w_out"] = w_out[l].astype(BF16)
    p["conv_w"] = jnp.concatenate([conv_w_dw[l].astype(F32), jnp.zeros((1, CONV_CH), F32)], axis=0)
    p["conv_b"] = conv_b_dw[l].astype(F32).reshape(1, -1)
    p["ln_g"] = conv_ln_g[l].astype(F32).reshape(1, -1)
    p["ln_b"] = conv_ln_b[l].astype(F32).reshape(1, -1)
    p["router_w"] = jnp.concatenate([router_w[l], jnp.zeros((dmodel, LANES - N_EXPERTS), router_w.dtype)],
                                    axis=1).astype(BF16)
    p["router_b"] = jnp.concatenate([router_b[l].astype(F32), jnp.zeros((LANES - N_EXPERTS,), F32)]).reshape(1, LANES)
    p["exp"] = (exp_w_gate[l].astype(BF16), exp_w_up[l].astype(BF16), exp_w_down[l].astype(BF16))
    p["sh"] = (sh_w_gate[l].astype(BF16), sh_w_up[l].astype(BF16), sh_w_down[l].astype(BF16))
    return p


def _group_matrices():
    i = np.arange(LANES)
    bd = (i[:, None] // NA_HEAD_DIM == i[None, :] // NA_HEAD_DIM)
    ones = np.ones((LANES, LANES), bool)
    e0 = np.broadcast_to((i < MLA_ROPE)[:, None], (LANES, LANES))
    e1 = np.broadcast_to((i >= MLA_ROPE)[:, None], (LANES, LANES))
    f = lambda m: jnp.asarray(m, BF16)
    return f(bd), (f(ones), f(e0), f(e1))


def _expand_tabs(tabs, tm):
    return tuple(jnp.broadcast_to(t, (tm, LANES)) if t.shape[0] == 1 else t for t in tabs)


def _mixer_inputs(h, p, mats, bd, seq, ctx):
    tm_mm = min(1024, h.shape[0])
    p1 =_matmul(h, p["w_p1"], BF16, tm_mm, 1024, "in_proj_1")
    p2 = _matmul(h, p["w_p2"], BF16, tm_mm, P2_W // 2, "in_proj_2")
    qk = _headnorm(p1, p["na_g"], bd, min(512, h.shape[0]))
    tr = 256
    qt = _expand_tabs(p["q_tabs_c"], tr) if ctx else p["q_tabs"]
    kt = _expand_tabs(p["k_tabs_c"], tr) if ctx else p["k_tabs"]
    tseq = tr if ctx else seq
    km, vm = _mla_kv(p2, p["kv_lora_g"], p["w_ukv"], p["k_gn"], kt[0], kt[1], mats, tseq, tr)
    return p1, p2, qk, km, vm, (qt, tseq, tr)


def kernel(x, c, ctx, c_ctx, ada_w, ada_b, norm1_g, norm2_g, w_in, na_q_g, na_k_g, na_rpb, w_na_o, mla_q_lora_g, mla_w_uq, mla_kv_lora_g, mla_w_ukv, mla_q_g, mla_k_g, w_mla_o, conv_w_dw, conv_b_dw, conv_ln_g, conv_ln_b, w_conv_o, w_out, router_w, router_b, exp_w_gate, exp_w_up, exp_w_down, sh_w_gate, sh_w_up, sh_w_down):
    batch, seq, d = x.shape
    lc = ctx.shape[1]
    depth = ada_w.shape[0]
    mod_rows = (batch + 1 + 7) // 8 * 8
    cc = jnp.concatenate([c, c_ctx[None], jnp.zeros((mod_rows - batch - 1, d), c.dtype)], axis=0)
    mod_all = _ada(cc, ada_w, ada_b)
    bd, mats = _group_matrices()
    xs = x.reshape(batch * seq, d)
    zs = ctx.reshape(batch * lc, d)
    tm_x = 512
    tm_c = 256
    x_row = lambda i: (i * tm_x) // seq
    c_row = lambda i: batch

    for l in range(depth):
        need_ctx = l < depth - 1
        p = _prep_layer(l, d, w_in, na_q_g, na_k_g, na_rpb, w_na_o, mla_q_lora_g, mla_w_uq, mla_kv_lora_g, mla_w_ukv,
                        mla_q_g, mla_k_g, w_mla_o, conv_w_dw, conv_b_dw, conv_ln_g, conv_ln_b, w_conv_o, w_out,
                        router_w, router_b, exp_w_gate, exp_w_up, exp_w_down, sh_w_gate, sh_w_up, sh_w_down, seq)
        mod3 = mod_all[l].reshape(mod_rows * 6, 1, d)
        hx = _norm_mod(xs, norm1_g[l], mod3, x_row, 1, 0, tm_x)
        hc = _norm_mod(zs, norm1_g[l], mod3, c_row, 1, 0, tm_c)
        p1x, p2x, qkx, kmx, vmx, (qtx, tsx, trx) = _mixer_inputs(hx, p, mats, bd, seq, False)
        p1c, p2c, qkc, kmc, vmc, (qtc, tsc, trc) = _mixer_inputs(hc, p, mats, bd, lc, True)
        o_na = _na_attention(qkx, p1x, qkc, p1c, p["na_bias"], batch)
        qmx = _mla_q(p2x, p["q_lora_g"], p["w_uq"], p["q_gn"], qtx[0], qtx[1], mats, tsx, trx)
        o_mla = _mla_attention(qmx, [(kmc, vmc), (kmx, vmx)], batch, 256)
        u = _conv_module(p1x, p["conv_w"], p["conv_b"], p["ln_g"], p["ln_b"], seq, 256)
        mrg = _merge(o_na, o_mla, u, p1x, p["w_na_o"], p["w_mla_o"], p["w_conv_o"], 1024, 512)
        xs, hx2 = _out_proj(mrg, p["w_out"], xs, mod3, x_row, norm2_g[l], tm_x)
        if need_ctx:
            o_na_c = _na_ctx_attention(qkc, p1c, batch)
            qmc = _mla_q(p2c, p["q_lora_g"], p["w_uq"], p["q_gn"], qtc[0], qtc[1], mats, tsc, trc)
            o_mla_c = _mla_attention(qmc, [(kmc, vmc)], batch, lc)
            u_c = _conv_module(p1c, p["conv_w"], p["conv_b"], p["ln_g"], p["ln_b"], lc, lc)
            mrg_c = _merge(o_na_c, o_mla_c, u_c, p1c, p["w_na_o"], p["w_mla_o"], p["w_conv_o"], 1024, 512)
            zs, hc2 = _out_proj(mrg_c, p["w_out"], zs, mod3, c_row, norm2_g[l], tm_c)
            tokens = jnp.concatenate([hc2, hx2], axis=0)
        else:
            tokens = hx2
        routed, shared = _moe(tokens, p["router_w"], p["router_b"], *p["exp"], *p["sh"])
        if need_ctx:
            nc = batch * lc
            zs = _moe_residual(zs, mod3, c_row, routed[:nc], shared[:nc], tm_c)
            xs = _moe_residual(xs, mod3, x_row, routed[nc:], shared[nc:], tm_x)
        else:
            xs = _moe_residual(xs, mod3, x_row, routed, shared, tm_x)
    return xs.reshape(batch, seq, d)
```

```python
import functools

import numpy as np
import jax
import jax.numpy as jnp
from jax import lax
from jax.experimental import pallas as pl
from jax.experimental.pallas import tpu as pltpu

F32 = jnp.float32
BF16 = jnp.bfloat16

GRID_W = 64
EPS = 1e-6
NEG_INF = -1e30
NA_HEADS = 16
NA_HEAD_DIM = 64
NA_W = NA_HEADS * NA_HEAD_DIM
NA_WIN_ROWS = 8
NA_WIN_COLS = 16
MLA_HEADS = 16
MLA_NOPE = 128
MLA_ROPE = 64
MLA_QK = MLA_NOPE + MLA_ROPE
MLA_V = 128
MLA_Q_LORA = 768
MLA_KV_LORA = 512
ROPE_THETA = 10000.0
CONV_CH = 1024
CONV_WIDTH = 31
CONV_HALO = 16
N_EXPERTS = 64
TOP_K = 8
D_EXPERT = 512
ROUTED_SCALE = 2.5
MOE_BLOCK = 256
LANES = 128
MLA_HEAD_PAD = 2 * LANES
VMEM_LIMIT = 56 * 1024 * 1024

P1_Q, P1_K, P1_V, P1_A, P1_B, P1_GNA, P1_GMLA, P1_GCV = 0, 1024, 2048, 3072, 4096, 5120, 7168, 9216
P1_W = 11264
P2_CQ, P2_KR, P2_KRS, P2_CKV = 0, 768, 896, 1024
P2_W = 1536


def _cparams(n_axes):
    return pltpu.CompilerParams(dimension_semantics=("arbitrary",) * n_axes, vmem_limit_bytes=VMEM_LIMIT)


def _dot(a, b):
    return jnp.dot(a, b, preferred_element_type=F32)


def _dot_nt(a, b):
    return lax.dot_general(a, b, (((1,), (1,)), ((), ())), preferred_element_type=F32)


def _dot_hilo(x, m):
    hi = x.astype(BF16)
    lo = (x - hi.astype(F32)).astype(BF16)
    return _dot(hi, m) + _dot(lo, m)


def _ada_kernel(c_ref, w_ref, b_ref, o_ref):
    c = c_ref[...]
    a = (c * jax.nn.sigmoid(c)).astype(BF16)
    o_ref[...] = _dot(a, w_ref[...].astype(BF16)) + b_ref[...]


def _ada(cc, ada_w, ada_b):
    nl, d, n = ada_w.shape
    r = cc.shape[0]
    tn = 1024
    return pl.pallas_call(
        _ada_kernel,
        out_shape=jax.ShapeDtypeStruct((nl, r, n), F32),
        grid=(nl, n // tn),
        in_specs=[pl.BlockSpec((r, d), lambda l, j: (0, 0)),
                  pl.BlockSpec((None, d, tn), lambda l, j: (l, 0, j)),
                  pl.BlockSpec((None, 1, tn), lambda l, j: (l, 0, j))],
        out_specs=pl.BlockSpec((None, r, tn), lambda l, j: (l, 0, j)),
        compiler_params=_cparams(2), name="ada_mod",
    )(cc, ada_w, ada_b.reshape(nl, 1, n))


def _norm_mod_kernel(x_ref, g_ref, sc_ref, sh_ref, o_ref):
    x = x_ref[...]
    y = x * lax.rsqrt(jnp.mean(x * x, axis=-1, keepdims=True) + EPS) * g_ref[...]
    o_ref[...] = (y * (1.0 + sc_ref[...]) + sh_ref[...]).astype(o_ref.dtype)


def _norm_mod(x, g, mod3, mod_row, k_sc, k_sh, tm):
    m, d = x.shape
    return pl.pallas_call(
        _norm_mod_kernel,
        out_shape=jax.ShapeDtypeStruct((m, d), BF16),
        grid=(m // tm,),
        in_specs=[pl.BlockSpec((tm, d), lambda i: (i, 0)),
                  pl.BlockSpec((1, d), lambda i: (0, 0)),
                  pl.BlockSpec((None, 1, d), lambda i: (mod_row(i) * 6 + k_sc, 0, 0)),
                  pl.BlockSpec((None, 1, d), lambda i: (mod_row(i) * 6 + k_sh, 0, 0))],
        out_specs=pl.BlockSpec((tm, d), lambda i: (i, 0)),
        compiler_params=_cparams(1), name="norm_mod",
    )(x, g.reshape(1, d), mod3, mod3)


def _mm_kernel(a_ref, w_ref, o_ref):
    o_ref[...] = _dot(a_ref[...], w_ref[...]).astype(o_ref.dtype)


def _matmul(a, w, out_dtype, tm, tn, name):
    m, k = a.shape
    n = w.shape[1]
    return pl.pallas_call(
        _mm_kernel,
        out_shape=jax.ShapeDtypeStruct((m, n), out_dtype),
        grid=(m // tm, n // tn),
        in_specs=[pl.BlockSpec((tm, k), lambda i, j: (i, 0)),
                  pl.BlockSpec((k, tn), lambda i, j: (0, j))],
        out_specs=pl.BlockSpec((tm, tn), lambda i, j: (i, j)),
        compiler_params=_cparams(2), name=name,
    )(a, w)


def _headnorm_kernel(x_ref, g_ref, bd_ref, o_ref):
    bd = bd_ref[...]
    for c in range(x_ref.shape[1] // LANES):
        sl = slice(c * LANES, (c + 1) * LANES)
        x = x_ref[:, sl].astype(F32)
        ss = _dot_hilo(x * x, bd)
        y = x * lax.rsqrt(ss * (1.0 / NA_HEAD_DIM) + EPS) * g_ref[:, sl]
        o_ref[:, sl] = y.astype(o_ref.dtype)


def _headnorm(p1, g_row, bd, tm):
    m = p1.shape[0]
    w = g_row.shape[1]
    return pl.pallas_call(
        _headnorm_kernel,
        out_shape=jax.ShapeDtypeStruct((m, w), BF16),
        grid=(m // tm,),
        in_specs=[pl.BlockSpec((tm, w), lambda i: (i, 0)),
                  pl.BlockSpec((1, w), lambda i: (0, 0)),
                  pl.BlockSpec((LANES, LANES), lambda i: (0, 0))],
        out_specs=pl.BlockSpec((tm, w), lambda i: (i, 0)),
        compiler_params=_cparams(1), name="na_headnorm",
    )(p1, g_row, bd)


LOG2E = 1.4426950408889634


def _softmax_pv(scores, values):
    m = scores[0].max(axis=-1, keepdims=True)
    for s in scores[1:]:
        m = jnp.maximum(m, s.max(axis=-1, keepdims=True))
    l = None
    o = None
    for s, v in zip(scores, values):
        p = jnp.exp2(s - m)
        ps = p.sum(axis=-1, keepdims=True)
        po = _dot(p.astype(BF16), v)
        l = ps if l is None else l + ps
        o = po if o is None else o + po
    return o / l


def _dot_tn(a, b):
    return lax.dot_general(a, b, (((0,), (0,)), ((), ())), preferred_element_type=F32)


def _na_kernel(q_ref, kx_ref, vx_ref, kc_ref, vc_ref, bias_ref, o_ref):
    rows = kx_ref.shape[0] // GRID_W
    r = pl.program_id(1)
    r0 = jnp.clip(r - NA_WIN_ROWS // 2, 0, rows - NA_WIN_ROWS)
    kstart = pl.multiple_of(r0 * GRID_W, GRID_W)
    nwin = NA_WIN_ROWS * GRID_W
    lo = lax.broadcasted_iota(jnp.int32, (GRID_W, LANES), 1) < NA_HEAD_DIM
    zero = jnp.zeros((GRID_W, LANES), q_ref.dtype)
    for p in range(NA_W // LANES):
        sl = slice(p * LANES, (p + 1) * LANES)
        q2 = q_ref[:, sl]
        qs = jnp.concatenate([jnp.where(lo, q2, zero), jnp.where(lo, zero, q2)], axis=0)
        kw = kx_ref[pl.ds(kstart, nwin), sl]
        vw = vx_ref[pl.ds(kstart, nwin), sl]
        s_w = _dot_nt(kw, qs) + bias_ref[p]
        s_c = _dot_nt(kc_ref[:, sl], qs)
        m = jnp.maximum(s_w.max(axis=0, keepdims=True), s_c.max(axis=0, keepdims=True))
        p_w = jnp.exp2(s_w - m)
        p_c = jnp.exp2(s_c - m)
        inv = 1.0 / (p_w.sum(axis=0, keepdims=True) + p_c.sum(axis=0, keepdims=True))
        o2 = _dot_tn((p_w * inv).astype(BF16), vw) + _dot_tn((p_c * inv).astype(BF16), vc_ref[:, sl])
        o_ref[:, sl] = jnp.where(lo, o2[:GRID_W], o2[GRID_W:]).astype(o_ref.dtype)


def _na_attention(qk_x, p1_x, qk_c, p1_c, bias_tab, batch):
    l = qk_x.shape[0] // batch
    lc = qk_c.shape[0] // batch
    rows = l // GRID_W
    nwin = NA_WIN_ROWS * GRID_W

    def bias_idx(b, r):
        r0 = jnp.clip(r - NA_WIN_ROWS // 2, 0, rows - NA_WIN_ROWS)
        return (r - r0, 0, 0, 0)

    return pl.pallas_call(
        _na_kernel,
        out_shape=jax.ShapeDtypeStruct((batch * l, NA_W), BF16),
        grid=(batch, rows),
        in_specs=[pl.BlockSpec((GRID_W, NA_W), lambda b, r: (b * rows + r, 0)),
                  pl.BlockSpec((l, NA_W), lambda b, r: (b, 1)),
                  pl.BlockSpec((l, NA_W), lambda b, r: (b, P1_V // NA_W)),
                  pl.BlockSpec((lc, NA_W), lambda b, r: (b, 1)),
                  pl.BlockSpec((lc, NA_W), lambda b, r: (b, P1_V // NA_W)),
                  pl.BlockSpec((None, NA_HEADS // 2, nwin, LANES), bias_idx)],
        out_specs=pl.BlockSpec((GRID_W, NA_W), lambda b, r: (b * rows + r, 0)),
        compiler_params=_cparams(2), name="na_attention",
    )(qk_x, qk_x, p1_x, qk_c, p1_c, bias_tab)


def _na_ctx_kernel(q_ref, k_ref, v_ref, o_ref):
    n = q_ref.shape[0]
    lo = lax.broadcasted_iota(jnp.int32, (n, LANES), 1) < NA_HEAD_DIM
    zero = jnp.zeros((n, LANES), q_ref.dtype)
    for p in range(NA_W // LANES):
        sl = slice(p * LANES, (p + 1) * LANES)
        q2 = q_ref[:, sl]
        k = k_ref[:, sl]
        v = v_ref[:, sl]
        outs = []
        for hh in range(2):
            qm = jnp.where(lo, q2, zero) if hh == 0 else jnp.where(lo, zero, q2)
            outs.append(_softmax_pv([_dot_nt(qm, k)], [v]))
        o_ref[:, sl] = jnp.where(lo, outs[0], outs[1]).astype(o_ref.dtype)


def _na_ctx_attention(qk_c, p1_c, batch):
    lc = qk_c.shape[0] // batch
    return pl.pallas_call(
        _na_ctx_kernel,
        out_shape=jax.ShapeDtypeStruct((batch * lc, NA_W), BF16),
        grid=(batch,),
        in_specs=[pl.BlockSpec((lc, NA_W), lambda b: (b, 0)),
                  pl.BlockSpec((lc, NA_W), lambda b: (b, 1)),
                  pl.BlockSpec((lc, NA_W), lambda b: (b, P1_V // NA_W))],
        out_specs=pl.BlockSpec((lc, NA_W), lambda b: (b, 0)),
        compiler_params=_cparams(1), name="na_ctx_attention",
    )(qk_c, qk_c, p1_c)


def _na_bias_table(rpb):
    cls = np.arange(NA_WIN_ROWS)[:, None, None, None]
    qc = np.arange(GRID_W)[None, :, None, None]
    w = np.arange(NA_WIN_ROWS)[None, None, :, None]
    kc = np.arange(GRID_W)[None, None, None, :]
    c0 = np.clip(qc - NA_WIN_COLS // 2, 0, GRID_W - NA_WIN_COLS)
    ok = (kc >= c0) & (kc < c0 + NA_WIN_COLS)
    dr = np.broadcast_to(w - cls + NA_WIN_ROWS - 1, (NA_WIN_ROWS, GRID_W, NA_WIN_ROWS, GRID_W))
    dc = np.broadcast_to(np.clip(kc - qc + NA_WIN_COLS - 1, 0, 2 * NA_WIN_COLS - 2), dr.shape)
    ok = np.broadcast_to(ok, dr.shape)
    tab = rpb.astype(F32)[:, dr, dc] * LOG2E
    tab = jnp.where(ok[None], tab, NEG_INF)
    tab = tab.reshape(NA_HEADS // 2, 2, NA_WIN_ROWS, GRID_W, NA_WIN_ROWS * GRID_W)
    tab = jnp.transpose(tab, (2, 0, 4, 1, 3))
    return tab.reshape(NA_WIN_ROWS, NA_HEADS // 2, NA_WIN_ROWS * GRID_W, LANES)


def _mla_q_kernel(p2_ref, gl_ref, w_ref, gn_ref, cg_ref, sg_ref, ones_ref, e0_ref, e1_ref, o_ref):
    cq = p2_ref[...].astype(F32)
    cqn = cq * lax.rsqrt(jnp.mean(cq * cq, axis=-1, keepdims=True) + EPS) * gl_ref[...]
    y = _dot(cqn.astype(BF16), w_ref[...])
    tm = y.shape[0]
    lo = lax.broadcasted_iota(jnp.int32, (tm, LANES), 1) < MLA_ROPE
    nope_w = MLA_HEADS * MLA_NOPE
    rope_w = MLA_HEADS * MLA_ROPE
    ones, e0, e1 = ones_ref[...], e0_ref[...], e1_ref[...]
    gn = gn_ref[...]
    for p in range(MLA_HEADS // 2):
        yr = y[:, nope_w + p * LANES: nope_w + (p + 1) * LANES]
        yrs = y[:, nope_w + rope_w + p * LANES: nope_w + rope_w + (p + 1) * LANES]
        rot = yr * cg_ref[...] + yrs * sg_ref[...]
        yr2 = yr * yr
        ssr = (_dot_hilo(yr2, e0), _dot_hilo(yr2, e1))
        for hh in range(2):
            h = 2 * p + hh
            nope = y[:, h * MLA_NOPE:(h + 1) * MLA_NOPE]
            tot = _dot_hilo(nope * nope, ones) + ssr[hh]
            scale = lax.rsqrt(tot * (1.0 / MLA_QK) + EPS)
            o_ref[:, h * MLA_HEAD_PAD: h * MLA_HEAD_PAD + LANES] = (nope * scale * gn).astype(o_ref.dtype)
            rs = rot * scale
            rs = jnp.where(lo, rs, 0.0) if hh == 0 else jnp.where(lo, 0.0, rs)
            o_ref[:, h * MLA_HEAD_PAD + LANES: (h + 1) * MLA_HEAD_PAD] = rs.astype(o_ref.dtype)


def _mla_q(p2, g_lora, w_uq, g_nope, cg, sg, mats, seq, tm):
    m = p2.shape[0]
    nt = seq // tm
    ones, e0, e1 = mats
    full = lambda shape: pl.BlockSpec(shape, lambda i: (0,) * len(shape))
    return pl.pallas_call(
        _mla_q_kernel,
        out_shape=jax.ShapeDtypeStruct((m, MLA_HEADS * MLA_HEAD_PAD), BF16),
        grid=(m // tm,),
        in_specs=[pl.BlockSpec((tm, MLA_Q_LORA), lambda i: (i, 0)),
                  full((1, MLA_Q_LORA)),
                  full(w_uq.shape),
                  full((1, LANES)),
                  pl.BlockSpec((tm, LANES), lambda i: (i % nt, 0)),
                  pl.BlockSpec((tm, LANES), lambda i: (i % nt, 0)),
                  full((LANES, LANES)), full((LANES, LANES)), full((LANES, LANES))],
        out_specs=pl.BlockSpec((tm, MLA_HEADS * MLA_HEAD_PAD), lambda i: (i, 0)),
        compiler_params=_cparams(1), name="mla_q_proj",
    )(p2, g_lora, w_uq, g_nope, cg, sg, ones, e0, e1)


def _mla_kv_kernel(ckv_ref, kr_ref, krs_ref, gl_ref, w_ref, gn_ref, cg_ref, sg_ref, ones_ref, e0_ref, k_ref, v_ref):
    ckv = ckv_ref[...].astype(F32)
    cn = ckv * lax.rsqrt(jnp.mean(ckv * ckv, axis=-1, keepdims=True) + EPS) * gl_ref[...]
    y = _dot(cn.astype(BF16), w_ref[...])
    tm = y.shape[0]
    lo = lax.broadcasted_iota(jnp.int32, (tm, LANES), 1) < MLA_ROPE
    nope_w = MLA_HEADS * MLA_NOPE
    v_ref[...] = y[:, nope_w:].astype(v_ref.dtype)
    kr = kr_ref[...].astype(F32)
    rot = kr * cg_ref[...] + krs_ref[...].astype(F32) * sg_ref[...]
    ssr = _dot_hilo(kr * kr, e0_ref[...])
    ones = ones_ref[...]
    gn = gn_ref[...]
    for h in range(MLA_HEADS):
        nope = y[:, h * MLA_NOPE:(h + 1) * MLA_NOPE]
        tot = _dot_hilo(nope * nope, ones) + ssr
        scale = lax.rsqrt(tot * (1.0 / MLA_QK) + EPS)
        k_ref[:, h * MLA_HEAD_PAD: h * MLA_HEAD_PAD + LANES] = (nope * scale * gn).astype(k_ref.dtype)
        rs = rot * scale
        rs = jnp.where(lo, rs, 0.0) if h % 2 == 0 else jnp.where(lo, 0.0, rs)
        k_ref[:, h * MLA_HEAD_PAD + LANES: (h + 1) * MLA_HEAD_PAD] = rs.astype(k_ref.dtype)


def _mla_kv(p2, g_lora, w_ukv, g_nope, cg, sg, mats, seq, tm):
    m = p2.shape[0]
    nt = seq // tm
    ones, e0, _ = mats
    full = lambda shape: pl.BlockSpec(shape, lambda i: (0,) * len(shape))
    return pl.pallas_call(
        _mla_kv_kernel,
        out_shape=(jax.ShapeDtypeStruct((m, MLA_HEADS * MLA_HEAD_PAD), BF16),
                   jax.ShapeDtypeStruct((m, MLA_HEADS * MLA_V), BF16)),
        grid=(m // tm,),
        in_specs=[pl.BlockSpec((tm, MLA_KV_LORA), lambda i: (i, P2_CKV // MLA_KV_LORA)),
                  pl.BlockSpec((tm, LANES), lambda i: (i, P2_KR // LANES)),
                  pl.BlockSpec((tm, LANES), lambda i: (i, P2_KRS // LANES)),
                  full((1, MLA_KV_LORA)),
                  full(w_ukv.shape),
                  full((1, LANES)),
                  pl.BlockSpec((tm, LANES), lambda i: (i % nt, 0)),
                  pl.BlockSpec((tm, LANES), lambda i: (i % nt, 0)),
                  full((LANES, LANES)), full((LANES, LANES))],
        out_specs=(pl.BlockSpec((tm, MLA_HEADS * MLA_HEAD_PAD), lambda i: (i, 0)),
                   pl.BlockSpec((tm, MLA_HEADS * MLA_V), lambda i: (i, 0))),
        compiler_params=_cparams(1), name="mla_kv_proj",
    )(p2, p2, p2, g_lora, w_ukv, g_nope, cg, sg, ones, e0)


def _mla_attn_kernel(*refs, n_kv):
    q_ref = refs[0]
    kv = refs[1:1 + 2 * n_kv]
    o_ref = refs[1 + 2 * n_kv]
    for hh in range(MLA_HEADS_PER_STEP):
        qsl = slice(hh * MLA_HEAD_PAD, (hh + 1) * MLA_HEAD_PAD)
        vsl = slice(hh * MLA_V, (hh + 1) * MLA_V)
        q = q_ref[:, qsl]
        scores = [_dot_nt(q, kv[2 * i][:, qsl]) for i in range(n_kv)]
        values = [kv[2 * i + 1][:, vsl] for i in range(n_kv)]
        o_ref[:, vsl] = _softmax_pv(scores, values).astype(o_ref.dtype)


MLA_HEADS_PER_STEP = 2


def _mla_attention(q, kvs, batch, tq):
    lq = q.shape[0] // batch
    nq = lq // tq
    hs = MLA_HEADS_PER_STEP
    in_specs = [pl.BlockSpec((tq, hs * MLA_HEAD_PAD), lambda b, h, j: (b * nq + j, h))]
    args = [q]
    for k, v in kvs:
        s = k.shape[0] // batch
        in_specs.append(pl.BlockSpec((s, hs * MLA_HEAD_PAD), lambda b, h, j: (b, h)))
        in_specs.append(pl.BlockSpec((s, hs * MLA_V), lambda b, h, j: (b, h)))
        args += [k, v]
    return pl.pallas_call(
        functools.partial(_mla_attn_kernel, n_kv=len(kvs)),
        out_shape=jax.ShapeDtypeStruct((batch * lq, MLA_HEADS * MLA_V), BF16),
        grid=(batch, MLA_HEADS // hs, nq),
        in_specs=in_specs,
        out_specs=pl.BlockSpec((tq, hs * MLA_V), lambda b, h, j: (b * nq + j, h)),
        compiler_params=_cparams(3), name="mla_attention",
    )(*args)


def _conv_kernel(ap_ref, a_ref, an_ref, bp_ref, b_ref, bn_ref, w_ref, cb_ref, g_ref, beta_ref, o_ref, u_scr, y_scr,
                 *, n_tiles):
    j = pl.program_id(1)
    tt = a_ref.shape[0]
    glu = lambda a, b: a[...].astype(F32) * jax.nn.sigmoid(b[...].astype(F32))
    u_scr[0:CONV_HALO, :] = jnp.where(j > 0, glu(ap_ref, bp_ref), 0.0)
    u_scr[CONV_HALO:CONV_HALO + tt, :] = glu(a_ref, b_ref)
    u_scr[CONV_HALO + tt:, :] = jnp.where(j < n_tiles - 1, glu(an_ref, bn_ref), 0.0)
    off = CONV_HALO - CONV_WIDTH // 2
    for c in range(CONV_CH // LANES):
        sl = slice(c * LANES, (c + 1) * LANES)
        acc = jnp.zeros((tt, LANES), F32)
        for k in range(CONV_WIDTH):
            acc = acc + u_scr[off + k: off + k + tt, sl] * w_ref[k:k + 1, sl]
        y_scr[:, sl] = acc + cb_ref[:, sl]
    y = y_scr[...]
    mu = jnp.mean(y, axis=-1, keepdims=True)
    d = y - mu
    var = jnp.mean(d * d, axis=-1, keepdims=True)
    z = d * lax.rsqrt(var + EPS) * g_ref[...] + beta_ref[...]
    o_ref[...] = (z * jax.nn.sigmoid(z)).astype(o_ref.dtype)


def _conv_module(p1, w_dw, b_dw, ln_g, ln_b, seq, tt):
    m = p1.shape[0]
    batch = m // seq
    n_tiles = seq // tt
    hb = tt // CONV_HALO
    n_hblk = m // CONV_HALO
    ca, cb = P1_A // CONV_CH, P1_B // CONV_CH

    def prev_idx(c):
        return lambda b, j: (jnp.maximum((b * n_tiles + j) * hb - 1, 0), c)

    def next_idx(c):
        return lambda b, j: (jnp.minimum((b * n_tiles + j + 1) * hb, n_hblk - 1), c)

    cur = lambda c: (lambda b, j: (b * n_tiles + j, c))
    full = lambda shape: pl.BlockSpec(shape, lambda b, j: (0,) * len(shape))
    return pl.pallas_call(
        functools.partial(_conv_kernel, n_tiles=n_tiles),
        out_shape=jax.ShapeDtypeStruct((m, CONV_CH), BF16),
        grid=(batch, n_tiles),
        in_specs=[pl.BlockSpec((CONV_HALO, CONV_CH), prev_idx(ca)),
                  pl.BlockSpec((tt, CONV_CH), cur(ca)),
                  pl.BlockSpec((CONV_HALO, CONV_CH), next_idx(ca)),
                  pl.BlockSpec((CONV_HALO, CONV_CH), prev_idx(cb)),
                  pl.BlockSpec((tt, CONV_CH), cur(cb)),
                  pl.BlockSpec((CONV_HALO, CONV_CH), next_idx(cb)),
                  full((32, CONV_CH)), full((1, CONV_CH)), full((1, CONV_CH)), full((1, CONV_CH))],
        out_specs=pl.BlockSpec((tt, CONV_CH), lambda b, j: (b * n_tiles + j, 0)),
        scratch_shapes=[pltpu.VMEM((tt + 2 * CONV_HALO, CONV_CH), F32), pltpu.VMEM((tt, CONV_CH), F32)],
        compiler_params=_cparams(2), name="conv_module",
    )(p1, p1, p1, p1, p1, p1, w_dw, b_dw, ln_g, ln_b)


def _merge_kernel(ona_ref, omla_ref, u_ref, gna_ref, gmla_ref, gcv_ref, wna_ref, wmla_ref, wcv_ref, o_ref):
    sig = lambda r: jax.nn.sigmoid(r[...].astype(F32))
    m = sig(gna_ref) * _dot(ona_ref[...], wna_ref[...])
    m = m + sig(gmla_ref) * _dot(omla_ref[...], wmla_ref[...])
    m = m + sig(gcv_ref) * _dot(u_ref[...], wcv_ref[...])
    o_ref[...] = m.astype(o_ref.dtype)


def _merge(o_na, o_mla, u, p1, w_na_o, w_mla_o, w_conv_o, tm, tn):
    m = o_na.shape[0]
    d = w_na_o.shape[1]
    tm = min(tm, m)
    row = lambda k: pl.BlockSpec((tm, k), lambda i, j: (i, 0))
    gate = lambda off: pl.BlockSpec((tm, tn), lambda i, j: (i, off // tn + j))
    wcol = lambda k: pl.BlockSpec((k, tn), lambda i, j: (0, j))
    return pl.pallas_call(
        _merge_kernel,
        out_shape=jax.ShapeDtypeStruct((m, d), BF16),
        grid=(m // tm, d // tn),
        in_specs=[row(o_na.shape[1]), row(o_mla.shape[1]), row(u.shape[1]),
                  gate(P1_GNA), gate(P1_GMLA), gate(P1_GCV),
                  wcol(w_na_o.shape[0]), wcol(w_mla_o.shape[0]), wcol(w_conv_o.shape[0])],
        out_specs=pl.BlockSpec((tm, tn), lambda i, j: (i, j)),
        compiler_params=_cparams(2), name="gated_merge",
    )(o_na, o_mla, u, p1, p1, p1, w_na_o, w_mla_o, w_conv_o)


def _out_proj_kernel(m_ref, w_ref, x_ref, gate_ref, g2_ref, sc_ref, sh_ref, xo_ref, ho_ref):
    x = x_ref[...] + gate_ref[...] * _dot(m_ref[...], w_ref[...])
    xo_ref[...] = x
    y = x * lax.rsqrt(jnp.mean(x * x, axis=-1, keepdims=True) + EPS) * g2_ref[...]
    ho_ref[...] = (y * (1.0 + sc_ref[...]) + sh_ref[...]).astype(ho_ref.dtype)


def _out_proj(mrg, w_out, x, mod3, mod_row, g2, tm):
    m, d = x.shape
    modspec = lambda k: pl.BlockSpec((None, 1, d), lambda i: (mod_row(i) * 6 + k, 0, 0))
    return pl.pallas_call(
        _out_proj_kernel,
        out_shape=(jax.ShapeDtypeStruct((m, d), F32), jax.ShapeDtypeStruct((m, d), BF16)),
        grid=(m // tm,),
        in_specs=[pl.BlockSpec((tm, d), lambda i: (i, 0)),
                  pl.BlockSpec((d, d), lambda i: (0, 0)),
                  pl.BlockSpec((tm, d), lambda i: (i, 0)),
                  modspec(2),
                  pl.BlockSpec((1, d), lambda i: (0, 0)),
                  modspec(4), modspec(3)],
        out_specs=(pl.BlockSpec((tm, d), lambda i: (i, 0)), pl.BlockSpec((tm, d), lambda i: (i, 0))),
        compiler_params=_cparams(1), name="out_proj_residual",
    )(mrg, w_out, x, mod3, g2.reshape(1, d), mod3, mod3)


def _router_kernel(h_ref, w_ref, b_ref, tri_ref, idx_ref, gate_ref, rank_ref, cnt_ref):
    scores = jax.nn.sigmoid(_dot(h_ref[...], w_ref[...]))
    tm = scores.shape[0]
    lane = lax.broadcasted_iota(jnp.int32, (tm, LANES), 1).astype(F32)
    sel = jnp.where(lane < N_EXPERTS, scores + b_ref[...], -jnp.inf)
    idx_out = jnp.zeros((tm, LANES), F32)
    val_out = jnp.zeros((tm, LANES), F32)
    total = jnp.zeros((tm, 1), F32)
    chosen = jnp.zeros((tm, LANES), F32)
    hits = []
    for k in range(TOP_K):
        mx = sel.max(axis=-1, keepdims=True)
        ix = jnp.where(sel == mx, lane, float(LANES)).min(axis=-1, keepdims=True)
        hit = lane == ix
        hits.append(hit)
        val = jnp.where(hit, scores, 0.0).sum(axis=-1, keepdims=True)
        sel = jnp.where(hit, -jnp.inf, sel)
        chosen = jnp.where(hit, 1.0, chosen)
        idx_out = jnp.where(lane == k, ix, idx_out)
        val_out = jnp.where(lane == k, val, val_out)
        total = total + val
    idx_ref[...] = idx_out.astype(jnp.int32)
    gate_ref[...] = val_out / total * ROUTED_SCALE
    before = _dot(tri_ref[...], chosen.astype(BF16))
    rank_out = jnp.zeros((tm, LANES), F32)
    for k in range(TOP_K):
        rk = jnp.where(hits[k], before, 0.0).sum(axis=-1, keepdims=True)
        rank_out = jnp.where(lane == k, rk, rank_out)
    rank_ref[...] = rank_out
    cnt_ref[...] = chosen.sum(axis=0, keepdims=True)


def _router(h, rw, rb, tm):
    t, d = h.shape
    nt = t // tm
    tri = jnp.asarray(np.tril(np.ones((tm, tm), np.float32), -1), BF16)
    row = pl.BlockSpec((tm, LANES), lambda i: (i, 0))
    return pl.pallas_call(
        _router_kernel,
        out_shape=(jax.ShapeDtypeStruct((t, LANES), jnp.int32), jax.ShapeDtypeStruct((t, LANES), F32),
                   jax.ShapeDtypeStruct((t, LANES), F32), jax.ShapeDtypeStruct((nt, 1, LANES), F32)),
        grid=(nt,),
        in_specs=[pl.BlockSpec((tm, d), lambda i: (i, 0)),
                  pl.BlockSpec((d, LANES), lambda i: (0, 0)),
                  pl.BlockSpec((1, LANES), lambda i: (0, 0)),
                  pl.BlockSpec((tm, tm), lambda i: (0, 0))],
        out_specs=(row, row, row, pl.BlockSpec((None, 1, LANES), lambda i: (i, 0, 0))),
        compiler_params=_cparams(1), name="moe_router",
    )(h, rw, rb, tri)


def _dest_kernel(idx_ref, rank_ref, base_ref, o_ref):
    tm = idx_ref.shape[0]
    lane = lax.broadcasted_iota(jnp.int32, (tm, LANES), 1).astype(F32)
    idx = idx_ref[...].astype(F32)
    base = base_ref[...]
    out = rank_ref[...]
    for k in range(TOP_K):
        ek = jnp.where(lane == k, idx, 0).sum(axis=-1, keepdims=True)
        bk = jnp.where(lane == ek, base, 0.0).sum(axis=-1, keepdims=True)
        out = jnp.where(lane == k, out + bk, out)
    o_ref[...] = out.astype(jnp.int32)


def _dest(idx, rank, base, tm):
    t = idx.shape[0]
    row = pl.BlockSpec((tm, LANES), lambda i: (i, 0))
    return pl.pallas_call(
        _dest_kernel,
        out_shape=jax.ShapeDtypeStruct((t, LANES), jnp.int32),
        grid=(t // tm,),
        in_specs=[row, row, pl.BlockSpec((None, 1, LANES), lambda i: (i, 0, 0))],
        out_specs=row,
        compiler_params=_cparams(1), name="moe_dest",
    )(idx, rank, base)


def _expert_kernel(be_ref, nb_ref, x_ref, wg_ref, wu_ref, wd_ref, o_ref, wg_s, wu_s, wd_s):
    i = pl.program_id(0)
    active = i < nb_ref[0]
    new_expert = (i == 0) | (be_ref[i] != be_ref[jnp.maximum(i - 1, 0)])

    @pl.when(active & new_expert)
    def _():
        wg_s[...] = wg_ref[...].astype(BF16)
        wu_s[...] = wu_ref[...].astype(BF16)
        wd_s[...] = wd_ref[...].astype(BF16)

    @pl.when(active)
    def _():
        x = x_ref[...]
        a = _dot(x, wg_s[...])
        a = a * jax.nn.sigmoid(a) * _dot(x, wu_s[...])
        o_ref[...] = _dot(a.astype(BF16), wd_s[...]).astype(o_ref.dtype)

    @pl.when(jnp.logical_not(active))
    def _():
        o_ref[...] = jnp.zeros(o_ref.shape, o_ref.dtype)


def _expert_ffn(xg, block_exp, n_used, wg, wu, wd, tb):
    n, d = xg.shape
    de = wg.shape[2]
    return pl.pallas_call(
        _expert_kernel,
        out_shape=jax.ShapeDtypeStruct((n, d), BF16),
        grid_spec=pltpu.PrefetchScalarGridSpec(
            num_scalar_prefetch=2,
            grid=(n // tb,),
            in_specs=[pl.BlockSpec((tb, d), lambda i, be, nb: (i, 0)),
                      pl.BlockSpec((None, d, de), lambda i, be, nb: (be[i], 0, 0)),
                      pl.BlockSpec((None, d, de), lambda i, be, nb: (be[i], 0, 0)),
                      pl.BlockSpec((None, de, d), lambda i, be, nb: (be[i], 0, 0))],
            out_specs=pl.BlockSpec((tb, d), lambda i, be, nb: (i, 0)),
            scratch_shapes=[pltpu.VMEM((d, de), BF16), pltpu.VMEM((d, de), BF16), pltpu.VMEM((de, d), BF16)]),
        compiler_params=_cparams(1), name="expert_ffn",
    )(block_exp, n_used, xg, wg, wu, wd)


def _moe_combine_kernel(x_ref, mgate_ref, yg_ref, gate_ref, s_ref, o_ref):
    d = x_ref.shape[1]
    acc = s_ref[...].astype(F32)
    gates = gate_ref[...]
    for k in range(TOP_K):
        acc = acc + gates[:, k:k + 1] * yg_ref[:, k * d:(k + 1) * d].astype(F32)
    o_ref[...] = x_ref[...] + mgate_ref[...] * acc


def _moe_combine(x, mod3, mod_row, yg, gates, shared, tile_off, tm):
    m, d = x.shape
    tok = lambda w: pl.BlockSpec((tm, w), lambda i: (i + tile_off, 0))
    return pl.pallas_call(
        _moe_combine_kernel,
        out_shape=jax.ShapeDtypeStruct((m, d), F32),
        grid=(m // tm,),
        in_specs=[pl.BlockSpec((tm, d), lambda i: (i, 0)),
                  pl.BlockSpec((None, 1, d), lambda i: (mod_row(i) * 6 + 5, 0, 0)),
                  tok(TOP_K * d), tok(LANES), tok(d)],
        out_specs=pl.BlockSpec((tm, d), lambda i: (i, 0)),
        compiler_params=_cparams(1), name="moe_combine",
    )(x, mod3, yg, gates, shared)


ROUTER_TILE = 512


def _moe(h, rw, rb, wg, wu, wd, sg, su, sd, layer):
    t, d = h.shape
    idx, gates, rank, cnt = _router(h, rw, rb, ROUTER_TILE)
    n_blocks = t * TOP_K // MOE_BLOCK + N_EXPERTS
    n_slots = n_blocks * MOE_BLOCK
    counts = cnt[:, 0, :].astype(jnp.int32)
    total = counts.sum(axis=0)
    padded = (total + MOE_BLOCK - 1) // MOE_BLOCK * MOE_BLOCK
    pad_end = jnp.cumsum(padded)
    base = (pad_end - padded)[None, :] + jnp.cumsum(counts, axis=0) - counts
    dest = _dest(idx, rank, base.astype(F32)[:, None, :], ROUTER_TILE)[:, :TOP_K]
    block_start = jnp.arange(n_blocks, dtype=jnp.int32) * MOE_BLOCK
    block_exp = jnp.minimum((pad_end[None, :N_EXPERTS] <= block_start[:, None]).sum(axis=1), N_EXPERTS - 1)
    n_used = (pad_end[N_EXPERTS - 1] // MOE_BLOCK).astype(jnp.int32).reshape(1)
    tok = jnp.broadcast_to(jnp.arange(t, dtype=jnp.int32)[:, None], (t, TOP_K))
    slot_tok = jnp.zeros((n_slots,), jnp.int32).at[dest.reshape(-1)].set(tok.reshape(-1))
    xg = jnp.take(h, slot_tok, axis=0)
    y = _expert_ffn(xg, block_exp.astype(jnp.int32) + layer * N_EXPERTS, n_used, wg, wu, wd, MOE_BLOCK)
    yg = jnp.take(y, dest.reshape(-1), axis=0).reshape(t, TOP_K * d)
    shared = _expert_ffn(h, jnp.full((t // MOE_BLOCK,), layer, jnp.int32), jnp.full((1,), t // MOE_BLOCK, jnp.int32),
                         sg, su, sd, MOE_BLOCK)
    return yg, gates, shared


def _rope_tables(seq_len, gain):
    nf = MLA_ROPE // 4
    inv = ROPE_THETA ** (-jnp.arange(nf, dtype=F32) / nf)
    pos = jnp.arange(seq_len, dtype=jnp.int32)
    ang_r = (pos // GRID_W).astype(F32)[:, None] * inv
    ang_c = (pos % GRID_W).astype(F32)[:, None] * inv
    cos = jnp.concatenate([jnp.cos(ang_r)] * 2 + [jnp.cos(ang_c)] * 2, axis=-1)
    sin = jnp.concatenate([-jnp.sin(ang_r), jnp.sin(ang_r), -jnp.sin(ang_c), jnp.sin(ang_c)], axis=-1)
    g = gain.astype(F32)
    cg = cos * g[None]
    sg = sin * g[_ROPE_PARTNER][None]
    return jnp.tile(cg, (1, 2)), jnp.tile(sg, (1, 2))


def _rope_partner():
    nf = MLA_ROPE // 4
    idx = np.arange(MLA_ROPE)
    within = idx % (2 * nf)
    return np.where(within < nf, idx + nf, idx - nf)


_ROPE_PARTNER = _rope_partner()


def _in_splits():
    sizes = (NA_W, NA_W, NA_W, MLA_Q_LORA, MLA_KV_LORA, MLA_ROPE, CONV_CH, CONV_CH)
    offs = np.concatenate([[0], np.cumsum(sizes)])
    return {n: int(o) for n, o in zip(("q", "k", "v", "cq", "ckv", "kr", "a", "b", "g"), offs)}


def _prep_layer(l, dmodel, w_in, na_q_g, na_k_g, na_rpb, w_na_o, mla_q_lora_g, mla_w_uq, mla_kv_lora_g, mla_w_ukv,
                mla_q_g, mla_k_g, w_mla_o, conv_w_dw, conv_b_dw, conv_ln_g, conv_ln_b, w_conv_o, w_out,
                router_w, router_b, exp_w_gate, exp_w_up, exp_w_down, sh_w_gate, sh_w_up, sh_w_down, seq_len):
    o = _in_splits()
    w = w_in[l]
    cols = lambda a, n: w[:, a:a + n]
    p = {}
    p["w_p1"] = jnp.concatenate([cols(o["q"], 3 * NA_W), cols(o["a"], 2 * CONV_CH), cols(o["g"], 3 * dmodel)],
                                axis=1).astype(BF16)
    kr = cols(o["kr"], MLA_ROPE)
    krs = kr[:, _ROPE_PARTNER]
    p["w_p2"] = jnp.concatenate([cols(o["cq"], MLA_Q_LORA), kr, kr, krs, krs, cols(o["ckv"], MLA_KV_LORA)],
                                axis=1).astype(BF16)
    qscale = NA_HEAD_DIM ** -0.5 * LOG2E
    p["na_g"] = jnp.concatenate([jnp.tile(na_q_g[l].astype(F32) * qscale, NA_HEADS),
                                 jnp.tile(na_k_g[l].astype(F32), NA_HEADS)]).reshape(1, 2 * NA_W)
    p["na_bias"] = _na_bias_table(na_rpb[l])
    wq = mla_w_uq[l].reshape(MLA_Q_LORA, MLA_HEADS, MLA_QK)
    wq_rope = wq[:, :, MLA_NOPE:]
    flat = lambda a: a.reshape(a.shape[0], -1)
    p["w_uq"] = jnp.concatenate([flat(wq[:, :, :MLA_NOPE]), flat(wq_rope), flat(wq_rope[:, :, _ROPE_PARTNER])],
                                axis=1).astype(BF16)
    wkv = mla_w_ukv[l].reshape(MLA_KV_LORA, MLA_HEADS, MLA_NOPE + MLA_V)
    p["w_ukv"] = jnp.concatenate([flat(wkv[:, :, :MLA_NOPE]), flat(wkv[:, :, MLA_NOPE:])], axis=1).astype(BF16)
    p["q_lora_g"] = mla_q_lora_g[l].astype(F32).reshape(1, -1)
    p["kv_lora_g"] = mla_kv_lora_g[l].astype(F32).reshape(1, -1)
    mscale = MLA_QK ** -0.5 * LOG2E
    p["q_gn"] = (mla_q_g[l][:MLA_NOPE].astype(F32) * mscale).reshape(1, -1)
    p["k_gn"] = mla_k_g[l][:MLA_NOPE].astype(F32).reshape(1, -1)
    cgq, sgq = _rope_tables(seq_len, mla_q_g[l][MLA_NOPE:] * mscale)
    cgk, sgk = _rope_tables(seq_len, mla_k_g[l][MLA_NOPE:])
    p["q_tabs"] = (cgq, sgq)
    p["k_tabs"] = (cgk, sgk)
    p["q_tabs_c"] = (jnp.tile((mla_q_g[l][MLA_NOPE:].astype(F32) * mscale)[None], (1, 2)), jnp.zeros((1, LANES), F32))
    p["k_tabs_c"] = (jnp.tile(mla_k_g[l][MLA_NOPE:].astype(F32)[None], (1, 2)), jnp.zeros((1, LANES), F32))
    p["w_na_o"] = w_na_o[l].astype(BF16)
    p["w_mla_o"] = w_mla_o[l].astype(BF16)
    p["w_conv_o"] = w_conv_o[l].astype(BF16)
    p["w_out"] = w_out[l].astype(BF16)
    p["conv_w"] = jnp.concatenate([conv_w_dw[l].astype(F32), jnp.zeros((1, CONV_CH), F32)], axis=0)
    p["conv_b"] = conv_b_dw[l].astype(F32).reshape(1, -1)
    p["ln_g"] = conv_ln_g[l].astype(F32).reshape(1, -1)
    p["ln_b"] = conv_ln_b[l].astype(F32).reshape(1, -1)
    p["router_w"] = jnp.concatenate([router_w[l], jnp.zeros((dmodel, LANES - N_EXPERTS), router_w.dtype)],
                                    axis=1).astype(BF16)
    p["router_b"] = jnp.concatenate([router_b[l].astype(F32), jnp.zeros((LANES - N_EXPERTS,), F32)]).reshape(1, LANES)
    flat_e = lambda a: a.reshape((-1,) + a.shape[2:])
    p["exp"] = (flat_e(exp_w_gate), flat_e(exp_w_up), flat_e(exp_w_down))
    p["sh"] = (sh_w_gate, sh_w_up, sh_w_down)
    return p


def _group_matrices():
    i = np.arange(LANES)
    bd = (i[:, None] // NA_HEAD_DIM == i[None, :] // NA_HEAD_DIM)
    ones = np.ones((LANES, LANES), bool)
    e0 = np.broadcast_to((i < MLA_ROPE)[:, None], (LANES, LANES))
    e1 = np.broadcast_to((i >= MLA_ROPE)[:, None], (LANES, LANES))
    f = lambda m: jnp.asarray(m, BF16)
    return f(bd), (f(ones), f(e0), f(e1))


def _expand_tabs(tabs, tm):
    return tuple(jnp.broadcast_to(t, (tm, LANES)) if t.shape[0] == 1 else t for t in tabs)


def _mixer_inputs(h, p, mats, bd, seq, ctx):
    tm_mm = min(1024, h.shape[0])
    p1 =_matmul(h, p["w_p1"], BF16, tm_mm, 1024, "in_proj_1")
    p2 = _matmul(h, p["w_p2"], BF16, tm_mm, P2_W // 2, "in_proj_2")
    qk = _headnorm(p1, p["na_g"], bd, min(512, h.shape[0]))
    tr = 256
    qt = _expand_tabs(p["q_tabs_c"], tr) if ctx else p["q_tabs"]
    kt = _expand_tabs(p["k_tabs_c"], tr) if ctx else p["k_tabs"]
    tseq = tr if ctx else seq
    km, vm = _mla_kv(p2, p["kv_lora_g"], p["w_ukv"], p["k_gn"], kt[0], kt[1], mats, tseq, tr)
    return p1, p2, qk, km, vm, (qt, tseq, tr)


def kernel(x, c, ctx, c_ctx, ada_w, ada_b, norm1_g, norm2_g, w_in, na_q_g, na_k_g, na_rpb, w_na_o, mla_q_lora_g, mla_w_uq, mla_kv_lora_g, mla_w_ukv, mla_q_g, mla_k_g, w_mla_o, conv_w_dw, conv_b_dw, conv_ln_g, conv_ln_b, w_conv_o, w_out, router_w, router_b, exp_w_gate, exp_w_up, exp_w_down, sh_w_gate, sh_w_up, sh_w_down):
    batch, seq, d = x.shape
    lc = ctx.shape[1]
    depth = ada_w.shape[0]
    mod_rows = (batch + 1 + 7) // 8 * 8
    cc = jnp.concatenate([c, c_ctx[None], jnp.zeros((mod_rows - batch - 1, d), c.dtype)], axis=0)
    mod_all = _ada(cc, ada_w, ada_b)
    bd, mats = _group_matrices()
    xs = x.reshape(batch * seq, d)
    zs = ctx.reshape(batch * lc, d)
    tm_x = 512
    tm_c = 256
    x_row = lambda i: (i * tm_x) // seq
    c_row = lambda i: batch

    for l in range(depth):
        need_ctx = l < depth - 1
        p = _prep_layer(l, d, w_in, na_q_g, na_k_g, na_rpb, w_na_o, mla_q_lora_g, mla_w_uq, mla_kv_lora_g, mla_w_ukv,
                        mla_q_g, mla_k_g, w_mla_o, conv_w_dw, conv_b_dw, conv_ln_g, conv_ln_b, w_conv_o, w_out,
                        router_w, router_b, exp_w_gate, exp_w_up, exp_w_down, sh_w_gate, sh_w_up, sh_w_down, seq)
        mod3 = mod_all[l].reshape(mod_rows * 6, 1, d)
        hx = _norm_mod(xs, norm1_g[l], mod3, x_row, 1, 0, tm_x)
        hc = _norm_mod(zs, norm1_g[l], mod3, c_row, 1, 0, tm_c)
        p1x, p2x, qkx, kmx, vmx, (qtx, tsx, trx) = _mixer_inputs(hx, p, mats, bd, seq, False)
        p1c, p2c, qkc, kmc, vmc, (qtc, tsc, trc) = _mixer_inputs(hc, p, mats, bd, lc, True)
        o_na = _na_attention(qkx, p1x, qkc, p1c, p["na_bias"], batch)
        qmx = _mla_q(p2x, p["q_lora_g"], p["w_uq"], p["q_gn"], qtx[0], qtx[1], mats, tsx, trx)
        o_mla = _mla_attention(qmx, [(kmc, vmc), (kmx, vmx)], batch, 512)
        u = _conv_module(p1x, p["conv_w"], p["conv_b"], p["ln_g"], p["ln_b"], seq, 256)
        mrg = _merge(o_na, o_mla, u, p1x, p["w_na_o"], p["w_mla_o"], p["w_conv_o"], 1024, 512)
        xs, hx2 = _out_proj(mrg, p["w_out"], xs, mod3, x_row, norm2_g[l], tm_x)
        if need_ctx:
            o_na_c = _na_ctx_attention(qkc, p1c, batch)
            qmc = _mla_q(p2c, p["q_lora_g"], p["w_uq"], p["q_gn"], qtc[0], qtc[1], mats, tsc, trc)
            o_mla_c = _mla_attention(qmc, [(kmc, vmc)], batch, lc)
            u_c = _conv_module(p1c, p["conv_w"], p["conv_b"], p["ln_g"], p["ln_b"], lc, lc)
            mrg_c = _merge(o_na_c, o_mla_c, u_c, p1c, p["w_na_o"], p["w_mla_o"], p["w_conv_o"], 1024, 512)
            zs, hc2 = _out_proj(mrg_c, p["w_out"], zs, mod3, c_row, norm2_g[l], tm_c)
            tokens = jnp.concatenate([hc2, hx2], axis=0)
        else:
            tokens = hx2
        yg, gates, shared = _moe(tokens, p["router_w"], p["router_b"], *p["exp"], *p["sh"], l)
        tm_cmb = 256
        x_row_cmb = lambda i: (i * tm_cmb) // seq
        if need_ctx:
            zs = _moe_combine(zs, mod3, c_row, yg, gates, shared, 0, tm_cmb)
            xs = _moe_combine(xs, mod3, x_row_cmb, yg, gates, shared, batch * lc // tm_cmb, tm_cmb)
        else:
            xs = _moe_combine(xs, mod3, x_row_cmb, yg, gates, shared, 0, tm_cmb)
    return xs.reshape(batch, seq, d)
```

```python
import functools

import numpy as np
import jax
import jax.numpy as jnp
from jax import lax
from jax.experimental import pallas as pl
from jax.experimental.pallas import tpu as pltpu

F32 = jnp.float32
BF16 = jnp.bfloat16

GRID_W = 64
EPS = 1e-6
NEG_INF = -1e30
NA_HEADS = 16
NA_HEAD_DIM = 64
NA_W = NA_HEADS * NA_HEAD_DIM
NA_WIN_ROWS = 8
NA_WIN_COLS = 16
MLA_HEADS = 16
MLA_NOPE = 128
MLA_ROPE = 64
MLA_QK = MLA_NOPE + MLA_ROPE
MLA_V = 128
MLA_Q_LORA = 768
MLA_KV_LORA = 512
ROPE_THETA = 10000.0
CONV_CH = 1024
CONV_WIDTH = 31
CONV_HALO = 16
N_EXPERTS = 64
TOP_K = 8
D_EXPERT = 512
ROUTED_SCALE = 2.5
MOE_BLOCK = 256
LANES = 128
MLA_HEAD_PAD = 2 * LANES
VMEM_LIMIT = 56 * 1024 * 1024

P1_Q, P1_K, P1_V, P1_A, P1_B, P1_GNA, P1_GMLA, P1_GCV = 0, 1024, 2048, 3072, 4096, 5120, 7168, 9216
P1_W = 11264
P2_CQ, P2_KR, P2_KRS, P2_CKV = 0, 768, 896, 1024
P2_W = 1536


def _cparams(n_axes):
    return pltpu.CompilerParams(dimension_semantics=("arbitrary",) * n_axes, vmem_limit_bytes=VMEM_LIMIT)


def _dot(a, b):
    return jnp.dot(a, b, preferred_element_type=F32)


def _dot_nt(a, b):
    return lax.dot_general(a, b, (((1,), (1,)), ((), ())), preferred_element_type=F32)


def _dot_hilo(x, m):
    hi = x.astype(BF16)
    lo = (x - hi.astype(F32)).astype(BF16)
    return _dot(hi, m) + _dot(lo, m)


def _ada_kernel(c_ref, w_ref, b_ref, o_ref):
    c = c_ref[...]
    a = (c * jax.nn.sigmoid(c)).astype(BF16)
    o_ref[...] = _dot(a, w_ref[...].astype(BF16)) + b_ref[...]


def _ada(cc, ada_w, ada_b):
    nl, d, n = ada_w.shape
    r = cc.shape[0]
    tn = 1024
    return pl.pallas_call(
        _ada_kernel,
        out_shape=jax.ShapeDtypeStruct((nl, r, n), F32),
        grid=(nl, n // tn),
        in_specs=[pl.BlockSpec((r, d), lambda l, j: (0, 0)),
                  pl.BlockSpec((None, d, tn), lambda l, j: (l, 0, j)),
                  pl.BlockSpec((None, 1, tn), lambda l, j: (l, 0, j))],
        out_specs=pl.BlockSpec((None, r, tn), lambda l, j: (l, 0, j)),
        compiler_params=_cparams(2), name="ada_mod",
    )(cc, ada_w, ada_b.reshape(nl, 1, n))


def _norm_mod_kernel(x_ref, g_ref, sc_ref, sh_ref, o_ref):
    x = x_ref[...]
    y = x * lax.rsqrt(jnp.mean(x * x, axis=-1, keepdims=True) + EPS) * g_ref[...]
    o_ref[...] = (y * (1.0 + sc_ref[...]) + sh_ref[...]).astype(o_ref.dtype)


def _norm_mod(x, g, mod3, mod_row, k_sc, k_sh, tm):
    m, d = x.shape
    return pl.pallas_call(
        _norm_mod_kernel,
        out_shape=jax.ShapeDtypeStruct((m, d), BF16),
        grid=(m // tm,),
        in_specs=[pl.BlockSpec((tm, d), lambda i: (i, 0)),
                  pl.BlockSpec((1, d), lambda i: (0, 0)),
                  pl.BlockSpec((None, 1, d), lambda i: (mod_row(i) * 6 + k_sc, 0, 0)),
                  pl.BlockSpec((None, 1, d), lambda i: (mod_row(i) * 6 + k_sh, 0, 0))],
        out_specs=pl.BlockSpec((tm, d), lambda i: (i, 0)),
        compiler_params=_cparams(1), name="norm_mod",
    )(x, g.reshape(1, d), mod3, mod3)


def _mm_kernel(a_ref, w_ref, o_ref):
    o_ref[...] = _dot(a_ref[...], w_ref[...]).astype(o_ref.dtype)


def _matmul(a, w, out_dtype, tm, tn, name):
    m, k = a.shape
    n = w.shape[1]
    return pl.pallas_call(
        _mm_kernel,
        out_shape=jax.ShapeDtypeStruct((m, n), out_dtype),
        grid=(m // tm, n // tn),
        in_specs=[pl.BlockSpec((tm, k), lambda i, j: (i, 0)),
                  pl.BlockSpec((k, tn), lambda i, j: (0, j))],
        out_specs=pl.BlockSpec((tm, tn), lambda i, j: (i, j)),
        compiler_params=_cparams(2), name=name,
    )(a, w)


def _headnorm_kernel(x_ref, g_ref, bd_ref, o_ref):
    bd = bd_ref[...]
    for c in range(x_ref.shape[1] // LANES):
        sl = slice(c * LANES, (c + 1) * LANES)
        x = x_ref[:, sl].astype(F32)
        ss = _dot_hilo(x * x, bd)
        y = x * lax.rsqrt(ss * (1.0 / NA_HEAD_DIM) + EPS) * g_ref[:, sl]
        o_ref[:, sl] = y.astype(o_ref.dtype)


def _headnorm(p1, g_row, bd, tm):
    m = p1.shape[0]
    w = g_row.shape[1]
    return pl.pallas_call(
        _headnorm_kernel,
        out_shape=jax.ShapeDtypeStruct((m, w), BF16),
        grid=(m // tm,),
        in_specs=[pl.BlockSpec((tm, w), lambda i: (i, 0)),
                  pl.BlockSpec((1, w), lambda i: (0, 0)),
                  pl.BlockSpec((LANES, LANES), lambda i: (0, 0))],
        out_specs=pl.BlockSpec((tm, w), lambda i: (i, 0)),
        compiler_params=_cparams(1), name="na_headnorm",
    )(p1, g_row, bd)


LOG2E = 1.4426950408889634


def _softmax_pv(scores, values):
    m = scores[0].max(axis=-1, keepdims=True)
    for s in scores[1:]:
        m = jnp.maximum(m, s.max(axis=-1, keepdims=True))
    l = None
    o = None
    for s, v in zip(scores, values):
        p = jnp.exp2(s - m)
        ps = p.sum(axis=-1, keepdims=True)
        po = _dot(p.astype(BF16), v)
        l = ps if l is None else l + ps
        o = po if o is None else o + po
    return o / l


def _dot_tn(a, b):
    return lax.dot_general(a, b, (((0,), (0,)), ((), ())), preferred_element_type=F32)


def _na_kernel(q_ref, kx_ref, vx_ref, kc_ref, vc_ref, bias_ref, o_ref):
    rows = kx_ref.shape[0] // GRID_W
    r = pl.program_id(1)
    r0 = jnp.clip(r - NA_WIN_ROWS // 2, 0, rows - NA_WIN_ROWS)
    kstart = pl.multiple_of(r0 * GRID_W, GRID_W)
    nwin = NA_WIN_ROWS * GRID_W
    lo = lax.broadcasted_iota(jnp.int32, (GRID_W, LANES), 1) < NA_HEAD_DIM
    zero = jnp.zeros((GRID_W, LANES), q_ref.dtype)
    for p in range(NA_W // LANES):
        sl = slice(p * LANES, (p + 1) * LANES)
        q2 = q_ref[:, sl]
        qs = jnp.concatenate([jnp.where(lo, q2, zero), jnp.where(lo, zero, q2)], axis=0)
        kw = kx_ref[pl.ds(kstart, nwin), sl]
        vw = vx_ref[pl.ds(kstart, nwin), sl]
        s_w = _dot_nt(kw, qs) + bias_ref[p]
        s_c = _dot_nt(kc_ref[:, sl], qs)
        m = jnp.maximum(s_w.max(axis=0, keepdims=True), s_c.max(axis=0, keepdims=True))
        p_w = jnp.exp2(s_w - m)
        p_c = jnp.exp2(s_c - m)
        inv = 1.0 / (p_w.sum(axis=0, keepdims=True) + p_c.sum(axis=0, keepdims=True))
        o2 = _dot_tn((p_w * inv).astype(BF16), vw) + _dot_tn((p_c * inv).astype(BF16), vc_ref[:, sl])
        o_ref[:, sl] = jnp.where(lo, o2[:GRID_W], o2[GRID_W:]).astype(o_ref.dtype)


def _na_attention(qk_x, p1_x, qk_c, p1_c, bias_tab, batch):
    l = qk_x.shape[0] // batch
    lc = qk_c.shape[0] // batch
    rows = l // GRID_W
    nwin = NA_WIN_ROWS * GRID_W

    def bias_idx(b, r):
        r0 = jnp.clip(r - NA_WIN_ROWS // 2, 0, rows - NA_WIN_ROWS)
        return (r - r0, 0, 0, 0)

    return pl.pallas_call(
        _na_kernel,
        out_shape=jax.ShapeDtypeStruct((batch * l, NA_W), BF16),
        grid=(batch, rows),
        in_specs=[pl.BlockSpec((GRID_W, NA_W), lambda b, r: (b * rows + r, 0)),
                  pl.BlockSpec((l, NA_W), lambda b, r: (b, 1)),
                  pl.BlockSpec((l, NA_W), lambda b, r: (b, P1_V // NA_W)),
                  pl.BlockSpec((lc, NA_W), lambda b, r: (b, 1)),
                  pl.BlockSpec((lc, NA_W), lambda b, r: (b, P1_V // NA_W)),
                  pl.BlockSpec((None, NA_HEADS // 2, nwin, LANES), bias_idx)],
        out_specs=pl.BlockSpec((GRID_W, NA_W), lambda b, r: (b * rows + r, 0)),
        compiler_params=_cparams(2), name="na_attention",
    )(qk_x, qk_x, p1_x, qk_c, p1_c, bias_tab)


def _na_ctx_kernel(q_ref, k_ref, v_ref, o_ref):
    n = q_ref.shape[0]
    lo = lax.broadcasted_iota(jnp.int32, (n, LANES), 1) < NA_HEAD_DIM
    zero = jnp.zeros((n, LANES), q_ref.dtype)
    for p in range(NA_W // LANES):
        sl = slice(p * LANES, (p + 1) * LANES)
        q2 = q_ref[:, sl]
        k = k_ref[:, sl]
        v = v_ref[:, sl]
        outs = []
        for hh in range(2):
            qm = jnp.where(lo, q2, zero) if hh == 0 else jnp.where(lo, zero, q2)
            outs.append(_softmax_pv([_dot_nt(qm, k)], [v]))
        o_ref[:, sl] = jnp.where(lo, outs[0], outs[1]).astype(o_ref.dtype)


def _na_ctx_attention(qk_c, p1_c, batch):
    lc = qk_c.shape[0] // batch
    return pl.pallas_call(
        _na_ctx_kernel,
        out_shape=jax.ShapeDtypeStruct((batch * lc, NA_W), BF16),
        grid=(batch,),
        in_specs=[pl.BlockSpec((lc, NA_W), lambda b: (b, 0)),
                  pl.BlockSpec((lc, NA_W), lambda b: (b, 1)),
                  pl.BlockSpec((lc, NA_W), lambda b: (b, P1_V // NA_W))],
        out_specs=pl.BlockSpec((lc, NA_W), lambda b: (b, 0)),
        compiler_params=_cparams(1), name="na_ctx_attention",
    )(qk_c, qk_c, p1_c)


def _na_bias_table(rpb):
    cls = np.arange(NA_WIN_ROWS)[:, None, None, None]
    qc = np.arange(GRID_W)[None, :, None, None]
    w = np.arange(NA_WIN_ROWS)[None, None, :, None]
    kc = np.arange(GRID_W)[None, None, None, :]
    c0 = np.clip(qc - NA_WIN_COLS // 2, 0, GRID_W - NA_WIN_COLS)
    ok = (kc >= c0) & (kc < c0 + NA_WIN_COLS)
    dr = np.broadcast_to(w - cls + NA_WIN_ROWS - 1, (NA_WIN_ROWS, GRID_W, NA_WIN_ROWS, GRID_W))
    dc = np.broadcast_to(np.clip(kc - qc + NA_WIN_COLS - 1, 0, 2 * NA_WIN_COLS - 2), dr.shape)
    ok = np.broadcast_to(ok, dr.shape)
    tab = rpb.astype(F32)[:, dr, dc] * LOG2E
    tab = jnp.where(ok[None], tab, NEG_INF)
    tab = tab.reshape(NA_HEADS // 2, 2, NA_WIN_ROWS, GRID_W, NA_WIN_ROWS * GRID_W)
    tab = jnp.transpose(tab, (2, 0, 4, 1, 3))
    return tab.reshape(NA_WIN_ROWS, NA_HEADS // 2, NA_WIN_ROWS * GRID_W, LANES)


def _mla_q_kernel(p2_ref, gl_ref, w_ref, gn_ref, cg_ref, sg_ref, ones_ref, e0_ref, e1_ref, o_ref):
    cq = p2_ref[...].astype(F32)
    cqn = cq * lax.rsqrt(jnp.mean(cq * cq, axis=-1, keepdims=True) + EPS) * gl_ref[...]
    y = _dot(cqn.astype(BF16), w_ref[...])
    tm = y.shape[0]
    lo = lax.broadcasted_iota(jnp.int32, (tm, LANES), 1) < MLA_ROPE
    nope_w = MLA_HEADS * MLA_NOPE
    rope_w = MLA_HEADS * MLA_ROPE
    ones, e0, e1 = ones_ref[...], e0_ref[...], e1_ref[...]
    gn = gn_ref[...]
    for p in range(MLA_HEADS // 2):
        yr = y[:, nope_w + p * LANES: nope_w + (p + 1) * LANES]
        yrs = y[:, nope_w + rope_w + p * LANES: nope_w + rope_w + (p + 1) * LANES]
        rot = yr * cg_ref[...] + yrs * sg_ref[...]
        yr2 = yr * yr
        ssr = (_dot_hilo(yr2, e0), _dot_hilo(yr2, e1))
        for hh in range(2):
            h = 2 * p + hh
            nope = y[:, h * MLA_NOPE:(h + 1) * MLA_NOPE]
            tot = _dot_hilo(nope * nope, ones) + ssr[hh]
            scale = lax.rsqrt(tot * (1.0 / MLA_QK) + EPS)
            o_ref[:, h * MLA_HEAD_PAD: h * MLA_HEAD_PAD + LANES] = (nope * scale * gn).astype(o_ref.dtype)
            rs = rot * scale
            rs = jnp.where(lo, rs, 0.0) if hh == 0 else jnp.where(lo, 0.0, rs)
            o_ref[:, h * MLA_HEAD_PAD + LANES: (h + 1) * MLA_HEAD_PAD] = rs.astype(o_ref.dtype)


def _mla_q(p2, g_lora, w_uq, g_nope, cg, sg, mats, seq, tm):
    m = p2.shape[0]
    nt = seq // tm
    ones, e0, e1 = mats
    full = lambda shape: pl.BlockSpec(shape, lambda i: (0,) * len(shape))
    return pl.pallas_call(
        _mla_q_kernel,
        out_shape=jax.ShapeDtypeStruct((m, MLA_HEADS * MLA_HEAD_PAD), BF16),
        grid=(m // tm,),
        in_specs=[pl.BlockSpec((tm, MLA_Q_LORA), lambda i: (i, 0)),
                  full((1, MLA_Q_LORA)),
                  full(w_uq.shape),
                  full((1, LANES)),
                  pl.BlockSpec((tm, LANES), lambda i: (i % nt, 0)),
                  pl.BlockSpec((tm, LANES), lambda i: (i % nt, 0)),
                  full((LANES, LANES)), full((LANES, LANES)), full((LANES, LANES))],
        out_specs=pl.BlockSpec((tm, MLA_HEADS * MLA_HEAD_PAD), lambda i: (i, 0)),
        compiler_params=_cparams(1), name="mla_q_proj",
    )(p2, g_lora, w_uq, g_nope, cg, sg, ones, e0, e1)


def _mla_kv_kernel(ckv_ref, kr_ref, krs_ref, gl_ref, w_ref, gn_ref, cg_ref, sg_ref, ones_ref, e0_ref, k_ref, v_ref):
    ckv = ckv_ref[...].astype(F32)
    cn = ckv * lax.rsqrt(jnp.mean(ckv * ckv, axis=-1, keepdims=True) + EPS) * gl_ref[...]
    y = _dot(cn.astype(BF16), w_ref[...])
    tm = y.shape[0]
    lo = lax.broadcasted_iota(jnp.int32, (tm, LANES), 1) < MLA_ROPE
    nope_w = MLA_HEADS * MLA_NOPE
    v_ref[...] = y[:, nope_w:].astype(v_ref.dtype)
    kr = kr_ref[...].astype(F32)
    rot = kr * cg_ref[...] + krs_ref[...].astype(F32) * sg_ref[...]
    ssr = _dot_hilo(kr * kr, e0_ref[...])
    ones = ones_ref[...]
    gn = gn_ref[...]
    for h in range(MLA_HEADS):
        nope = y[:, h * MLA_NOPE:(h + 1) * MLA_NOPE]
        tot = _dot_hilo(nope * nope, ones) + ssr
        scale = lax.rsqrt(tot * (1.0 / MLA_QK) + EPS)
        k_ref[:, h * MLA_HEAD_PAD: h * MLA_HEAD_PAD + LANES] = (nope * scale * gn).astype(k_ref.dtype)
        rs = rot * scale
        rs = jnp.where(lo, rs, 0.0) if h % 2 == 0 else jnp.where(lo, 0.0, rs)
        k_ref[:, h * MLA_HEAD_PAD + LANES: (h + 1) * MLA_HEAD_PAD] = rs.astype(k_ref.dtype)


def _mla_kv(p2, g_lora, w_ukv, g_nope, cg, sg, mats, seq, tm):
    m = p2.shape[0]
    nt = seq // tm
    ones, e0, _ = mats
    full = lambda shape: pl.BlockSpec(shape, lambda i: (0,) * len(shape))
    return pl.pallas_call(
        _mla_kv_kernel,
        out_shape=(jax.ShapeDtypeStruct((m, MLA_HEADS * MLA_HEAD_PAD), BF16),
                   jax.ShapeDtypeStruct((m, MLA_HEADS * MLA_V), BF16)),
        grid=(m // tm,),
        in_specs=[pl.BlockSpec((tm, MLA_KV_LORA), lambda i: (i, P2_CKV // MLA_KV_LORA)),
                  pl.BlockSpec((tm, LANES), lambda i: (i, P2_KR // LANES)),
                  pl.BlockSpec((tm, LANES), lambda i: (i, P2_KRS // LANES)),
                  full((1, MLA_KV_LORA)),
                  full(w_ukv.shape),
                  full((1, LANES)),
                  pl.BlockSpec((tm, LANES), lambda i: (i % nt, 0)),
                  pl.BlockSpec((tm, LANES), lambda i: (i % nt, 0)),
                  full((LANES, LANES)), full((LANES, LANES))],
        out_specs=(pl.BlockSpec((tm, MLA_HEADS * MLA_HEAD_PAD), lambda i: (i, 0)),
                   pl.BlockSpec((tm, MLA_HEADS * MLA_V), lambda i: (i, 0))),
        compiler_params=_cparams(1), name="mla_kv_proj",
    )(p2, p2, p2, g_lora, w_ukv, g_nope, cg, sg, ones, e0)


def _mla_attn_kernel(*refs, n_kv):
    q_ref = refs[0]
    kv = refs[1:1 + 2 * n_kv]
    o_ref = refs[1 + 2 * n_kv]
    for hh in range(MLA_HEADS_PER_STEP):
        qsl = slice(hh * MLA_HEAD_PAD, (hh + 1) * MLA_HEAD_PAD)
        vsl = slice(hh * MLA_V, (hh + 1) * MLA_V)
        q = q_ref[:, qsl]
        scores = [_dot_nt(q, kv[2 * i][:, qsl]) for i in range(n_kv)]
        values = [kv[2 * i + 1][:, vsl] for i in range(n_kv)]
        o_ref[:, vsl] = _softmax_pv(scores, values).astype(o_ref.dtype)


MLA_HEADS_PER_STEP = 2


def _mla_attention(q, kvs, batch, tq):
    lq = q.shape[0] // batch
    nq = lq // tq
    hs = MLA_HEADS_PER_STEP
    in_specs = [pl.BlockSpec((tq, hs * MLA_HEAD_PAD), lambda b, h, j: (b * nq + j, h))]
    args = [q]
    for k, v in kvs:
        s = k.shape[0] // batch
        in_specs.append(pl.BlockSpec((s, hs * MLA_HEAD_PAD), lambda b, h, j: (b, h)))
        in_specs.append(pl.BlockSpec((s, hs * MLA_V), lambda b, h, j: (b, h)))
        args += [k, v]
    return pl.pallas_call(
        functools.partial(_mla_attn_kernel, n_kv=len(kvs)),
        out_shape=jax.ShapeDtypeStruct((batch * lq, MLA_HEADS * MLA_V), BF16),
        grid=(batch, MLA_HEADS // hs, nq),
        in_specs=in_specs,
        out_specs=pl.BlockSpec((tq, hs * MLA_V), lambda b, h, j: (b * nq + j, h)),
        compiler_params=_cparams(3), name="mla_attention",
    )(*args)


def _conv_kernel(ap_ref, a_ref, an_ref, bp_ref, b_ref, bn_ref, w_ref, cb_ref, g_ref, beta_ref, o_ref, u_scr, y_scr,
                 *, n_tiles):
    j = pl.program_id(1)
    tt = a_ref.shape[0]
    glu = lambda a, b: a[...].astype(F32) * jax.nn.sigmoid(b[...].astype(F32))
    u_scr[0:CONV_HALO, :] = jnp.where(j > 0, glu(ap_ref, bp_ref), 0.0)
    u_scr[CONV_HALO:CONV_HALO + tt, :] = glu(a_ref, b_ref)
    u_scr[CONV_HALO + tt:, :] = jnp.where(j < n_tiles - 1, glu(an_ref, bn_ref), 0.0)
    off = CONV_HALO - CONV_WIDTH // 2
    for c in range(CONV_CH // LANES):
        sl = slice(c * LANES, (c + 1) * LANES)
        acc = jnp.zeros((tt, LANES), F32)
        for k in range(CONV_WIDTH):
            acc = acc + u_scr[off + k: off + k + tt, sl] * w_ref[k:k + 1, sl]
        y_scr[:, sl] = acc + cb_ref[:, sl]
    y = y_scr[...]
    mu = jnp.mean(y, axis=-1, keepdims=True)
    d = y - mu
    var = jnp.mean(d * d, axis=-1, keepdims=True)
    z = d * lax.rsqrt(var + EPS) * g_ref[...] + beta_ref[...]
    o_ref[...] = (z * jax.nn.sigmoid(z)).astype(o_ref.dtype)


def _conv_module(p1, w_dw, b_dw, ln_g, ln_b, seq, tt):
    m = p1.shape[0]
    batch = m // seq
    n_tiles = seq // tt
    hb = tt // CONV_HALO
    n_hblk = m // CONV_HALO
    ca, cb = P1_A // CONV_CH, P1_B // CONV_CH

    def prev_idx(c):
        return lambda b, j: (jnp.maximum((b * n_tiles + j) * hb - 1, 0), c)

    def next_idx(c):
        return lambda b, j: (jnp.minimum((b * n_tiles + j + 1) * hb, n_hblk - 1), c)

    cur = lambda c: (lambda b, j: (b * n_tiles + j, c))
    full = lambda shape: pl.BlockSpec(shape, lambda b, j: (0,) * len(shape))
    return pl.pallas_call(
        functools.partial(_conv_kernel, n_tiles=n_tiles),
        out_shape=jax.ShapeDtypeStruct((m, CONV_CH), BF16),
        grid=(batch, n_tiles),
        in_specs=[pl.BlockSpec((CONV_HALO, CONV_CH), prev_idx(ca)),
                  pl.BlockSpec((tt, CONV_CH), cur(ca)),
                  pl.BlockSpec((CONV_HALO, CONV_CH), next_idx(ca)),
                  pl.BlockSpec((CONV_HALO, CONV_CH), prev_idx(cb)),
                  pl.BlockSpec((tt, CONV_CH), cur(cb)),
                  pl.BlockSpec((CONV_HALO, CONV_CH), next_idx(cb)),
                  full((32, CONV_CH)), full((1, CONV_CH)), full((1, CONV_CH)), full((1, CONV_CH))],
        out_specs=pl.BlockSpec((tt, CONV_CH), lambda b, j: (b * n_tiles + j, 0)),
        scratch_shapes=[pltpu.VMEM((tt + 2 * CONV_HALO, CONV_CH), F32), pltpu.VMEM((tt, CONV_CH), F32)],
        compiler_params=_cparams(2), name="conv_module",
    )(p1, p1, p1, p1, p1, p1, w_dw, b_dw, ln_g, ln_b)


def _merge_kernel(ona_ref, omla_ref, u_ref, gna_ref, gmla_ref, gcv_ref, wna_ref, wmla_ref, wcv_ref, o_ref):
    sig = lambda r: jax.nn.sigmoid(r[...].astype(F32))
    m = sig(gna_ref) * _dot(ona_ref[...], wna_ref[...])
    m = m + sig(gmla_ref) * _dot(omla_ref[...], wmla_ref[...])
    m = m + sig(gcv_ref) * _dot(u_ref[...], wcv_ref[...])
    o_ref[...] = m.astype(o_ref.dtype)


def _merge(o_na, o_mla, u, p1, w_na_o, w_mla_o, w_conv_o, tm, tn):
    m = o_na.shape[0]
    d = w_na_o.shape[1]
    tm = min(tm, m)
    row = lambda k: pl.BlockSpec((tm, k), lambda i, j: (i, 0))
    gate = lambda off: pl.BlockSpec((tm, tn), lambda i, j: (i, off // tn + j))
    wcol = lambda k: pl.BlockSpec((k, tn), lambda i, j: (0, j))
    return pl.pallas_call(
        _merge_kernel,
        out_shape=jax.ShapeDtypeStruct((m, d), BF16),
        grid=(m // tm, d // tn),
        in_specs=[row(o_na.shape[1]), row(o_mla.shape[1]), row(u.shape[1]),
                  gate(P1_GNA), gate(P1_GMLA), gate(P1_GCV),
                  wcol(w_na_o.shape[0]), wcol(w_mla_o.shape[0]), wcol(w_conv_o.shape[0])],
        out_specs=pl.BlockSpec((tm, tn), lambda i, j: (i, j)),
        compiler_params=_cparams(2), name="gated_merge",
    )(o_na, o_mla, u, p1, p1, p1, w_na_o, w_mla_o, w_conv_o)


def _out_proj_kernel(m_ref, w_ref, x_ref, gate_ref, g2_ref, sc_ref, sh_ref, xo_ref, ho_ref):
    x = x_ref[...] + gate_ref[...] * _dot(m_ref[...], w_ref[...])
    xo_ref[...] = x
    y = x * lax.rsqrt(jnp.mean(x * x, axis=-1, keepdims=True) + EPS) * g2_ref[...]
    ho_ref[...] = (y * (1.0 + sc_ref[...]) + sh_ref[...]).astype(ho_ref.dtype)


def _out_proj(mrg, w_out, x, mod3, mod_row, g2, tm):
    m, d = x.shape
    modspec = lambda k: pl.BlockSpec((None, 1, d), lambda i: (mod_row(i) * 6 + k, 0, 0))
    return pl.pallas_call(
        _out_proj_kernel,
        out_shape=(jax.ShapeDtypeStruct((m, d), F32), jax.ShapeDtypeStruct((m, d), BF16)),
        grid=(m // tm,),
        in_specs=[pl.BlockSpec((tm, d), lambda i: (i, 0)),
                  pl.BlockSpec((d, d), lambda i: (0, 0)),
                  pl.BlockSpec((tm, d), lambda i: (i, 0)),
                  modspec(2),
                  pl.BlockSpec((1, d), lambda i: (0, 0)),
                  modspec(4), modspec(3)],
        out_specs=(pl.BlockSpec((tm, d), lambda i: (i, 0)), pl.BlockSpec((tm, d), lambda i: (i, 0))),
        compiler_params=_cparams(1), name="out_proj_residual",
    )(mrg, w_out, x, mod3, g2.reshape(1, d), mod3, mod3)


def _router_kernel(h_ref, w_ref, b_ref, tri_ref, rank_ref, gate_ref, cnt_ref):
    scores = jax.nn.sigmoid(_dot(h_ref[...], w_ref[...]))
    tm = scores.shape[0]
    lane = lax.broadcasted_iota(jnp.int32, (tm, LANES), 1).astype(F32)
    sel = jnp.where(lane < N_EXPERTS, scores + b_ref[...], -jnp.inf)
    total = jnp.zeros((tm, 1), F32)
    chosen = jnp.zeros((tm, LANES), F32)
    for _ in range(TOP_K):
        mx = sel.max(axis=-1, keepdims=True)
        ix = jnp.where(sel == mx, lane, float(LANES)).min(axis=-1, keepdims=True)
        hit = lane == ix
        total = total + jnp.where(hit, scores, 0.0).sum(axis=-1, keepdims=True)
        sel = jnp.where(hit, -jnp.inf, sel)
        chosen = jnp.where(hit, 1.0, chosen)
    before = _dot(tri_ref[...], chosen.astype(BF16))
    is_chosen = chosen > 0.0
    rank_ref[...] = jnp.where(is_chosen, before, -1.0).T
    gate_ref[...] = (jnp.where(is_chosen, scores, 0.0) / total * ROUTED_SCALE).T
    cnt_ref[...] = chosen.sum(axis=0, keepdims=True)


def _router(h, rw, rb, tm):
    t, d = h.shape
    nt = t // tm
    tri = jnp.asarray(np.tril(np.ones((tm, tm), np.float32), -1), BF16)
    col = pl.BlockSpec((LANES, tm), lambda i: (0, i))
    return pl.pallas_call(
        _router_kernel,
        out_shape=(jax.ShapeDtypeStruct((LANES, t), F32), jax.ShapeDtypeStruct((LANES, t), F32),
                   jax.ShapeDtypeStruct((nt, 1, LANES), F32)),
        grid=(nt,),
        in_specs=[pl.BlockSpec((tm, d), lambda i: (i, 0)),
                  pl.BlockSpec((d, LANES), lambda i: (0, 0)),
                  pl.BlockSpec((1, LANES), lambda i: (0, 0)),
                  pl.BlockSpec((tm, tm), lambda i: (0, 0))],
        out_specs=(col, col, pl.BlockSpec((None, 1, LANES), lambda i: (i, 0, 0))),
        compiler_params=_cparams(1), name="moe_router",
    )(h, rw, rb, tri)


DISPATCH_WIN = 128
GROUP_ALIGN = 16
N_PAIRS = N_EXPERTS // 2


def _selection(rank_row, gate_row, win_off):
    tm = rank_row.shape[1]
    i = lax.broadcasted_iota(jnp.int32, (DISPATCH_WIN, tm), 0).astype(F32) + win_off
    hit = i == jnp.broadcast_to(rank_row, (DISPATCH_WIN, tm))
    if gate_row is None:
        return jnp.where(hit, 1.0, 0.0).astype(BF16)
    return jnp.where(hit, jnp.broadcast_to(gate_row, (DISPATCH_WIN, tm)), 0.0).astype(BF16)


def _dispatch_kernel(base_ref, nchunk_ref, rend_ref, h_ref, rank_ref, xg_ref, obuf, sem):
    j = pl.program_id(0)
    w = DISPATCH_WIN

    def window_copy(slot, half, start):
        return pltpu.make_async_copy(obuf.at[slot, pl.ds(half * w, w)], xg_ref.at[pl.ds(start, w)], sem.at[slot, half])

    @pl.when(j == 0)
    def _zero_region_tails():
        obuf[0] = jnp.zeros(obuf.shape[1:], obuf.dtype)
        n_clear = (MOE_BLOCK + DISPATCH_WIN) // w
        for e in range(N_EXPERTS):
            for i in range(n_clear):
                start = jnp.maximum(rend_ref[e] - (i + 1) * w, 0)
                window_copy(0, i % 2, pl.multiple_of(start, GROUP_ALIGN)).start()
        for e in range(N_EXPERTS):
            for i in range(n_clear):
                window_copy(0, i % 2, 0).wait()

    h = h_ref[...]

    def pair_body(p, carry):
        slot = p % 2

        @pl.when(p >= 2)
        def _():
            window_copy(slot, 0, 0).wait()
            window_copy(slot, 1, 0).wait()

        sel = jnp.concatenate([_selection(rank_ref[pl.ds(2 * p + q, 1), :], None, 0.0) for q in range(2)], axis=0)
        obuf[slot] = _dot(sel, h).astype(obuf.dtype)
        for q in range(2):
            window_copy(slot, q, pl.multiple_of(base_ref[j * N_EXPERTS + 2 * p + q], GROUP_ALIGN)).start()
        return carry

    lax.fori_loop(0, N_PAIRS, pair_body, 0)
    for slot in range(2):
        for half in range(2):
            window_copy(slot, half, 0).wait()

    def overflow_body(e, carry):
        def chunk_body(c, carry2):
            sel = _selection(rank_ref[pl.ds(e, 1), :], None, (c * w).astype(F32))
            obuf[0, 0:w, :] = _dot(sel, h).astype(obuf.dtype)
            cp = window_copy(0, 0, pl.multiple_of(base_ref[j * N_EXPERTS + e] + c * w, GROUP_ALIGN))
            cp.start()
            cp.wait()
            return carry2

        lax.fori_loop(1, nchunk_ref[j * N_EXPERTS + e], chunk_body, 0)
        return carry

    lax.fori_loop(0, N_EXPERTS, overflow_body, 0)


def _dispatch(h, rank_t, base, nchunk, region_end, n_slots, tm):
    t, d = h.shape
    return pl.pallas_call(
        _dispatch_kernel,
        out_shape=jax.ShapeDtypeStruct((n_slots, d), BF16),
        grid_spec=pltpu.PrefetchScalarGridSpec(
            num_scalar_prefetch=3,
            grid=(t // tm,),
            in_specs=[pl.BlockSpec((tm, d), lambda i, *_: (i, 0)),
                      pl.BlockSpec((LANES, tm), lambda i, *_: (0, i))],
            out_specs=pl.BlockSpec(memory_space=pl.ANY),
            scratch_shapes=[pltpu.VMEM((2, 2 * DISPATCH_WIN, d), BF16), pltpu.SemaphoreType.DMA((2, 2))]),
        compiler_params=_cparams(1), name="moe_dispatch",
    )(base, nchunk, region_end, h, rank_t)


def _expert_kernel(be_ref, nb_ref, x_ref, wg_ref, wu_ref, wd_ref, o_ref, wg_s, wu_s, wd_s):
    i = pl.program_id(0)
    active = i < nb_ref[0]
    new_expert = (i == 0) | (be_ref[i] != be_ref[jnp.maximum(i - 1, 0)])

    @pl.when(active & new_expert)
    def _():
        wg_s[...] = wg_ref[...].astype(BF16)
        wu_s[...] = wu_ref[...].astype(BF16)
        wd_s[...] = wd_ref[...].astype(BF16)

    @pl.when(active)
    def _():
        x = x_ref[...]
        a = _dot(x, wg_s[...])
        a = a * jax.nn.sigmoid(a) * _dot(x, wu_s[...])
        o_ref[...] = _dot(a.astype(BF16), wd_s[...]).astype(o_ref.dtype)

    @pl.when(jnp.logical_not(active))
    def _():
        o_ref[...] = jnp.zeros(o_ref.shape, o_ref.dtype)


def _expert_ffn(xg, block_exp, n_used, wg, wu, wd, tb):
    n, d = xg.shape
    de = wg.shape[2]
    return pl.pallas_call(
        _expert_kernel,
        out_shape=jax.ShapeDtypeStruct((n, d), BF16),
        grid_spec=pltpu.PrefetchScalarGridSpec(
            num_scalar_prefetch=2,
            grid=(n // tb,),
            in_specs=[pl.BlockSpec((tb, d), lambda i, be, nb: (jnp.minimum(i, nb[0] - 1), 0)),
                      pl.BlockSpec((None, d, de), lambda i, be, nb: (be[i], 0, 0)),
                      pl.BlockSpec((None, d, de), lambda i, be, nb: (be[i], 0, 0)),
                      pl.BlockSpec((None, de, d), lambda i, be, nb: (be[i], 0, 0))],
            out_specs=pl.BlockSpec((tb, d), lambda i, be, nb: (i, 0)),
            scratch_shapes=[pltpu.VMEM((d, de), BF16), pltpu.VMEM((d, de), BF16), pltpu.VMEM((de, d), BF16)]),
        compiler_params=_cparams(1), name="expert_ffn",
    )(block_exp, n_used, xg, wg, wu, wd)


def _moe_combine_kernel(base_ref, nchunk_ref, x_ref, mgate_ref, s_ref, rank_ref, gate_ref, y_ref, o_ref,
                        ybuf, acc_ref, sem, *, tile_off):
    j = pl.program_id(0) + tile_off
    w = DISPATCH_WIN

    def window_copy(slot, half, start):
        return pltpu.make_async_copy(y_ref.at[pl.ds(start, w)], ybuf.at[slot, pl.ds(half * w, w)], sem.at[slot, half])

    def start_pair(p, slot):
        for q in range(2):
            window_copy(slot, q, pl.multiple_of(base_ref[j * N_EXPERTS + 2 * p + q], GROUP_ALIGN)).start()

    start_pair(0, 0)
    acc_ref[...] = s_ref[...].astype(F32)

    def pair_body(p, carry):
        slot = p % 2
        window_copy(slot, 0, 0).wait()
        window_copy(slot, 1, 0).wait()

        @pl.when(p + 1 < N_PAIRS)
        def _():
            start_pair(p + 1, 1 - slot)

        sel = jnp.concatenate([_selection(rank_ref[pl.ds(2 * p + q, 1), :], gate_ref[pl.ds(2 * p + q, 1), :], 0.0)
                               for q in range(2)], axis=0)
        acc_ref[...] += _dot_tn(sel, ybuf[slot])
        return carry

    lax.fori_loop(0, N_PAIRS, pair_body, 0)

    def overflow_body(e, carry):
        def chunk_body(c, carry2):
            cp = window_copy(0, 0, pl.multiple_of(base_ref[j * N_EXPERTS + e] + c * w, GROUP_ALIGN))
            cp.start()
            cp.wait()
            sel = _selection(rank_ref[pl.ds(e, 1), :], gate_ref[pl.ds(e, 1), :], (c * w).astype(F32))
            acc_ref[...] += _dot_tn(sel, ybuf[0, 0:w, :])
            return carry2

        lax.fori_loop(1, nchunk_ref[j * N_EXPERTS + e], chunk_body, 0)
        return carry

    lax.fori_loop(0, N_EXPERTS, overflow_body, 0)
    o_ref[...] = x_ref[...] + mgate_ref[...] * acc_ref[...]


def _moe_combine(x, mod3, mod_row, y, rank_t, gate_t, shared, base, nchunk, tile_off, tm):
    m, d = x.shape
    return pl.pallas_call(
        functools.partial(_moe_combine_kernel, tile_off=tile_off),
        out_shape=jax.ShapeDtypeStruct((m, d), F32),
        grid_spec=pltpu.PrefetchScalarGridSpec(
            num_scalar_prefetch=2,
            grid=(m // tm,),
            in_specs=[pl.BlockSpec((tm, d), lambda i, *_: (i, 0)),
                      pl.BlockSpec((None, 1, d), lambda i, *_: (mod_row(i) * 6 + 5, 0, 0)),
                      pl.BlockSpec((tm, d), lambda i, *_: (i + tile_off, 0)),
                      pl.BlockSpec((LANES, tm), lambda i, *_: (0, i + tile_off)),
                      pl.BlockSpec((LANES, tm), lambda i, *_: (0, i + tile_off)),
                      pl.BlockSpec(memory_space=pl.ANY)],
            out_specs=pl.BlockSpec((tm, d), lambda i, *_: (i, 0)),
            scratch_shapes=[pltpu.VMEM((2, 2 * DISPATCH_WIN, d), BF16), pltpu.VMEM((tm, d), F32),
                            pltpu.SemaphoreType.DMA((2, 2))]),
        compiler_params=_cparams(1), name="moe_combine",
    )(base, nchunk, x, mod3, shared, rank_t, gate_t, y)


ROUTER_TILE = 512


def _moe(h, rw, rb, wg, wu, wd, sg, su, sd, layer):
    t, d = h.shape
    nt = t // ROUTER_TILE
    rank_t, gate_t, cnt = _router(h, rw, rb, ROUTER_TILE)
    counts = cnt[:, 0, :N_EXPERTS].astype(jnp.int32)
    aligned = (counts + GROUP_ALIGN - 1) // GROUP_ALIGN * GROUP_ALIGN
    region = (aligned.sum(axis=0) + DISPATCH_WIN + MOE_BLOCK - 1) // MOE_BLOCK * MOE_BLOCK
    region_end = jnp.cumsum(region)
    base = ((region_end - region)[None, :] + jnp.cumsum(aligned, axis=0) - aligned).reshape(-1)
    nchunk = ((counts + DISPATCH_WIN - 1) // DISPATCH_WIN).reshape(-1)
    n_blocks = -(-(t * TOP_K + (GROUP_ALIGN - 1) * nt * N_EXPERTS + N_EXPERTS * (DISPATCH_WIN + MOE_BLOCK - 1))
                 // MOE_BLOCK)
    block_start = jnp.arange(n_blocks, dtype=jnp.int32) * MOE_BLOCK
    block_exp = jnp.minimum((region_end[None, :] <= block_start[:, None]).sum(axis=1), N_EXPERTS - 1)
    n_used = (region_end[N_EXPERTS - 1] // MOE_BLOCK).astype(jnp.int32).reshape(1)
    xg = _dispatch(h, rank_t, base, nchunk, region_end.astype(jnp.int32), n_blocks * MOE_BLOCK, ROUTER_TILE)
    y = _expert_ffn(xg, block_exp.astype(jnp.int32) + layer * N_EXPERTS, n_used, wg, wu, wd, MOE_BLOCK)
    shared = _expert_ffn(h, jnp.full((t // MOE_BLOCK,), layer, jnp.int32), jnp.full((1,), t // MOE_BLOCK, jnp.int32),
                         sg, su, sd, MOE_BLOCK)
    return y, shared, (rank_t, gate_t, base, nchunk)


def _rope_tables(seq_len, gain):
    nf = MLA_ROPE // 4
    inv = ROPE_THETA ** (-jnp.arange(nf, dtype=F32) / nf)
    pos = jnp.arange(seq_len, dtype=jnp.int32)
    ang_r = (pos // GRID_W).astype(F32)[:, None] * inv
    ang_c = (pos % GRID_W).astype(F32)[:, None] * inv
    cos = jnp.concatenate([jnp.cos(ang_r)] * 2 + [jnp.cos(ang_c)] * 2, axis=-1)
    sin = jnp.concatenate([-jnp.sin(ang_r), jnp.sin(ang_r), -jnp.sin(ang_c), jnp.sin(ang_c)], axis=-1)
    g = gain.astype(F32)
    cg = cos * g[None]
    sg = sin * g[_ROPE_PARTNER][None]
    return jnp.tile(cg, (1, 2)), jnp.tile(sg, (1, 2))


def _rope_partner():
    nf = MLA_ROPE // 4
    idx = np.arange(MLA_ROPE)
    within = idx % (2 * nf)
    return np.where(within < nf, idx + nf, idx - nf)


_ROPE_PARTNER = _rope_partner()


def _in_splits():
    sizes = (NA_W, NA_W, NA_W, MLA_Q_LORA, MLA_KV_LORA, MLA_ROPE, CONV_CH, CONV_CH)
    offs = np.concatenate([[0], np.cumsum(sizes)])
    return {n: int(o) for n, o in zip(("q", "k", "v", "cq", "ckv", "kr", "a", "b", "g"), offs)}


def _prep_layer(l, dmodel, w_in, na_q_g, na_k_g, na_rpb, w_na_o, mla_q_lora_g, mla_w_uq, mla_kv_lora_g, mla_w_ukv,
                mla_q_g, mla_k_g, w_mla_o, conv_w_dw, conv_b_dw, conv_ln_g, conv_ln_b, w_conv_o, w_out,
                router_w, router_b, exp_w_gate, exp_w_up, exp_w_down, sh_w_gate, sh_w_up, sh_w_down, seq_len):
    o = _in_splits()
    w = w_in[l]
    cols = lambda a, n: w[:, a:a + n]
    p = {}
    p["w_p1"] = jnp.concatenate([cols(o["q"], 3 * NA_W), cols(o["a"], 2 * CONV_CH), cols(o["g"], 3 * dmodel)],
                                axis=1).astype(BF16)
    kr = cols(o["kr"], MLA_ROPE)
    krs = kr[:, _ROPE_PARTNER]
    p["w_p2"] = jnp.concatenate([cols(o["cq"], MLA_Q_LORA), kr, kr, krs, krs, cols(o["ckv"], MLA_KV_LORA)],
                                axis=1).astype(BF16)
    qscale = NA_HEAD_DIM ** -0.5 * LOG2E
    p["na_g"] = jnp.concatenate([jnp.tile(na_q_g[l].astype(F32) * qscale, NA_HEADS),
                                 jnp.tile(na_k_g[l].astype(F32), NA_HEADS)]).reshape(1, 2 * NA_W)
    p["na_bias"] = _na_bias_table(na_rpb[l])
    wq = mla_w_uq[l].reshape(MLA_Q_LORA, MLA_HEADS, MLA_QK)
    wq_rope = wq[:, :, MLA_NOPE:]
    flat = lambda a: a.reshape(a.shape[0], -1)
    p["w_uq"] = jnp.concatenate([flat(wq[:, :, :MLA_NOPE]), flat(wq_rope), flat(wq_rope[:, :, _ROPE_PARTNER])],
                                axis=1).astype(BF16)
    wkv = mla_w_ukv[l].reshape(MLA_KV_LORA, MLA_HEADS, MLA_NOPE + MLA_V)
    p["w_ukv"] = jnp.concatenate([flat(wkv[:, :, :MLA_NOPE]), flat(wkv[:, :, MLA_NOPE:])], axis=1).astype(BF16)
    p["q_lora_g"] = mla_q_lora_g[l].astype(F32).reshape(1, -1)
    p["kv_lora_g"] = mla_kv_lora_g[l].astype(F32).reshape(1, -1)
    mscale = MLA_QK ** -0.5 * LOG2E
    p["q_gn"] = (mla_q_g[l][:MLA_NOPE].astype(F32) * mscale).reshape(1, -1)
    p["k_gn"] = mla_k_g[l][:MLA_NOPE].astype(F32).reshape(1, -1)
    cgq, sgq = _rope_tables(seq_len, mla_q_g[l][MLA_NOPE:] * mscale)
    cgk, sgk = _rope_tables(seq_len, mla_k_g[l][MLA_NOPE:])
    p["q_tabs"] = (cgq, sgq)
    p["k_tabs"] = (cgk, sgk)
    p["q_tabs_c"] = (jnp.tile((mla_q_g[l][MLA_NOPE:].astype(F32) * mscale)[None], (1, 2)), jnp.zeros((1, LANES), F32))
    p["k_tabs_c"] = (jnp.tile(mla_k_g[l][MLA_NOPE:].astype(F32)[None], (1, 2)), jnp.zeros((1, LANES), F32))
    p["w_na_o"] = w_na_o[l].astype(BF16)
    p["w_mla_o"] = w_mla_o[l].astype(BF16)
    p["w_conv_o"] = w_conv_o[l].astype(BF16)
    p["w_out"] = w_out[l].astype(BF16)
    p["conv_w"] = jnp.concatenate([conv_w_dw[l].astype(F32), jnp.zeros((1, CONV_CH), F32)], axis=0)
    p["conv_b"] = conv_b_dw[l].astype(F32).reshape(1, -1)
    p["ln_g"] = conv_ln_g[l].astype(F32).reshape(1, -1)
    p["ln_b"] = conv_ln_b[l].astype(F32).reshape(1, -1)
    p["router_w"] = jnp.concatenate([router_w[l], jnp.zeros((dmodel, LANES - N_EXPERTS), router_w.dtype)],
                                    axis=1).astype(BF16)
    p["router_b"] = jnp.concatenate([router_b[l].astype(F32), jnp.zeros((LANES - N_EXPERTS,), F32)]).reshape(1, LANES)
    flat_e = lambda a: a.reshape((-1,) + a.shape[2:])
    p["exp"] = (flat_e(exp_w_gate), flat_e(exp_w_up), flat_e(exp_w_down))
    p["sh"] = (sh_w_gate, sh_w_up, sh_w_down)
    return p


def _group_matrices():
    i = np.arange(LANES)
    bd = (i[:, None] // NA_HEAD_DIM == i[None, :] // NA_HEAD_DIM)
    ones = np.ones((LANES, LANES), bool)
    e0 = np.broadcast_to((i < MLA_ROPE)[:, None], (LANES, LANES))
    e1 = np.broadcast_to((i >= MLA_ROPE)[:, None], (LANES, LANES))
    f = lambda m: jnp.asarray(m, BF16)
    return f(bd), (f(ones), f(e0), f(e1))


def _expand_tabs(tabs, tm):
    return tuple(jnp.broadcast_to(t, (tm, LANES)) if t.shape[0] == 1 else t for t in tabs)


def _mixer_inputs(h, p, mats, bd, seq, ctx):
    tm_mm = min(1024, h.shape[0])
    p1 =_matmul(h, p["w_p1"], BF16, tm_mm, 1024, "in_proj_1")
    p2 = _matmul(h, p["w_p2"], BF16, tm_mm, P2_W // 2, "in_proj_2")
    qk = _headnorm(p1, p["na_g"], bd, min(512, h.shape[0]))
    tr = 256
    qt = _expand_tabs(p["q_tabs_c"], tr) if ctx else p["q_tabs"]
    kt = _expand_tabs(p["k_tabs_c"], tr) if ctx else p["k_tabs"]
    tseq = tr if ctx else seq
    km, vm = _mla_kv(p2, p["kv_lora_g"], p["w_ukv"], p["k_gn"], kt[0], kt[1], mats, tseq, tr)
    return p1, p2, qk, km, vm, (qt, tseq, tr)


def kernel(x, c, ctx, c_ctx, ada_w, ada_b, norm1_g, norm2_g, w_in, na_q_g, na_k_g, na_rpb, w_na_o, mla_q_lora_g, mla_w_uq, mla_kv_lora_g, mla_w_ukv, mla_q_g, mla_k_g, w_mla_o, conv_w_dw, conv_b_dw, conv_ln_g, conv_ln_b, w_conv_o, w_out, router_w, router_b, exp_w_gate, exp_w_up, exp_w_down, sh_w_gate, sh_w_up, sh_w_down):
    batch, seq, d = x.shape
    lc = ctx.shape[1]
    depth = ada_w.shape[0]
    mod_rows = (batch + 1 + 7) // 8 * 8
    cc = jnp.concatenate([c, c_ctx[None], jnp.zeros((mod_rows - batch - 1, d), c.dtype)], axis=0)
    mod_all = _ada(cc, ada_w, ada_b)
    bd, mats = _group_matrices()
    xs = x.reshape(batch * seq, d)
    zs = ctx.reshape(batch * lc, d)
    tm_x = 512
    tm_c = 256
    x_row = lambda i: (i * tm_x) // seq
    c_row = lambda i: batch

    for l in range(depth):
        need_ctx = l < depth - 1
        p = _prep_layer(l, d, w_in, na_q_g, na_k_g, na_rpb, w_na_o, mla_q_lora_g, mla_w_uq, mla_kv_lora_g, mla_w_ukv,
                        mla_q_g, mla_k_g, w_mla_o, conv_w_dw, conv_b_dw, conv_ln_g, conv_ln_b, w_conv_o, w_out,
                        router_w, router_b, exp_w_gate, exp_w_up, exp_w_down, sh_w_gate, sh_w_up, sh_w_down, seq)
        mod3 = mod_all[l].reshape(mod_rows * 6, 1, d)
        hx = _norm_mod(xs, norm1_g[l], mod3, x_row, 1, 0, tm_x)
        hc = _norm_mod(zs, norm1_g[l], mod3, c_row, 1, 0, tm_c)
        p1x, p2x, qkx, kmx, vmx, (qtx, tsx, trx) = _mixer_inputs(hx, p, mats, bd, seq, False)
        p1c, p2c, qkc, kmc, vmc, (qtc, tsc, trc) = _mixer_inputs(hc, p, mats, bd, lc, True)
        o_na = _na_attention(qkx, p1x, qkc, p1c, p["na_bias"], batch)
        qmx = _mla_q(p2x, p["q_lora_g"], p["w_uq"], p["q_gn"], qtx[0], qtx[1], mats, tsx, trx)
        o_mla = _mla_attention(qmx, [(kmc, vmc), (kmx, vmx)], batch, 512)
        u = _conv_module(p1x, p["conv_w"], p["conv_b"], p["ln_g"], p["ln_b"], seq, 256)
        mrg = _merge(o_na, o_mla, u, p1x, p["w_na_o"], p["w_mla_o"], p["w_conv_o"], 1024, 512)
        xs, hx2 = _out_proj(mrg, p["w_out"], xs, mod3, x_row, norm2_g[l], tm_x)
        if need_ctx:
            o_na_c = _na_ctx_attention(qkc, p1c, batch)
            qmc = _mla_q(p2c, p["q_lora_g"], p["w_uq"], p["q_gn"], qtc[0], qtc[1], mats, tsc, trc)
            o_mla_c = _mla_attention(qmc, [(kmc, vmc)], batch, lc)
            u_c = _conv_module(p1c, p["conv_w"], p["conv_b"], p["ln_g"], p["ln_b"], lc, lc)
            mrg_c = _merge(o_na_c, o_mla_c, u_c, p1c, p["w_na_o"], p["w_mla_o"], p["w_conv_o"], 1024, 512)
            zs, hc2 = _out_proj(mrg_c, p["w_out"], zs, mod3, c_row, norm2_g[l], tm_c)
            tokens = jnp.concatenate([hc2, hx2], axis=0)
        else:
            tokens = hx2
        y, shared, (rank_t, gate_t, base, nchunk) = _moe(tokens, p["router_w"], p["router_b"], *p["exp"], *p["sh"], l)
        tm_cmb = ROUTER_TILE
        x_row_cmb = lambda i: (i * tm_cmb) // seq
        if need_ctx:
            zs = _moe_combine(zs, mod3, c_row, y, rank_t, gate_t, shared, base, nchunk, 0, tm_cmb)
            xs = _moe_combine(xs, mod3, x_row_cmb, y, rank_t, gate_t, shared, base, nchunk, batch * lc // tm_cmb,
                              tm_cmb)
        else:
            xs = _moe_combine(xs, mod3, x_row_cmb, y, rank_t, gate_t, shared, base, nchunk, 0, tm_cmb)
    return xs.reshape(batch, seq, d)
```

```python
import functools

import numpy as np
import jax
import jax.numpy as jnp
from jax import lax
from jax.experimental import pallas as pl
from jax.experimental.pallas import tpu as pltpu

F32 = jnp.float32
BF16 = jnp.bfloat16

GRID_W = 64
EPS = 1e-6
NEG_INF = -1e30
NA_HEADS = 16
NA_HEAD_DIM = 64
NA_W = NA_HEADS * NA_HEAD_DIM
NA_WIN_ROWS = 8
NA_WIN_COLS = 16
MLA_HEADS = 16
MLA_NOPE = 128
MLA_ROPE = 64
MLA_QK = MLA_NOPE + MLA_ROPE
MLA_V = 128
MLA_Q_LORA = 768
MLA_KV_LORA = 512
ROPE_THETA = 10000.0
CONV_CH = 1024
CONV_WIDTH = 31
CONV_HALO = 16
N_EXPERTS = 64
TOP_K = 8
D_EXPERT = 512
ROUTED_SCALE = 2.5
MOE_BLOCK = 256
LANES = 128
MLA_HEAD_PAD = 2 * LANES
VMEM_LIMIT = 56 * 1024 * 1024

P1_Q, P1_K, P1_V, P1_A, P1_B, P1_GNA, P1_GMLA, P1_GCV = 0, 1024, 2048, 3072, 4096, 5120, 7168, 9216
P1_W = 11264
P2_CQ, P2_KR, P2_KRS, P2_CKV = 0, 768, 896, 1024
P2_W = 1536


def _cparams(n_axes):
    return pltpu.CompilerParams(dimension_semantics=("arbitrary",) * n_axes, vmem_limit_bytes=VMEM_LIMIT)


def _dot(a, b):
    return jnp.dot(a, b, preferred_element_type=F32)


def _dot_nt(a, b):
    return lax.dot_general(a, b, (((1,), (1,)), ((), ())), preferred_element_type=F32)


def _dot_hilo(x, m):
    hi = x.astype(BF16)
    lo = (x - hi.astype(F32)).astype(BF16)
    return _dot(hi, m) + _dot(lo, m)


def _ada_kernel(c_ref, w_ref, b_ref, o_ref):
    c = c_ref[...]
    a = (c * jax.nn.sigmoid(c)).astype(BF16)
    o_ref[...] = _dot(a, w_ref[...].astype(BF16)) + b_ref[...]


def _ada(cc, ada_w, ada_b):
    nl, d, n = ada_w.shape
    r = cc.shape[0]
    tn = 1024
    return pl.pallas_call(
        _ada_kernel,
        out_shape=jax.ShapeDtypeStruct((nl, r, n), F32),
        grid=(nl, n // tn),
        in_specs=[pl.BlockSpec((r, d), lambda l, j: (0, 0)),
                  pl.BlockSpec((None, d, tn), lambda l, j: (l, 0, j)),
                  pl.BlockSpec((None, 1, tn), lambda l, j: (l, 0, j))],
        out_specs=pl.BlockSpec((None, r, tn), lambda l, j: (l, 0, j)),
        compiler_params=_cparams(2), name="ada_mod",
    )(cc, ada_w, ada_b.reshape(nl, 1, n))


def _norm_mod_kernel(x_ref, g_ref, sc_ref, sh_ref, o_ref):
    x = x_ref[...]
    y = x * lax.rsqrt(jnp.mean(x * x, axis=-1, keepdims=True) + EPS) * g_ref[...]
    o_ref[...] = (y * (1.0 + sc_ref[...]) + sh_ref[...]).astype(o_ref.dtype)


def _norm_mod(x, g, mod3, mod_row, k_sc, k_sh, tm):
    m, d = x.shape
    return pl.pallas_call(
        _norm_mod_kernel,
        out_shape=jax.ShapeDtypeStruct((m, d), BF16),
        grid=(m // tm,),
        in_specs=[pl.BlockSpec((tm, d), lambda i: (i, 0)),
                  pl.BlockSpec((1, d), lambda i: (0, 0)),
                  pl.BlockSpec((None, 1, d), lambda i: (mod_row(i) * 6 + k_sc, 0, 0)),
                  pl.BlockSpec((None, 1, d), lambda i: (mod_row(i) * 6 + k_sh, 0, 0))],
        out_specs=pl.BlockSpec((tm, d), lambda i: (i, 0)),
        compiler_params=_cparams(1), name="norm_mod",
    )(x, g.reshape(1, d), mod3, mod3)


def _mm_kernel(a_ref, w_ref, o_ref):
    o_ref[...] = _dot(a_ref[...], w_ref[...]).astype(o_ref.dtype)


def _matmul(a, w, out_dtype, tm, tn, name):
    m, k = a.shape
    n = w.shape[1]
    return pl.pallas_call(
        _mm_kernel,
        out_shape=jax.ShapeDtypeStruct((m, n), out_dtype),
        grid=(m // tm, n // tn),
        in_specs=[pl.BlockSpec((tm, k), lambda i, j: (i, 0)),
                  pl.BlockSpec((k, tn), lambda i, j: (0, j))],
        out_specs=pl.BlockSpec((tm, tn), lambda i, j: (i, j)),
        compiler_params=_cparams(2), name=name,
    )(a, w)


def _headnorm_kernel(x_ref, g_ref, bd_ref, o_ref):
    bd = bd_ref[...]
    for c in range(x_ref.shape[1] // LANES):
        sl = slice(c * LANES, (c + 1) * LANES)
        x = x_ref[:, sl].astype(F32)
        ss = _dot_hilo(x * x, bd)
        y = x * lax.rsqrt(ss * (1.0 / NA_HEAD_DIM) + EPS) * g_ref[:, sl]
        o_ref[:, sl] = y.astype(o_ref.dtype)


def _headnorm(p1, g_row, bd, tm):
    m = p1.shape[0]
    w = g_row.shape[1]
    return pl.pallas_call(
        _headnorm_kernel,
        out_shape=jax.ShapeDtypeStruct((m, w), BF16),
        grid=(m // tm,),
        in_specs=[pl.BlockSpec((tm, w), lambda i: (i, 0)),
                  pl.BlockSpec((1, w), lambda i: (0, 0)),
                  pl.BlockSpec((LANES, LANES), lambda i: (0, 0))],
        out_specs=pl.BlockSpec((tm, w), lambda i: (i, 0)),
        compiler_params=_cparams(1), name="na_headnorm",
    )(p1, g_row, bd)


LOG2E = 1.4426950408889634


def _softmax_pv(scores, values):
    m = scores[0].max(axis=-1, keepdims=True)
    for s in scores[1:]:
        m = jnp.maximum(m, s.max(axis=-1, keepdims=True))
    l = None
    o = None
    for s, v in zip(scores, values):
        p = jnp.exp2(s - m)
        ps = p.sum(axis=-1, keepdims=True)
        po = _dot(p.astype(BF16), v)
        l = ps if l is None else l + ps
        o = po if o is None else o + po
    return o / l


def _dot_tn(a, b):
    return lax.dot_general(a, b, (((0,), (0,)), ((), ())), preferred_element_type=F32)


def _na_kernel(q_ref, kx_ref, vx_ref, kc_ref, vc_ref, bias_ref, o_ref):
    rows = kx_ref.shape[0] // GRID_W
    r = pl.program_id(1)
    r0 = jnp.clip(r - NA_WIN_ROWS // 2, 0, rows - NA_WIN_ROWS)
    kstart = pl.multiple_of(r0 * GRID_W, GRID_W)
    nwin = NA_WIN_ROWS * GRID_W
    lo = lax.broadcasted_iota(jnp.int32, (GRID_W, LANES), 1) < NA_HEAD_DIM
    zero = jnp.zeros((GRID_W, LANES), q_ref.dtype)
    for p in range(NA_W // LANES):
        sl = slice(p * LANES, (p + 1) * LANES)
        q2 = q_ref[:, sl]
        qs = jnp.concatenate([jnp.where(lo, q2, zero), jnp.where(lo, zero, q2)], axis=0)
        kw = kx_ref[pl.ds(kstart, nwin), sl]
        vw = vx_ref[pl.ds(kstart, nwin), sl]
        s_w = _dot_nt(kw, qs) + bias_ref[p]
        s_c = _dot_nt(kc_ref[:, sl], qs)
        m = jnp.maximum(s_w.max(axis=0, keepdims=True), s_c.max(axis=0, keepdims=True))
        p_w = jnp.exp2(s_w - m)
        p_c = jnp.exp2(s_c - m)
        inv = 1.0 / (p_w.sum(axis=0, keepdims=True) + p_c.sum(axis=0, keepdims=True))
        o2 = _dot_tn((p_w * inv).astype(BF16), vw) + _dot_tn((p_c * inv).astype(BF16), vc_ref[:, sl])
        o_ref[:, sl] = jnp.where(lo, o2[:GRID_W], o2[GRID_W:]).astype(o_ref.dtype)


def _na_attention(qk_x, p1_x, qk_c, p1_c, bias_tab, batch):
    l = qk_x.shape[0] // batch
    lc = qk_c.shape[0] // batch
    rows = l // GRID_W
    nwin = NA_WIN_ROWS * GRID_W

    def bias_idx(b, r):
        r0 = jnp.clip(r - NA_WIN_ROWS // 2, 0, rows - NA_WIN_ROWS)
        return (r - r0, 0, 0, 0)

    return pl.pallas_call(
        _na_kernel,
        out_shape=jax.ShapeDtypeStruct((batch * l, NA_W), BF16),
        grid=(batch, rows),
        in_specs=[pl.BlockSpec((GRID_W, NA_W), lambda b, r: (b * rows + r, 0)),
                  pl.BlockSpec((l, NA_W), lambda b, r: (b, 1)),
                  pl.BlockSpec((l, NA_W), lambda b, r: (b, P1_V // NA_W)),
                  pl.BlockSpec((lc, NA_W), lambda b, r: (b, 1)),
                  pl.BlockSpec((lc, NA_W), lambda b, r: (b, P1_V // NA_W)),
                  pl.BlockSpec((None, NA_HEADS // 2, nwin, LANES), bias_idx)],
        out_specs=pl.BlockSpec((GRID_W, NA_W), lambda b, r: (b * rows + r, 0)),
        compiler_params=_cparams(2), name="na_attention",
    )(qk_x, qk_x, p1_x, qk_c, p1_c, bias_tab)


def _na_ctx_kernel(q_ref, k_ref, v_ref, o_ref):
    n = q_ref.shape[0]
    lo = lax.broadcasted_iota(jnp.int32, (n, LANES), 1) < NA_HEAD_DIM
    zero = jnp.zeros((n, LANES), q_ref.dtype)
    for p in range(NA_W // LANES):
        sl = slice(p * LANES, (p + 1) * LANES)
        q2 = q_ref[:, sl]
        k = k_ref[:, sl]
        v = v_ref[:, sl]
        outs = []
        for hh in range(2):
            qm = jnp.where(lo, q2, zero) if hh == 0 else jnp.where(lo, zero, q2)
            outs.append(_softmax_pv([_dot_nt(qm, k)], [v]))
        o_ref[:, sl] = jnp.where(lo, outs[0], outs[1]).astype(o_ref.dtype)


def _na_ctx_attention(qk_c, p1_c, batch):
    lc = qk_c.shape[0] // batch
    return pl.pallas_call(
        _na_ctx_kernel,
        out_shape=jax.ShapeDtypeStruct((batch * lc, NA_W), BF16),
        grid=(batch,),
        in_specs=[pl.BlockSpec((lc, NA_W), lambda b: (b, 0)),
                  pl.BlockSpec((lc, NA_W), lambda b: (b, 1)),
                  pl.BlockSpec((lc, NA_W), lambda b: (b, P1_V // NA_W))],
        out_specs=pl.BlockSpec((lc, NA_W), lambda b: (b, 0)),
        compiler_params=_cparams(1), name="na_ctx_attention",
    )(qk_c, qk_c, p1_c)


def _na_bias_table(rpb):
    cls = np.arange(NA_WIN_ROWS)[:, None]
    w = np.arange(NA_WIN_ROWS)[None, :]
    qc = np.arange(GRID_W)[:, None]
    kc = np.arange(GRID_W)[None, :]
    c0 = np.clip(qc - NA_WIN_COLS // 2, 0, GRID_W - NA_WIN_COLS)
    ok = (kc >= c0) & (kc < c0 + NA_WIN_COLS)
    dr = w - cls + NA_WIN_ROWS - 1
    dc = np.clip(kc - qc + NA_WIN_COLS - 1, 0, 2 * NA_WIN_COLS - 2)
    row_sel = jnp.asarray(dr[:, :, None] == np.arange(2 * NA_WIN_ROWS - 1), F32)
    col_sel = jnp.asarray(dc[:, :, None] == np.arange(2 * NA_WIN_COLS - 1), F32)
    exact = lax.Precision.HIGHEST
    t1 = jnp.einsum('hab,cwa->hcwb', rpb.astype(F32) * LOG2E, row_sel, precision=exact)
    tab = jnp.einsum('hcwb,qkb->chwkq', t1, col_sel, precision=exact)
    tab = jnp.where(jnp.asarray(ok.T)[None, None, None], tab, NEG_INF)
    tab = tab.reshape(NA_WIN_ROWS, NA_HEADS // 2, 2, NA_WIN_ROWS * GRID_W, GRID_W)
    tab = jnp.transpose(tab, (0, 1, 3, 2, 4))
    return tab.reshape(NA_WIN_ROWS, NA_HEADS // 2, NA_WIN_ROWS * GRID_W, LANES)


def _mla_q_kernel(p2_ref, gl_ref, w_ref, gn_ref, cg_ref, sg_ref, ones_ref, e0_ref, e1_ref, o_ref):
    cq = p2_ref[...].astype(F32)
    cqn = cq * lax.rsqrt(jnp.mean(cq * cq, axis=-1, keepdims=True) + EPS) * gl_ref[...]
    y = _dot(cqn.astype(BF16), w_ref[...])
    tm = y.shape[0]
    lo = lax.broadcasted_iota(jnp.int32, (tm, LANES), 1) < MLA_ROPE
    nope_w = MLA_HEADS * MLA_NOPE
    rope_w = MLA_HEADS * MLA_ROPE
    ones, e0, e1 = ones_ref[...], e0_ref[...], e1_ref[...]
    gn = gn_ref[...]
    for p in range(MLA_HEADS // 2):
        yr = y[:, nope_w + p * LANES: nope_w + (p + 1) * LANES]
        yrs = y[:, nope_w + rope_w + p * LANES: nope_w + rope_w + (p + 1) * LANES]
        rot = yr * cg_ref[...] + yrs * sg_ref[...]
        yr2 = yr * yr
        ssr = (_dot_hilo(yr2, e0), _dot_hilo(yr2, e1))
        for hh in range(2):
            h = 2 * p + hh
            nope = y[:, h * MLA_NOPE:(h + 1) * MLA_NOPE]
            tot = _dot_hilo(nope * nope, ones) + ssr[hh]
            scale = lax.rsqrt(tot * (1.0 / MLA_QK) + EPS)
            o_ref[:, h * MLA_HEAD_PAD: h * MLA_HEAD_PAD + LANES] = (nope * scale * gn).astype(o_ref.dtype)
            rs = rot * scale
            rs = jnp.where(lo, rs, 0.0) if hh == 0 else jnp.where(lo, 0.0, rs)
            o_ref[:, h * MLA_HEAD_PAD + LANES: (h + 1) * MLA_HEAD_PAD] = rs.astype(o_ref.dtype)


def _mla_q(p2, g_lora, w_uq, g_nope, cg, sg, mats, seq, tm):
    m = p2.shape[0]
    nt = seq // tm
    ones, e0, e1 = mats
    full = lambda shape: pl.BlockSpec(shape, lambda i: (0,) * len(shape))
    return pl.pallas_call(
        _mla_q_kernel,
        out_shape=jax.ShapeDtypeStruct((m, MLA_HEADS * MLA_HEAD_PAD), BF16),
        grid=(m // tm,),
        in_specs=[pl.BlockSpec((tm, MLA_Q_LORA), lambda i: (i, 0)),
                  full((1, MLA_Q_LORA)),
                  full(w_uq.shape),
                  full((1, LANES)),
                  pl.BlockSpec((tm, LANES), lambda i: (i % nt, 0)),
                  pl.BlockSpec((tm, LANES), lambda i: (i % nt, 0)),
                  full((LANES, LANES)), full((LANES, LANES)), full((LANES, LANES))],
        out_specs=pl.BlockSpec((tm, MLA_HEADS * MLA_HEAD_PAD), lambda i: (i, 0)),
        compiler_params=_cparams(1), name="mla_q_proj",
    )(p2, g_lora, w_uq, g_nope, cg, sg, ones, e0, e1)


def _mla_kv_kernel(ckv_ref, kr_ref, krs_ref, gl_ref, w_ref, gn_ref, cg_ref, sg_ref, ones_ref, e0_ref, k_ref, v_ref):
    ckv = ckv_ref[...].astype(F32)
    cn = ckv * lax.rsqrt(jnp.mean(ckv * ckv, axis=-1, keepdims=True) + EPS) * gl_ref[...]
    y = _dot(cn.astype(BF16), w_ref[...])
    tm = y.shape[0]
    lo = lax.broadcasted_iota(jnp.int32, (tm, LANES), 1) < MLA_ROPE
    nope_w = MLA_HEADS * MLA_NOPE
    v_ref[...] = y[:, nope_w:].astype(v_ref.dtype)
    kr = kr_ref[...].astype(F32)
    rot = kr * cg_ref[...] + krs_ref[...].astype(F32) * sg_ref[...]
    ssr = _dot_hilo(kr * kr, e0_ref[...])
    ones = ones_ref[...]
    gn = gn_ref[...]
    for h in range(MLA_HEADS):
        nope = y[:, h * MLA_NOPE:(h + 1) * MLA_NOPE]
        tot = _dot_hilo(nope * nope, ones) + ssr
        scale = lax.rsqrt(tot * (1.0 / MLA_QK) + EPS)
        k_ref[:, h * MLA_HEAD_PAD: h * MLA_HEAD_PAD + LANES] = (nope * scale * gn).astype(k_ref.dtype)
        rs = rot * scale
        rs = jnp.where(lo, rs, 0.0) if h % 2 == 0 else jnp.where(lo, 0.0, rs)
        k_ref[:, h * MLA_HEAD_PAD + LANES: (h + 1) * MLA_HEAD_PAD] = rs.astype(k_ref.dtype)


def _mla_kv(p2, g_lora, w_ukv, g_nope, cg, sg, mats, seq, tm):
    m = p2.shape[0]
    nt = seq // tm
    ones, e0, _ = mats
    full = lambda shape: pl.BlockSpec(shape, lambda i: (0,) * len(shape))
    return pl.pallas_call(
        _mla_kv_kernel,
        out_shape=(jax.ShapeDtypeStruct((m, MLA_HEADS * MLA_HEAD_PAD), BF16),
                   jax.ShapeDtypeStruct((m, MLA_HEADS * MLA_V), BF16)),
        grid=(m // tm,),
        in_specs=[pl.BlockSpec((tm, MLA_KV_LORA), lambda i: (i, P2_CKV // MLA_KV_LORA)),
                  pl.BlockSpec((tm, LANES), lambda i: (i, P2_KR // LANES)),
                  pl.BlockSpec((tm, LANES), lambda i: (i, P2_KRS // LANES)),
                  full((1, MLA_KV_LORA)),
                  full(w_ukv.shape),
                  full((1, LANES)),
                  pl.BlockSpec((tm, LANES), lambda i: (i % nt, 0)),
                  pl.BlockSpec((tm, LANES), lambda i: (i % nt, 0)),
                  full((LANES, LANES)), full((LANES, LANES))],
        out_specs=(pl.BlockSpec((tm, MLA_HEADS * MLA_HEAD_PAD), lambda i: (i, 0)),
                   pl.BlockSpec((tm, MLA_HEADS * MLA_V), lambda i: (i, 0))),
        compiler_params=_cparams(1), name="mla_kv_proj",
    )(p2, p2, p2, g_lora, w_ukv, g_nope, cg, sg, ones, e0)


def _mla_attn_kernel(*refs, n_kv):
    q_ref = refs[0]
    kv = refs[1:1 + 2 * n_kv]
    o_ref = refs[1 + 2 * n_kv]
    for hh in range(MLA_HEADS_PER_STEP):
        qsl = slice(hh * MLA_HEAD_PAD, (hh + 1) * MLA_HEAD_PAD)
        vsl = slice(hh * MLA_V, (hh + 1) * MLA_V)
        q = q_ref[:, qsl]
        m = l = acc = None
        for i in range(n_kv):
            k_ref, v_ref = kv[2 * i], kv[2 * i + 1]
            n_keys = k_ref.shape[0]
            step = min(MLA_KEY_CHUNK, n_keys)
            for c0 in range(0, n_keys, step):
                s = _dot_nt(q, k_ref[c0:c0 + step, qsl])
                mc = s.max(axis=-1, keepdims=True)
                m_new = mc if m is None else jnp.maximum(m, mc)
                p = jnp.exp2(s - m_new)
                ps = p.sum(axis=-1, keepdims=True)
                pv = _dot(p.astype(BF16), v_ref[c0:c0 + step, vsl])
                if m is None:
                    l, acc = ps, pv
                else:
                    a = jnp.exp2(m - m_new)
                    l, acc = l * a + ps, acc * a + pv
                m = m_new
        o_ref[:, vsl] = (acc / l).astype(o_ref.dtype)


MLA_HEADS_PER_STEP = 2
MLA_KEY_CHUNK = 512


def _mla_attention(q, kvs, batch, tq):
    lq = q.shape[0] // batch
    nq = lq // tq
    hs = MLA_HEADS_PER_STEP
    in_specs = [pl.BlockSpec((tq, hs * MLA_HEAD_PAD), lambda b, h, j: (b * nq + j, h))]
    args = [q]
    for k, v in kvs:
        s = k.shape[0] // batch
        in_specs.append(pl.BlockSpec((s, hs * MLA_HEAD_PAD), lambda b, h, j: (b, h)))
        in_specs.append(pl.BlockSpec((s, hs * MLA_V), lambda b, h, j: (b, h)))
        args += [k, v]
    return pl.pallas_call(
        functools.partial(_mla_attn_kernel, n_kv=len(kvs)),
        out_shape=jax.ShapeDtypeStruct((batch * lq, MLA_HEADS * MLA_V), BF16),
        grid=(batch, MLA_HEADS // hs, nq),
        in_specs=in_specs,
        out_specs=pl.BlockSpec((tq, hs * MLA_V), lambda b, h, j: (b * nq + j, h)),
        compiler_params=_cparams(3), name="mla_attention",
    )(*args)


def _conv_kernel(ap_ref, a_ref, an_ref, bp_ref, b_ref, bn_ref, w_ref, cb_ref, g_ref, beta_ref, o_ref, u_scr, y_scr,
                 *, n_tiles):
    j = pl.program_id(1)
    tt = a_ref.shape[0]
    glu = lambda a, b: a[...].astype(F32) * jax.nn.sigmoid(b[...].astype(F32))
    u_scr[0:CONV_HALO, :] = jnp.where(j > 0, glu(ap_ref, bp_ref), 0.0)
    u_scr[CONV_HALO:CONV_HALO + tt, :] = glu(a_ref, b_ref)
    u_scr[CONV_HALO + tt:, :] = jnp.where(j < n_tiles - 1, glu(an_ref, bn_ref), 0.0)
    off = CONV_HALO - CONV_WIDTH // 2
    for c in range(CONV_CH // LANES):
        sl = slice(c * LANES, (c + 1) * LANES)
        acc = jnp.zeros((tt, LANES), F32)
        for k in range(CONV_WIDTH):
            acc = acc + u_scr[off + k: off + k + tt, sl] * w_ref[k:k + 1, sl]
        y_scr[:, sl] = acc + cb_ref[:, sl]
    y = y_scr[...]
    mu = jnp.mean(y, axis=-1, keepdims=True)
    d = y - mu
    var = jnp.mean(d * d, axis=-1, keepdims=True)
    z = d * lax.rsqrt(var + EPS) * g_ref[...] + beta_ref[...]
    o_ref[...] = (z * jax.nn.sigmoid(z)).astype(o_ref.dtype)


def _conv_module(p1, w_dw, b_dw, ln_g, ln_b, seq, tt):
    m = p1.shape[0]
    batch = m // seq
    n_tiles = seq // tt
    hb = tt // CONV_HALO
    n_hblk = m // CONV_HALO
    ca, cb = P1_A // CONV_CH, P1_B // CONV_CH

    def prev_idx(c):
        return lambda b, j: (jnp.maximum((b * n_tiles + j) * hb - 1, 0), c)

    def next_idx(c):
        return lambda b, j: (jnp.minimum((b * n_tiles + j + 1) * hb, n_hblk - 1), c)

    cur = lambda c: (lambda b, j: (b * n_tiles + j, c))
    full = lambda shape: pl.BlockSpec(shape, lambda b, j: (0,) * len(shape))
    return pl.pallas_call(
        functools.partial(_conv_kernel, n_tiles=n_tiles),
        out_shape=jax.ShapeDtypeStruct((m, CONV_CH), BF16),
        grid=(batch, n_tiles),
        in_specs=[pl.BlockSpec((CONV_HALO, CONV_CH), prev_idx(ca)),
                  pl.BlockSpec((tt, CONV_CH), cur(ca)),
                  pl.BlockSpec((CONV_HALO, CONV_CH), next_idx(ca)),
                  pl.BlockSpec((CONV_HALO, CONV_CH), prev_idx(cb)),
                  pl.BlockSpec((tt, CONV_CH), cur(cb)),
                  pl.BlockSpec((CONV_HALO, CONV_CH), next_idx(cb)),
                  full((32, CONV_CH)), full((1, CONV_CH)), full((1, CONV_CH)), full((1, CONV_CH))],
        out_specs=pl.BlockSpec((tt, CONV_CH), lambda b, j: (b * n_tiles + j, 0)),
        scratch_shapes=[pltpu.VMEM((tt + 2 * CONV_HALO, CONV_CH), F32), pltpu.VMEM((tt, CONV_CH), F32)],
        compiler_params=_cparams(2), name="conv_module",
    )(p1, p1, p1, p1, p1, p1, w_dw, b_dw, ln_g, ln_b)


def _merge_kernel(ona_ref, omla_ref, u_ref, gna_ref, gmla_ref, gcv_ref, wna_ref, wmla_ref, wcv_ref, o_ref):
    sig = lambda r: jax.nn.sigmoid(r[...].astype(F32))
    m = sig(gna_ref) * _dot(ona_ref[...], wna_ref[...])
    m = m + sig(gmla_ref) * _dot(omla_ref[...], wmla_ref[...])
    m = m + sig(gcv_ref) * _dot(u_ref[...], wcv_ref[...])
    o_ref[...] = m.astype(o_ref.dtype)


def _merge(o_na, o_mla, u, p1, w_na_o, w_mla_o, w_conv_o, tm, tn):
    m = o_na.shape[0]
    d = w_na_o.shape[1]
    tm = min(tm, m)
    row = lambda k: pl.BlockSpec((tm, k), lambda i, j: (i, 0))
    gate = lambda off: pl.BlockSpec((tm, tn), lambda i, j: (i, off // tn + j))
    wcol = lambda k: pl.BlockSpec((k, tn), lambda i, j: (0, j))
    return pl.pallas_call(
        _merge_kernel,
        out_shape=jax.ShapeDtypeStruct((m, d), BF16),
        grid=(m // tm, d // tn),
        in_specs=[row(o_na.shape[1]), row(o_mla.shape[1]), row(u.shape[1]),
                  gate(P1_GNA), gate(P1_GMLA), gate(P1_GCV),
                  wcol(w_na_o.shape[0]), wcol(w_mla_o.shape[0]), wcol(w_conv_o.shape[0])],
        out_specs=pl.BlockSpec((tm, tn), lambda i, j: (i, j)),
        compiler_params=_cparams(2), name="gated_merge",
    )(o_na, o_mla, u, p1, p1, p1, w_na_o, w_mla_o, w_conv_o)


def _out_proj_kernel(m_ref, w_ref, x_ref, gate_ref, g2_ref, sc_ref, sh_ref, xo_ref, ho_ref):
    x = x_ref[...] + gate_ref[...] * _dot(m_ref[...], w_ref[...])
    xo_ref[...] = x
    y = x * lax.rsqrt(jnp.mean(x * x, axis=-1, keepdims=True) + EPS) * g2_ref[...]
    ho_ref[...] = (y * (1.0 + sc_ref[...]) + sh_ref[...]).astype(ho_ref.dtype)


def _out_proj(mrg, w_out, x, mod3, mod_row, g2, tm):
    m, d = x.shape
    modspec = lambda k: pl.BlockSpec((None, 1, d), lambda i: (mod_row(i) * 6 + k, 0, 0))
    return pl.pallas_call(
        _out_proj_kernel,
        out_shape=(jax.ShapeDtypeStruct((m, d), F32), jax.ShapeDtypeStruct((m, d), BF16)),
        grid=(m // tm,),
        in_specs=[pl.BlockSpec((tm, d), lambda i: (i, 0)),
                  pl.BlockSpec((d, d), lambda i: (0, 0)),
                  pl.BlockSpec((tm, d), lambda i: (i, 0)),
                  modspec(2),
                  pl.BlockSpec((1, d), lambda i: (0, 0)),
                  modspec(4), modspec(3)],
        out_specs=(pl.BlockSpec((tm, d), lambda i: (i, 0)), pl.BlockSpec((tm, d), lambda i: (i, 0))),
        compiler_params=_cparams(1), name="out_proj_residual",
    )(mrg, w_out, x, mod3, g2.reshape(1, d), mod3, mod3)


def _router_kernel(h_ref, w_ref, b_ref, tri_ref, rank_ref, gate_ref, cnt_ref):
    scores = jax.nn.sigmoid(_dot(h_ref[...], w_ref[...]))
    tm = scores.shape[0]
    lane = lax.broadcasted_iota(jnp.int32, (tm, LANES), 1).astype(F32)
    sel = jnp.where(lane < N_EXPERTS, scores + b_ref[...], -jnp.inf)
    total = jnp.zeros((tm, 1), F32)
    chosen = jnp.zeros((tm, LANES), F32)
    for _ in range(TOP_K):
        mx = sel.max(axis=-1, keepdims=True)
        ix = jnp.where(sel == mx, lane, float(LANES)).min(axis=-1, keepdims=True)
        hit = lane == ix
        total = total + jnp.where(hit, scores, 0.0).sum(axis=-1, keepdims=True)
        sel = jnp.where(hit, -jnp.inf, sel)
        chosen = jnp.where(hit, 1.0, chosen)
    before = _dot(tri_ref[...], chosen.astype(BF16))
    is_chosen = chosen > 0.0
    rank_ref[...] = jnp.where(is_chosen, before, -1.0).T
    gate_ref[...] = (jnp.where(is_chosen, scores, 0.0) / total * ROUTED_SCALE).T
    cnt_ref[...] = chosen.sum(axis=0, keepdims=True)


def _router(h, rw, rb, tm):
    t, d = h.shape
    nt = t // tm
    tri = jnp.asarray(np.tril(np.ones((tm, tm), np.float32), -1), BF16)
    col = pl.BlockSpec((LANES, tm), lambda i: (0, i))
    return pl.pallas_call(
        _router_kernel,
        out_shape=(jax.ShapeDtypeStruct((LANES, t), F32), jax.ShapeDtypeStruct((LANES, t), F32),
                   jax.ShapeDtypeStruct((nt, 1, LANES), F32)),
        grid=(nt,),
        in_specs=[pl.BlockSpec((tm, d), lambda i: (i, 0)),
                  pl.BlockSpec((d, LANES), lambda i: (0, 0)),
                  pl.BlockSpec((1, LANES), lambda i: (0, 0)),
                  pl.BlockSpec((tm, tm), lambda i: (0, 0))],
        out_specs=(col, col, pl.BlockSpec((None, 1, LANES), lambda i: (i, 0, 0))),
        compiler_params=_cparams(1), name="moe_router",
    )(h, rw, rb, tri)


DISPATCH_WIN = 128
GROUP_ALIGN = 16
N_PAIRS = N_EXPERTS // 2
WINDOW_RING = 4


def _selection(rank_row, gate_row, win_off):
    tm = rank_row.shape[1]
    i = lax.broadcasted_iota(jnp.int32, (DISPATCH_WIN, tm), 0).astype(F32) + win_off
    hit = i == jnp.broadcast_to(rank_row, (DISPATCH_WIN, tm))
    if gate_row is None:
        return jnp.where(hit, 1.0, 0.0).astype(BF16)
    return jnp.where(hit, jnp.broadcast_to(gate_row, (DISPATCH_WIN, tm)), 0.0).astype(BF16)


def _dispatch_kernel(base_ref, nchunk_ref, rend_ref, h_ref, rank_ref, xg_ref, obuf, sem):
    j = pl.program_id(0)
    w = DISPATCH_WIN

    def window_copy(slot, half, start):
        return pltpu.make_async_copy(obuf.at[slot, pl.ds(half * w, w)], xg_ref.at[pl.ds(start, w)], sem.at[slot, half])

    @pl.when(j == 0)
    def _zero_region_tails():
        obuf[0] = jnp.zeros(obuf.shape[1:], obuf.dtype)
        n_clear = (MOE_BLOCK + DISPATCH_WIN) // w
        for e in range(N_EXPERTS):
            for i in range(n_clear):
                start = jnp.maximum(rend_ref[e] - (i + 1) * w, 0)
                window_copy(0, i % 2, pl.multiple_of(start, GROUP_ALIGN)).start()
        for e in range(N_EXPERTS):
            for i in range(n_clear):
                window_copy(0, i % 2, 0).wait()

    h = h_ref[...]

    def pair_body(p, carry):
        slot = p % WINDOW_RING

        @pl.when(p >= WINDOW_RING)
        def _():
            window_copy(slot, 0, 0).wait()
            window_copy(slot, 1, 0).wait()

        sel = jnp.concatenate([_selection(rank_ref[pl.ds(2 * p + q, 1), :], None, 0.0) for q in range(2)], axis=0)
        obuf[slot] = _dot(sel, h).astype(obuf.dtype)
        for q in range(2):
            window_copy(slot, q, pl.multiple_of(base_ref[j * N_EXPERTS + 2 * p + q], GROUP_ALIGN)).start()
        return carry

    lax.fori_loop(0, N_PAIRS, pair_body, 0)
    for slot in range(WINDOW_RING):
        for half in range(2):
            window_copy(slot, half, 0).wait()

    def overflow_body(e, carry):
        def chunk_body(c, carry2):
            sel = _selection(rank_ref[pl.ds(e, 1), :], None, (c * w).astype(F32))
            obuf[0, 0:w, :] = _dot(sel, h).astype(obuf.dtype)
            cp = window_copy(0, 0, pl.multiple_of(base_ref[j * N_EXPERTS + e] + c * w, GROUP_ALIGN))
            cp.start()
            cp.wait()
            return carry2

        lax.fori_loop(1, nchunk_ref[j * N_EXPERTS + e], chunk_body, 0)
        return carry

    lax.fori_loop(0, N_EXPERTS, overflow_body, 0)


def _dispatch(h, rank_t, base, nchunk, region_end, n_slots, tm):
    t, d = h.shape
    return pl.pallas_call(
        _dispatch_kernel,
        out_shape=jax.ShapeDtypeStruct((n_slots, d), BF16),
        grid_spec=pltpu.PrefetchScalarGridSpec(
            num_scalar_prefetch=3,
            grid=(t // tm,),
            in_specs=[pl.BlockSpec((tm, d), lambda i, *_: (i, 0)),
                      pl.BlockSpec((LANES, tm), lambda i, *_: (0, i))],
            out_specs=pl.BlockSpec(memory_space=pl.ANY),
            scratch_shapes=[pltpu.VMEM((WINDOW_RING, 2 * DISPATCH_WIN, d), BF16), pltpu.SemaphoreType.DMA((WINDOW_RING, 2))]),
        compiler_params=_cparams(1), name="moe_dispatch",
    )(base, nchunk, region_end, h, rank_t)


def _expert_kernel(be_ref, nb_ref, x_ref, wg_ref, wu_ref, wd_ref, o_ref, wg_s, wu_s, wd_s):
    i = pl.program_id(0)
    active = i < nb_ref[0]
    new_expert = (i == 0) | (be_ref[i] != be_ref[jnp.maximum(i - 1, 0)])

    @pl.when(active & new_expert)
    def _():
        wg_s[...] = wg_ref[...].astype(BF16)
        wu_s[...] = wu_ref[...].astype(BF16)
        wd_s[...] = wd_ref[...].astype(BF16)

    @pl.when(active)
    def _():
        x = x_ref[...]
        a = _dot(x, wg_s[...])
        a = a * jax.nn.sigmoid(a) * _dot(x, wu_s[...])
        o_ref[...] = _dot(a.astype(BF16), wd_s[...]).astype(o_ref.dtype)

    @pl.when(jnp.logical_not(active))
    def _():
        o_ref[...] = jnp.zeros(o_ref.shape, o_ref.dtype)


def _expert_ffn(xg, block_exp, n_used, wg, wu, wd, tb):
    n, d = xg.shape
    de = wg.shape[2]
    return pl.pallas_call(
        _expert_kernel,
        out_shape=jax.ShapeDtypeStruct((n, d), BF16),
        grid_spec=pltpu.PrefetchScalarGridSpec(
            num_scalar_prefetch=2,
            grid=(n // tb,),
            in_specs=[pl.BlockSpec((tb, d), lambda i, be, nb: (jnp.minimum(i, nb[0] - 1), 0)),
                      pl.BlockSpec((None, d, de), lambda i, be, nb: (be[i], 0, 0)),
                      pl.BlockSpec((None, d, de), lambda i, be, nb: (be[i], 0, 0)),
                      pl.BlockSpec((None, de, d), lambda i, be, nb: (be[i], 0, 0))],
            out_specs=pl.BlockSpec((tb, d), lambda i, be, nb: (i, 0)),
            scratch_shapes=[pltpu.VMEM((d, de), BF16), pltpu.VMEM((d, de), BF16), pltpu.VMEM((de, d), BF16)]),
        compiler_params=_cparams(1), name="expert_ffn",
    )(block_exp, n_used, xg, wg, wu, wd)


def _moe_combine_kernel(base_ref, nchunk_ref, x_ref, mgate_ref, s_ref, rank_ref, gate_ref, y_ref, o_ref,
                        ybuf, acc_ref, sem, *, tile_off):
    j = pl.program_id(0) + tile_off
    w = DISPATCH_WIN

    def window_copy(slot, half, start):
        return pltpu.make_async_copy(y_ref.at[pl.ds(start, w)], ybuf.at[slot, pl.ds(half * w, w)], sem.at[slot, half])

    def start_pair(p, slot):
        for q in range(2):
            window_copy(slot, q, pl.multiple_of(base_ref[j * N_EXPERTS + 2 * p + q], GROUP_ALIGN)).start()

    ahead = WINDOW_RING - 1
    for p0 in range(ahead):
        start_pair(p0, p0)
    acc_ref[...] = s_ref[...].astype(F32)

    def pair_body(p, carry):
        slot = p % WINDOW_RING
        window_copy(slot, 0, 0).wait()
        window_copy(slot, 1, 0).wait()

        @pl.when(p + ahead < N_PAIRS)
        def _():
            start_pair(p + ahead, (p + ahead) % WINDOW_RING)

        sel = jnp.concatenate([_selection(rank_ref[pl.ds(2 * p + q, 1), :], gate_ref[pl.ds(2 * p + q, 1), :], 0.0)
                               for q in range(2)], axis=0)
        acc_ref[...] += _dot_tn(sel, ybuf[slot])
        return carry

    lax.fori_loop(0, N_PAIRS, pair_body, 0)

    def overflow_body(e, carry):
        def chunk_body(c, carry2):
            cp = window_copy(0, 0, pl.multiple_of(base_ref[j * N_EXPERTS + e] + c * w, GROUP_ALIGN))
            cp.start()
            cp.wait()
            sel = _selection(rank_ref[pl.ds(e, 1), :], gate_ref[pl.ds(e, 1), :], (c * w).astype(F32))
            acc_ref[...] += _dot_tn(sel, ybuf[0, 0:w, :])
            return carry2

        lax.fori_loop(1, nchunk_ref[j * N_EXPERTS + e], chunk_body, 0)
        return carry

    lax.fori_loop(0, N_EXPERTS, overflow_body, 0)
    o_ref[...] = x_ref[...] + mgate_ref[...] * acc_ref[...]


def _moe_combine(x, mod3, mod_row, y, rank_t, gate_t, shared, base, nchunk, tile_off, tm):
    m, d = x.shape
    return pl.pallas_call(
        functools.partial(_moe_combine_kernel, tile_off=tile_off),
        out_shape=jax.ShapeDtypeStruct((m, d), F32),
        grid_spec=pltpu.PrefetchScalarGridSpec(
            num_scalar_prefetch=2,
            grid=(m // tm,),
            in_specs=[pl.BlockSpec((tm, d), lambda i, *_: (i, 0)),
                      pl.BlockSpec((None, 1, d), lambda i, *_: (mod_row(i) * 6 + 5, 0, 0)),
                      pl.BlockSpec((tm, d), lambda i, *_: (i + tile_off, 0)),
                      pl.BlockSpec((LANES, tm), lambda i, *_: (0, i + tile_off)),
                      pl.BlockSpec((LANES, tm), lambda i, *_: (0, i + tile_off)),
                      pl.BlockSpec(memory_space=pl.ANY)],
            out_specs=pl.BlockSpec((tm, d), lambda i, *_: (i, 0)),
            scratch_shapes=[pltpu.VMEM((WINDOW_RING, 2 * DISPATCH_WIN, d), BF16), pltpu.VMEM((tm, d), F32),
                            pltpu.SemaphoreType.DMA((WINDOW_RING, 2))]),
        compiler_params=_cparams(1), name="moe_combine",
    )(base, nchunk, x, mod3, shared, rank_t, gate_t, y)


ROUTER_TILE = 512


def _moe(h, rw, rb, wg, wu, wd, sg, su, sd, layer):
    t, d = h.shape
    nt = t // ROUTER_TILE
    rank_t, gate_t, cnt = _router(h, rw, rb, ROUTER_TILE)
    counts = cnt[:, 0, :N_EXPERTS].astype(jnp.int32)
    aligned = (counts + GROUP_ALIGN - 1) // GROUP_ALIGN * GROUP_ALIGN
    region = (aligned.sum(axis=0) + DISPATCH_WIN + MOE_BLOCK - 1) // MOE_BLOCK * MOE_BLOCK
    region_end = jnp.cumsum(region)
    base = ((region_end - region)[None, :] + jnp.cumsum(aligned, axis=0) - aligned).reshape(-1)
    nchunk = ((counts + DISPATCH_WIN - 1) // DISPATCH_WIN).reshape(-1)
    n_blocks = -(-(t * TOP_K + (GROUP_ALIGN - 1) * nt * N_EXPERTS + N_EXPERTS * (DISPATCH_WIN + MOE_BLOCK - 1))
                 // MOE_BLOCK)
    block_start = jnp.arange(n_blocks, dtype=jnp.int32) * MOE_BLOCK
    block_exp = jnp.minimum((region_end[None, :] <= block_start[:, None]).sum(axis=1), N_EXPERTS - 1)
    n_used = (region_end[N_EXPERTS - 1] // MOE_BLOCK).astype(jnp.int32).reshape(1)
    xg = _dispatch(h, rank_t, base, nchunk, region_end.astype(jnp.int32), n_blocks * MOE_BLOCK, ROUTER_TILE)
    y = _expert_ffn(xg, block_exp.astype(jnp.int32) + layer * N_EXPERTS, n_used, wg, wu, wd, MOE_BLOCK)
    shared = _expert_ffn(h, jnp.full((t // MOE_BLOCK,), layer, jnp.int32), jnp.full((1,), t // MOE_BLOCK, jnp.int32),
                         sg, su, sd, MOE_BLOCK)
    return y, shared, (rank_t, gate_t, base, nchunk)


def _rope_tables(seq_len, gain):
    nf = MLA_ROPE // 4
    inv = ROPE_THETA ** (-jnp.arange(nf, dtype=F32) / nf)
    pos = jnp.arange(seq_len, dtype=jnp.int32)
    ang_r = (pos // GRID_W).astype(F32)[:, None] * inv
    ang_c = (pos % GRID_W).astype(F32)[:, None] * inv
    cos = jnp.concatenate([jnp.cos(ang_r)] * 2 + [jnp.cos(ang_c)] * 2, axis=-1)
    sin = jnp.concatenate([-jnp.sin(ang_r), jnp.sin(ang_r), -jnp.sin(ang_c), jnp.sin(ang_c)], axis=-1)
    g = gain.astype(F32)
    cg = cos * g[None]
    sg = sin * g[_ROPE_PARTNER][None]
    return jnp.tile(cg, (1, 2)), jnp.tile(sg, (1, 2))


def _rope_partner():
    nf = MLA_ROPE // 4
    idx = np.arange(MLA_ROPE)
    within = idx % (2 * nf)
    return np.where(within < nf, idx + nf, idx - nf)


_ROPE_PARTNER = _rope_partner()


def _in_splits():
    sizes = (NA_W, NA_W, NA_W, MLA_Q_LORA, MLA_KV_LORA, MLA_ROPE, CONV_CH, CONV_CH)
    offs = np.concatenate([[0], np.cumsum(sizes)])
    return {n: int(o) for n, o in zip(("q", "k", "v", "cq", "ckv", "kr", "a", "b", "g"), offs)}


def _prep_layer(l, dmodel, w_in, na_q_g, na_k_g, na_rpb, w_na_o, mla_q_lora_g, mla_w_uq, mla_kv_lora_g, mla_w_ukv,
                mla_q_g, mla_k_g, w_mla_o, conv_w_dw, conv_b_dw, conv_ln_g, conv_ln_b, w_conv_o, w_out,
                router_w, router_b, exp_w_gate, exp_w_up, exp_w_down, sh_w_gate, sh_w_up, sh_w_down, seq_len):
    o = _in_splits()
    w = w_in[l]
    cols = lambda a, n: w[:, a:a + n]
    p = {}
    p["w_p1"] = jnp.concatenate([cols(o["q"], 3 * NA_W), cols(o["a"], 2 * CONV_CH), cols(o["g"], 3 * dmodel)],
                                axis=1).astype(BF16)
    kr = cols(o["kr"], MLA_ROPE)
    krs = kr[:, _ROPE_PARTNER]
    p["w_p2"] = jnp.concatenate([cols(o["cq"], MLA_Q_LORA), kr, kr, krs, krs, cols(o["ckv"], MLA_KV_LORA)],
                                axis=1).astype(BF16)
    qscale = NA_HEAD_DIM ** -0.5 * LOG2E
    p["na_g"] = jnp.concatenate([jnp.tile(na_q_g[l].astype(F32) * qscale, NA_HEADS),
                                 jnp.tile(na_k_g[l].astype(F32), NA_HEADS)]).reshape(1, 2 * NA_W)
    p["na_bias"] = _na_bias_table(na_rpb[l])
    wq = mla_w_uq[l].reshape(MLA_Q_LORA, MLA_HEADS, MLA_QK)
    wq_rope = wq[:, :, MLA_NOPE:]
    flat = lambda a: a.reshape(a.shape[0], -1)
    p["w_uq"] = jnp.concatenate([flat(wq[:, :, :MLA_NOPE]), flat(wq_rope), flat(wq_rope[:, :, _ROPE_PARTNER])],
                                axis=1).astype(BF16)
    wkv = mla_w_ukv[l].reshape(MLA_KV_LORA, MLA_HEADS, MLA_NOPE + MLA_V)
    p["w_ukv"] = jnp.concatenate([flat(wkv[:, :, :MLA_NOPE]), flat(wkv[:, :, MLA_NOPE:])], axis=1).astype(BF16)
    p["q_lora_g"] = mla_q_lora_g[l].astype(F32).reshape(1, -1)
    p["kv_lora_g"] = mla_kv_lora_g[l].astype(F32).reshape(1, -1)
    mscale = MLA_QK ** -0.5 * LOG2E
    p["q_gn"] = (mla_q_g[l][:MLA_NOPE].astype(F32) * mscale).reshape(1, -1)
    p["k_gn"] = mla_k_g[l][:MLA_NOPE].astype(F32).reshape(1, -1)
    cgq, sgq = _rope_tables(seq_len, mla_q_g[l][MLA_NOPE:] * mscale)
    cgk, sgk = _rope_tables(seq_len, mla_k_g[l][MLA_NOPE:])
    p["q_tabs"] = (cgq, sgq)
    p["k_tabs"] = (cgk, sgk)
    p["q_tabs_c"] = (jnp.tile((mla_q_g[l][MLA_NOPE:].astype(F32) * mscale)[None], (1, 2)), jnp.zeros((1, LANES), F32))
    p["k_tabs_c"] = (jnp.tile(mla_k_g[l][MLA_NOPE:].astype(F32)[None], (1, 2)), jnp.zeros((1, LANES), F32))
    p["w_na_o"] = w_na_o[l].astype(BF16)
    p["w_mla_o"] = w_mla_o[l].astype(BF16)
    p["w_conv_o"] = w_conv_o[l].astype(BF16)
    p["w_out"] = w_out[l].astype(BF16)
    p["conv_w"] = jnp.concatenate([conv_w_dw[l].astype(F32), jnp.zeros((1, CONV_CH), F32)], axis=0)
    p["conv_b"] = conv_b_dw[l].astype(F32).reshape(1, -1)
    p["ln_g"] = conv_ln_g[l].astype(F32).reshape(1, -1)
    p["ln_b"] = conv_ln_b[l].astype(F32).reshape(1, -1)
    p["router_w"] = jnp.concatenate([router_w[l], jnp.zeros((dmodel, LANES - N_EXPERTS), router_w.dtype)],
                                    axis=1).astype(BF16)
    p["router_b"] = jnp.concatenate([router_b[l].astype(F32), jnp.zeros((LANES - N_EXPERTS,), F32)]).reshape(1, LANES)
    flat_e = lambda a: a.reshape((-1,) + a.shape[2:])
    p["exp"] = (flat_e(exp_w_gate), flat_e(exp_w_up), flat_e(exp_w_down))
    p["sh"] = (sh_w_gate, sh_w_up, sh_w_down)
    return p


def _group_matrices():
    i = np.arange(LANES)
    bd = (i[:, None] // NA_HEAD_DIM == i[None, :] // NA_HEAD_DIM)
    ones = np.ones((LANES, LANES), bool)
    e0 = np.broadcast_to((i < MLA_ROPE)[:, None], (LANES, LANES))
    e1 = np.broadcast_to((i >= MLA_ROPE)[:, None], (LANES, LANES))
    f = lambda m: jnp.asarray(m, BF16)
    return f(bd), (f(ones), f(e0), f(e1))


def _expand_tabs(tabs, tm):
    return tuple(jnp.broadcast_to(t, (tm, LANES)) if t.shape[0] == 1 else t for t in tabs)


def _mixer_inputs(h, p, mats, bd, seq, ctx):
    tm_mm = min(1024, h.shape[0])
    p1 =_matmul(h, p["w_p1"], BF16, tm_mm, 1024, "in_proj_1")
    p2 = _matmul(h, p["w_p2"], BF16, tm_mm, P2_W // 2, "in_proj_2")
    qk = _headnorm(p1, p["na_g"], bd, min(512, h.shape[0]))
    tr = 256
    qt = _expand_tabs(p["q_tabs_c"], tr) if ctx else p["q_tabs"]
    kt = _expand_tabs(p["k_tabs_c"], tr) if ctx else p["k_tabs"]
    tseq = tr if ctx else seq
    km, vm = _mla_kv(p2, p["kv_lora_g"], p["w_ukv"], p["k_gn"], kt[0], kt[1], mats, tseq, tr)
    return p1, p2, qk, km, vm, (qt, tseq, tr)


def kernel(x, c, ctx, c_ctx, ada_w, ada_b, norm1_g, norm2_g, w_in, na_q_g, na_k_g, na_rpb, w_na_o, mla_q_lora_g, mla_w_uq, mla_kv_lora_g, mla_w_ukv, mla_q_g, mla_k_g, w_mla_o, conv_w_dw, conv_b_dw, conv_ln_g, conv_ln_b, w_conv_o, w_out, router_w, router_b, exp_w_gate, exp_w_up, exp_w_down, sh_w_gate, sh_w_up, sh_w_down):
    batch, seq, d = x.shape
    lc = ctx.shape[1]
    depth = ada_w.shape[0]
    mod_rows = (batch + 1 + 7) // 8 * 8
    cc = jnp.concatenate([c, c_ctx[None], jnp.zeros((mod_rows - batch - 1, d), c.dtype)], axis=0)
    mod_all = _ada(cc, ada_w, ada_b)
    bd, mats = _group_matrices()
    xs = x.reshape(batch * seq, d)
    zs = ctx.reshape(batch * lc, d)
    tm_x = 512
    tm_c = 256
    x_row = lambda i: (i * tm_x) // seq
    c_row = lambda i: batch

    for l in range(depth):
        need_ctx = l < depth - 1
        p = _prep_layer(l, d, w_in, na_q_g, na_k_g, na_rpb, w_na_o, mla_q_lora_g, mla_w_uq, mla_kv_lora_g, mla_w_ukv,
                        mla_q_g, mla_k_g, w_mla_o, conv_w_dw, conv_b_dw, conv_ln_g, conv_ln_b, w_conv_o, w_out,
                        router_w, router_b, exp_w_gate, exp_w_up, exp_w_down, sh_w_gate, sh_w_up, sh_w_down, seq)
        mod3 = mod_all[l].reshape(mod_rows * 6, 1, d)
        hx = _norm_mod(xs, norm1_g[l], mod3, x_row, 1, 0, tm_x)
        hc = _norm_mod(zs, norm1_g[l], mod3, c_row, 1, 0, tm_c)
        p1x, p2x, qkx, kmx, vmx, (qtx, tsx, trx) = _mixer_inputs(hx, p, mats, bd, seq, False)
        p1c, p2c, qkc, kmc, vmc, (qtc, tsc, trc) = _mixer_inputs(hc, p, mats, bd, lc, True)
        o_na = _na_attention(qkx, p1x, qkc, p1c, p["na_bias"], batch)
        qmx = _mla_q(p2x, p["q_lora_g"], p["w_uq"], p["q_gn"], qtx[0], qtx[1], mats, tsx, trx)
        o_mla = _mla_attention(qmx, [(kmc, vmc), (kmx, vmx)], batch, 512)
        u = _conv_module(p1x, p["conv_w"], p["conv_b"], p["ln_g"], p["ln_b"], seq, 256)
        mrg = _merge(o_na, o_mla, u, p1x, p["w_na_o"], p["w_mla_o"], p["w_conv_o"], 1024, 512)
        xs, hx2 = _out_proj(mrg, p["w_out"], xs, mod3, x_row, norm2_g[l], tm_x)
        if need_ctx:
            o_na_c = _na_ctx_attention(qkc, p1c, batch)
            qmc = _mla_q(p2c, p["q_lora_g"], p["w_uq"], p["q_gn"], qtc[0], qtc[1], mats, tsc, trc)
            o_mla_c = _mla_attention(qmc, [(kmc, vmc)], batch, lc)
            u_c = _conv_module(p1c, p["conv_w"], p["conv_b"], p["ln_g"], p["ln_b"], lc, lc)
            mrg_c = _merge(o_na_c, o_mla_c, u_c, p1c, p["w_na_o"], p["w_mla_o"], p["w_conv_o"], 1024, 512)
            zs, hc2 = _out_proj(mrg_c, p["w_out"], zs, mod3, c_row, norm2_g[l], tm_c)
            tokens = jnp.concatenate([hc2, hx2], axis=0)
        else:
            tokens = hx2
        y, shared, (rank_t, gate_t, base, nchunk) = _moe(tokens, p["router_w"], p["router_b"], *p["exp"], *p["sh"], l)
        tm_cmb = ROUTER_TILE
        x_row_cmb = lambda i: (i * tm_cmb) // seq
        if need_ctx:
            zs = _moe_combine(zs, mod3, c_row, y, rank_t, gate_t, shared, base, nchunk, 0, tm_cmb)
            xs = _moe_combine(xs, mod3, x_row_cmb, y, rank_t, gate_t, shared, base, nchunk, batch * lc // tm_cmb,
                              tm_cmb)
        else:
            xs = _moe_combine(xs, mod3, x_row_cmb, y, rank_t, gate_t, shared, base, nchunk, 0, tm_cmb)
    return xs.reshape(batch, seq, d)
```

```python
import functools

import numpy as np
import jax
import jax.numpy as jnp
from jax import lax
from jax.experimental import pallas as pl
from jax.experimental.pallas import tpu as pltpu

F32 = jnp.float32
BF16 = jnp.bfloat16

GRID_W = 64
EPS = 1e-6
NEG_INF = -1e30
NA_HEADS = 16
NA_HEAD_DIM = 64
NA_W = NA_HEADS * NA_HEAD_DIM
NA_WIN_ROWS = 8
NA_WIN_COLS = 16
MLA_HEADS = 16
MLA_NOPE = 128
MLA_ROPE = 64
MLA_QK = MLA_NOPE + MLA_ROPE
MLA_V = 128
MLA_Q_LORA = 768
MLA_KV_LORA = 512
ROPE_THETA = 10000.0
CONV_CH = 1024
CONV_WIDTH = 31
CONV_HALO = 16
CONV_ACC_ROWS = 128
N_EXPERTS = 64
TOP_K = 8
D_EXPERT = 512
ROUTED_SCALE = 2.5
MOE_BLOCK = 256
LANES = 128
SUBLANES = 8
MLA_HEAD_PAD = 2 * LANES
VMEM_LIMIT = 56 * 1024 * 1024

P1_Q, P1_K, P1_V, P1_A, P1_B, P1_GNA, P1_GMLA, P1_GCV = 0, 1024, 2048, 3072, 4096, 5120, 7168, 9216
P1_W = 11264
P2_CQ, P2_KR, P2_KRS, P2_CKV = 0, 768, 896, 1024
P2_W = 1536


def _cparams(n_axes):
    return pltpu.CompilerParams(dimension_semantics=("arbitrary",) * n_axes, vmem_limit_bytes=VMEM_LIMIT)


def _dot(a, b):
    return jnp.dot(a, b, preferred_element_type=F32)


def _dot_nt(a, b):
    return lax.dot_general(a, b, (((1,), (1,)), ((), ())), preferred_element_type=F32)


def _dot_hilo(x, m):
    hi = x.astype(BF16)
    lo = (x - hi.astype(F32)).astype(BF16)
    return _dot(hi, m) + _dot(lo, m)


def _ada_kernel(c_ref, w_ref, b_ref, o_ref):
    c = c_ref[...]
    a = (c * jax.nn.sigmoid(c)).astype(BF16)
    o_ref[...] = _dot(a, w_ref[...].astype(BF16)) + b_ref[...]


def _ada(cc, ada_w, ada_b):
    nl, d, n = ada_w.shape
    r = cc.shape[0]
    tn = 1024
    return pl.pallas_call(
        _ada_kernel,
        out_shape=jax.ShapeDtypeStruct((nl, r, n), F32),
        grid=(nl, n // tn),
        in_specs=[pl.BlockSpec((r, d), lambda l, j: (0, 0)),
                  pl.BlockSpec((None, d, tn), lambda l, j: (l, 0, j)),
                  pl.BlockSpec((None, 1, tn), lambda l, j: (l, 0, j))],
        out_specs=pl.BlockSpec((None, r, tn), lambda l, j: (l, 0, j)),
        compiler_params=_cparams(2), name="ada_mod",
    )(cc, ada_w, ada_b.reshape(nl, 1, n))


def _norm_mod_kernel(x_ref, g_ref, sc_ref, sh_ref, o_ref):
    x = x_ref[...]
    y = x * lax.rsqrt(jnp.mean(x * x, axis=-1, keepdims=True) + EPS) * g_ref[...]
    o_ref[...] = (y * (1.0 + sc_ref[...]) + sh_ref[...]).astype(o_ref.dtype)


def _norm_mod(x, g, mod3, mod_row, k_sc, k_sh, tm):
    m, d = x.shape
    return pl.pallas_call(
        _norm_mod_kernel,
        out_shape=jax.ShapeDtypeStruct((m, d), BF16),
        grid=(m // tm,),
        in_specs=[pl.BlockSpec((tm, d), lambda i: (i, 0)),
                  pl.BlockSpec((1, d), lambda i: (0, 0)),
                  pl.BlockSpec((None, 1, d), lambda i: (mod_row(i) * 6 + k_sc, 0, 0)),
                  pl.BlockSpec((None, 1, d), lambda i: (mod_row(i) * 6 + k_sh, 0, 0))],
        out_specs=pl.BlockSpec((tm, d), lambda i: (i, 0)),
        compiler_params=_cparams(1), name="norm_mod",
    )(x, g.reshape(1, d), mod3, mod3)


def _mm_kernel(a_ref, w_ref, o_ref):
    o_ref[...] = _dot(a_ref[...], w_ref[...]).astype(o_ref.dtype)


def _matmul(a, w, out_dtype, tm, tn, name):
    m, k = a.shape
    n = w.shape[1]
    return pl.pallas_call(
        _mm_kernel,
        out_shape=jax.ShapeDtypeStruct((m, n), out_dtype),
        grid=(m // tm, n // tn),
        in_specs=[pl.BlockSpec((tm, k), lambda i, j: (i, 0)),
                  pl.BlockSpec((k, tn), lambda i, j: (0, j))],
        out_specs=pl.BlockSpec((tm, tn), lambda i, j: (i, j)),
        compiler_params=_cparams(2), name=name,
    )(a, w)


def _headnorm_kernel(x_ref, g_ref, bd_ref, o_ref):
    bd = bd_ref[...]
    for c in range(x_ref.shape[1] // LANES):
        sl = slice(c * LANES, (c + 1) * LANES)
        x = x_ref[:, sl].astype(F32)
        ss = _dot_hilo(x * x, bd)
        y = x * lax.rsqrt(ss * (1.0 / NA_HEAD_DIM) + EPS) * g_ref[:, sl]
        o_ref[:, sl] = y.astype(o_ref.dtype)


def _headnorm(p1, g_row, bd, tm):
    m = p1.shape[0]
    w = g_row.shape[1]
    return pl.pallas_call(
        _headnorm_kernel,
        out_shape=jax.ShapeDtypeStruct((m, w), BF16),
        grid=(m // tm,),
        in_specs=[pl.BlockSpec((tm, w), lambda i: (i, 0)),
                  pl.BlockSpec((1, w), lambda i: (0, 0)),
                  pl.BlockSpec((LANES, LANES), lambda i: (0, 0))],
        out_specs=pl.BlockSpec((tm, w), lambda i: (i, 0)),
        compiler_params=_cparams(1), name="na_headnorm",
    )(p1, g_row, bd)


LOG2E = 1.4426950408889634


def _softmax_pv(scores, values):
    m = scores[0].max(axis=-1, keepdims=True)
    for s in scores[1:]:
        m = jnp.maximum(m, s.max(axis=-1, keepdims=True))
    l = None
    o = None
    for s, v in zip(scores, values):
        p = jnp.exp2(s - m)
        ps = p.sum(axis=-1, keepdims=True)
        po = _dot(p.astype(BF16), v)
        l = ps if l is None else l + ps
        o = po if o is None else o + po
    return o / l


def _dot_tn(a, b):
    return lax.dot_general(a, b, (((0,), (0,)), ((), ())), preferred_element_type=F32)


def _na_kernel(q_ref, kx_ref, vx_ref, kc_ref, vc_ref, bias_ref, o_ref):
    rows = kx_ref.shape[0] // GRID_W
    r = pl.program_id(1)
    r0 = jnp.clip(r - NA_WIN_ROWS // 2, 0, rows - NA_WIN_ROWS)
    kstart = pl.multiple_of(r0 * GRID_W, GRID_W)
    nwin = NA_WIN_ROWS * GRID_W
    lo = lax.broadcasted_iota(jnp.int32, (GRID_W, LANES), 1) < NA_HEAD_DIM
    zero = jnp.zeros((GRID_W, LANES), q_ref.dtype)
    n_pairs = NA_W // LANES
    pair = lambda p: slice(p * LANES, (p + 1) * LANES)
    scores = []
    for p in range(n_pairs):
        q2 = q_ref[:, pair(p)]
        qs = jnp.concatenate([jnp.where(lo, q2, zero), jnp.where(lo, zero, q2)], axis=0)
        s_w = _dot_nt(kx_ref[pl.ds(kstart, nwin), pair(p)], qs) + bias_ref[p]
        s_c = _dot_nt(kc_ref[:, pair(p)], qs)
        scores.append((s_w, s_c))
    probs = []
    for s_w, s_c in scores:
        m = jnp.maximum(s_w.max(axis=0, keepdims=True), s_c.max(axis=0, keepdims=True))
        p_w = jnp.exp2(s_w - m)
        p_c = jnp.exp2(s_c - m)
        inv = 1.0 / (p_w.sum(axis=0, keepdims=True) + p_c.sum(axis=0, keepdims=True))
        probs.append(((p_w * inv).astype(BF16), (p_c * inv).astype(BF16)))
    outs = []
    for p, (p_w, p_c) in enumerate(probs):
        o2 = _dot_tn(p_w, vx_ref[pl.ds(kstart, nwin), pair(p)]) + _dot_tn(p_c, vc_ref[:, pair(p)])
        outs.append(jnp.where(lo, o2[:GRID_W], o2[GRID_W:]).astype(o_ref.dtype))
    o_ref[...] = jnp.concatenate(outs, axis=1)


def _na_attention(qk_x, p1_x, qk_c, p1_c, bias_tab, batch):
    l = qk_x.shape[0] // batch
    lc = qk_c.shape[0] // batch
    rows = l // GRID_W
    nwin = NA_WIN_ROWS * GRID_W

    def bias_idx(b, r):
        r0 = jnp.clip(r - NA_WIN_ROWS // 2, 0, rows - NA_WIN_ROWS)
        return (r - r0, 0, 0, 0)

    return pl.pallas_call(
        _na_kernel,
        out_shape=jax.ShapeDtypeStruct((batch * l, NA_W), BF16),
        grid=(batch, rows),
        in_specs=[pl.BlockSpec((GRID_W, NA_W), lambda b, r: (b * rows + r, 0)),
                  pl.BlockSpec((l, NA_W), lambda b, r: (b, 1)),
                  pl.BlockSpec((l, NA_W), lambda b, r: (b, P1_V // NA_W)),
                  pl.BlockSpec((lc, NA_W), lambda b, r: (b, 1)),
                  pl.BlockSpec((lc, NA_W), lambda b, r: (b, P1_V // NA_W)),
                  pl.BlockSpec((None, NA_HEADS // 2, nwin, LANES), bias_idx)],
        out_specs=pl.BlockSpec((GRID_W, NA_W), lambda b, r: (b * rows + r, 0)),
        compiler_params=_cparams(2), name="na_attention",
    )(qk_x, qk_x, p1_x, qk_c, p1_c, bias_tab)


def _na_ctx_kernel(q_ref, k_ref, v_ref, o_ref):
    n = q_ref.shape[0]
    lo = lax.broadcasted_iota(jnp.int32, (n, LANES), 1) < NA_HEAD_DIM
    zero = jnp.zeros((n, LANES), q_ref.dtype)
    for p in range(NA_W // LANES):
        sl = slice(p * LANES, (p + 1) * LANES)
        q2 = q_ref[:, sl]
        k = k_ref[:, sl]
        v = v_ref[:, sl]
        outs = []
        for hh in range(2):
            qm = jnp.where(lo, q2, zero) if hh == 0 else jnp.where(lo, zero, q2)
            outs.append(_softmax_pv([_dot_nt(qm, k)], [v]))
        o_ref[:, sl] = jnp.where(lo, outs[0], outs[1]).astype(o_ref.dtype)


def _na_ctx_attention(qk_c, p1_c, batch):
    lc = qk_c.shape[0] // batch
    return pl.pallas_call(
        _na_ctx_kernel,
        out_shape=jax.ShapeDtypeStruct((batch * lc, NA_W), BF16),
        grid=(batch,),
        in_specs=[pl.BlockSpec((lc, NA_W), lambda b: (b, 0)),
                  pl.BlockSpec((lc, NA_W), lambda b: (b, 1)),
                  pl.BlockSpec((lc, NA_W), lambda b: (b, P1_V // NA_W))],
        out_specs=pl.BlockSpec((lc, NA_W), lambda b: (b, 0)),
        compiler_params=_cparams(1), name="na_ctx_attention",
    )(qk_c, qk_c, p1_c)


def _na_bias_table(rpb):
    cls = np.arange(NA_WIN_ROWS)[:, None]
    w = np.arange(NA_WIN_ROWS)[None, :]
    qc = np.arange(GRID_W)[:, None]
    kc = np.arange(GRID_W)[None, :]
    c0 = np.clip(qc - NA_WIN_COLS // 2, 0, GRID_W - NA_WIN_COLS)
    ok = (kc >= c0) & (kc < c0 + NA_WIN_COLS)
    dr = w - cls + NA_WIN_ROWS - 1
    dc = np.clip(kc - qc + NA_WIN_COLS - 1, 0, 2 * NA_WIN_COLS - 2)
    row_sel = jnp.asarray(dr[:, :, None] == np.arange(2 * NA_WIN_ROWS - 1), F32)
    col_sel = jnp.asarray(dc[:, :, None] == np.arange(2 * NA_WIN_COLS - 1), F32)
    exact = lax.Precision.HIGHEST
    t1 = jnp.einsum('hab,cwa->hcwb', rpb.astype(F32) * LOG2E, row_sel, precision=exact)
    tab = jnp.einsum('hcwb,qkb->chwkq', t1, col_sel, precision=exact)
    tab = jnp.where(jnp.asarray(ok.T)[None, None, None], tab, NEG_INF)
    tab = tab.reshape(NA_WIN_ROWS, NA_HEADS // 2, 2, NA_WIN_ROWS * GRID_W, GRID_W)
    tab = jnp.transpose(tab, (0, 1, 3, 2, 4))
    return tab.reshape(NA_WIN_ROWS, NA_HEADS // 2, NA_WIN_ROWS * GRID_W, LANES)


def _mla_q_kernel(p2_ref, gl_ref, w_ref, gn_ref, cg_ref, sg_ref, ones_ref, e0_ref, e1_ref, o_ref):
    cq = p2_ref[...].astype(F32)
    cqn = cq * lax.rsqrt(jnp.mean(cq * cq, axis=-1, keepdims=True) + EPS) * gl_ref[...]
    y = _dot(cqn.astype(BF16), w_ref[...])
    tm = y.shape[0]
    lo = lax.broadcasted_iota(jnp.int32, (tm, LANES), 1) < MLA_ROPE
    nope_w = MLA_HEADS * MLA_NOPE
    rope_w = MLA_HEADS * MLA_ROPE
    ones, e0, e1 = ones_ref[...], e0_ref[...], e1_ref[...]
    gn = gn_ref[...]
    for p in range(MLA_HEADS // 2):
        yr = y[:, nope_w + p * LANES: nope_w + (p + 1) * LANES]
        yrs = y[:, nope_w + rope_w + p * LANES: nope_w + rope_w + (p + 1) * LANES]
        rot = yr * cg_ref[...] + yrs * sg_ref[...]
        yr2 = yr * yr
        ssr = (_dot_hilo(yr2, e0), _dot_hilo(yr2, e1))
        for hh in range(2):
            h = 2 * p + hh
            nope = y[:, h * MLA_NOPE:(h + 1) * MLA_NOPE]
            tot = _dot_hilo(nope * nope, ones) + ssr[hh]
            scale = lax.rsqrt(tot * (1.0 / MLA_QK) + EPS)
            o_ref[:, h * MLA_HEAD_PAD: h * MLA_HEAD_PAD + LANES] = (nope * scale * gn).astype(o_ref.dtype)
            rs = rot * scale
            rs = jnp.where(lo, rs, 0.0) if hh == 0 else jnp.where(lo, 0.0, rs)
            o_ref[:, h * MLA_HEAD_PAD + LANES: (h + 1) * MLA_HEAD_PAD] = rs.astype(o_ref.dtype)


def _mla_q(p2, g_lora, w_uq, g_nope, cg, sg, mats, seq, tm):
    m = p2.shape[0]
    nt = seq // tm
    ones, e0, e1 = mats
    full = lambda shape: pl.BlockSpec(shape, lambda i: (0,) * len(shape))
    return pl.pallas_call(
        _mla_q_kernel,
        out_shape=jax.ShapeDtypeStruct((m, MLA_HEADS * MLA_HEAD_PAD), BF16),
        grid=(m // tm,),
        in_specs=[pl.BlockSpec((tm, MLA_Q_LORA), lambda i: (i, 0)),
                  full((1, MLA_Q_LORA)),
                  full(w_uq.shape),
                  full((1, LANES)),
                  pl.BlockSpec((tm, LANES), lambda i: (i % nt, 0)),
                  pl.BlockSpec((tm, LANES), lambda i: (i % nt, 0)),
                  full((LANES, LANES)), full((LANES, LANES)), full((LANES, LANES))],
        out_specs=pl.BlockSpec((tm, MLA_HEADS * MLA_HEAD_PAD), lambda i: (i, 0)),
        compiler_params=_cparams(1), name="mla_q_proj",
    )(p2, g_lora, w_uq, g_nope, cg, sg, ones, e0, e1)


def _mla_kv_kernel(ckv_ref, kr_ref, krs_ref, gl_ref, w_ref, gn_ref, cg_ref, sg_ref, ones_ref, e0_ref, k_ref, v_ref):
    ckv = ckv_ref[...].astype(F32)
    cn = ckv * lax.rsqrt(jnp.mean(ckv * ckv, axis=-1, keepdims=True) + EPS) * gl_ref[...]
    y = _dot(cn.astype(BF16), w_ref[...])
    tm = y.shape[0]
    lo = lax.broadcasted_iota(jnp.int32, (tm, LANES), 1) < MLA_ROPE
    nope_w = MLA_HEADS * MLA_NOPE
    v_ref[...] = y[:, nope_w:].astype(v_ref.dtype)
    kr = kr_ref[...].astype(F32)
    rot = kr * cg_ref[...] + krs_ref[...].astype(F32) * sg_ref[...]
    ssr = _dot_hilo(kr * kr, e0_ref[...])
    ones = ones_ref[...]
    gn = gn_ref[...]
    for h in range(MLA_HEADS):
        nope = y[:, h * MLA_NOPE:(h + 1) * MLA_NOPE]
        tot = _dot_hilo(nope * nope, ones) + ssr
        scale = lax.rsqrt(tot * (1.0 / MLA_QK) + EPS)
        k_ref[:, h * MLA_HEAD_PAD: h * MLA_HEAD_PAD + LANES] = (nope * scale * gn).astype(k_ref.dtype)
        rs = rot * scale
        rs = jnp.where(lo, rs, 0.0) if h % 2 == 0 else jnp.where(lo, 0.0, rs)
        k_ref[:, h * MLA_HEAD_PAD + LANES: (h + 1) * MLA_HEAD_PAD] = rs.astype(k_ref.dtype)


def _mla_kv(p2, g_lora, w_ukv, g_nope, cg, sg, mats, seq, tm):
    m = p2.shape[0]
    nt = seq // tm
    ones, e0, _ = mats
    full = lambda shape: pl.BlockSpec(shape, lambda i: (0,) * len(shape))
    return pl.pallas_call(
        _mla_kv_kernel,
        out_shape=(jax.ShapeDtypeStruct((m, MLA_HEADS * MLA_HEAD_PAD), BF16),
                   jax.ShapeDtypeStruct((m, MLA_HEADS * MLA_V), BF16)),
        grid=(m // tm,),
        in_specs=[pl.BlockSpec((tm, MLA_KV_LORA), lambda i: (i, P2_CKV // MLA_KV_LORA)),
                  pl.BlockSpec((tm, LANES), lambda i: (i, P2_KR // LANES)),
                  pl.BlockSpec((tm, LANES), lambda i: (i, P2_KRS // LANES)),
                  full((1, MLA_KV_LORA)),
                  full(w_ukv.shape),
                  full((1, LANES)),
                  pl.BlockSpec((tm, LANES), lambda i: (i % nt, 0)),
                  pl.BlockSpec((tm, LANES), lambda i: (i % nt, 0)),
                  full((LANES, LANES)), full((LANES, LANES))],
        out_specs=(pl.BlockSpec((tm, MLA_HEADS * MLA_HEAD_PAD), lambda i: (i, 0)),
                   pl.BlockSpec((tm, MLA_HEADS * MLA_V), lambda i: (i, 0))),
        compiler_params=_cparams(1), name="mla_kv_proj",
    )(p2, p2, p2, g_lora, w_ukv, g_nope, cg, sg, ones, e0)


def _mla_attn_kernel(*refs, n_kv):
    q_ref = refs[0]
    kv = refs[1:1 + 2 * n_kv]
    o_ref = refs[1 + 2 * n_kv]
    heads = range(MLA_HEADS_PER_STEP)
    qsl = lambda hh: slice(hh * MLA_HEAD_PAD, (hh + 1) * MLA_HEAD_PAD)
    vsl = lambda hh: slice(hh * MLA_V, (hh + 1) * MLA_V)
    scores = [[_dot_nt(q_ref[:, qsl(hh)], kv[2 * i][:, qsl(hh)]) for i in range(n_kv)] for hh in heads]
    probs = []
    for s_list in scores:
        m = s_list[0].max(axis=-1, keepdims=True)
        for s in s_list[1:]:
            m = jnp.maximum(m, s.max(axis=-1, keepdims=True))
        p_list = [jnp.exp2(s - m) for s in s_list]
        l = p_list[0].sum(axis=-1, keepdims=True)
        for p in p_list[1:]:
            l = l + p.sum(axis=-1, keepdims=True)
        probs.append(([p.astype(BF16) for p in p_list], l))
    outs = []
    for hh, (p_list, l) in zip(heads, probs):
        o = _dot(p_list[0], kv[1][:, vsl(hh)])
        for i in range(1, n_kv):
            o = o + _dot(p_list[i], kv[2 * i + 1][:, vsl(hh)])
        outs.append((o / l).astype(o_ref.dtype))
    o_ref[...] = jnp.concatenate(outs, axis=1)


MLA_HEADS_PER_STEP = 2


def _mla_attention(q, kvs, batch, tq):
    lq = q.shape[0] // batch
    nq = lq // tq
    hs = MLA_HEADS_PER_STEP
    in_specs = [pl.BlockSpec((tq, hs * MLA_HEAD_PAD), lambda b, h, j: (b * nq + j, h))]
    args = [q]
    for k, v in kvs:
        s = k.shape[0] // batch
        in_specs.append(pl.BlockSpec((s, hs * MLA_HEAD_PAD), lambda b, h, j: (b, h)))
        in_specs.append(pl.BlockSpec((s, hs * MLA_V), lambda b, h, j: (b, h)))
        args += [k, v]
    return pl.pallas_call(
        functools.partial(_mla_attn_kernel, n_kv=len(kvs)),
        out_shape=jax.ShapeDtypeStruct((batch * lq, MLA_HEADS * MLA_V), BF16),
        grid=(batch, MLA_HEADS // hs, nq),
        in_specs=in_specs,
        out_specs=pl.BlockSpec((tq, hs * MLA_V), lambda b, h, j: (b * nq + j, h)),
        compiler_params=_cparams(3), name="mla_attention",
    )(*args)


def _conv_kernel(ap_ref, a_ref, an_ref, bp_ref, b_ref, bn_ref, w_ref, cb_ref, g_ref, beta_ref, o_ref, u_scr, sh_scr,
                 y_scr, *, n_tiles):
    j = pl.program_id(1)
    tt = a_ref.shape[0]
    glu = lambda a, b: a[...].astype(F32) * jax.nn.sigmoid(b[...].astype(F32))
    u_scr[0:CONV_HALO, :] = jnp.where(j > 0, glu(ap_ref, bp_ref), 0.0)
    u_scr[CONV_HALO:CONV_HALO + tt, :] = glu(a_ref, b_ref)
    u_scr[CONV_HALO + tt:, :] = jnp.where(j < n_tiles - 1, glu(an_ref, bn_ref), 0.0)
    n_sh = sh_scr.shape[1]
    for b in range(SUBLANES):
        sh_scr[b] = u_scr[b:b + n_sh, :]
    off = CONV_HALO - CONV_WIDTH // 2
    rows = min(tt, CONV_ACC_ROWS)
    for c in range(CONV_CH // LANES):
        sl = slice(c * LANES, (c + 1) * LANES)
        for r0 in range(0, tt, rows):
            acc = jnp.zeros((rows, LANES), F32)
            for k in range(CONV_WIDTH):
                a8, b = (off + k) // SUBLANES * SUBLANES + r0, (off + k) % SUBLANES
                acc = acc + sh_scr[b, a8:a8 + rows, sl] * w_ref[k:k + 1, sl]
            y_scr[r0:r0 + rows, sl] = acc + cb_ref[:, sl]
    y = y_scr[...]
    mu = jnp.mean(y, axis=-1, keepdims=True)
    d = y - mu
    var = jnp.mean(d * d, axis=-1, keepdims=True)
    z = d * lax.rsqrt(var + EPS) * g_ref[...] + beta_ref[...]
    o_ref[...] = (z * jax.nn.sigmoid(z)).astype(o_ref.dtype)


def _conv_module(p1, w_dw, b_dw, ln_g, ln_b, seq, tt):
    m = p1.shape[0]
    batch = m // seq
    n_tiles = seq // tt
    hb = tt // CONV_HALO
    n_hblk = m // CONV_HALO
    ca, cb = P1_A // CONV_CH, P1_B // CONV_CH

    def prev_idx(c):
        return lambda b, j: (jnp.maximum((b * n_tiles + j) * hb - 1, 0), c)

    def next_idx(c):
        return lambda b, j: (jnp.minimum((b * n_tiles + j + 1) * hb, n_hblk - 1), c)

    cur = lambda c: (lambda b, j: (b * n_tiles + j, c))
    full = lambda shape: pl.BlockSpec(shape, lambda b, j: (0,) * len(shape))
    return pl.pallas_call(
        functools.partial(_conv_kernel, n_tiles=n_tiles),
        out_shape=jax.ShapeDtypeStruct((m, CONV_CH), BF16),
        grid=(batch, n_tiles),
        in_specs=[pl.BlockSpec((CONV_HALO, CONV_CH), prev_idx(ca)),
                  pl.BlockSpec((tt, CONV_CH), cur(ca)),
                  pl.BlockSpec((CONV_HALO, CONV_CH), next_idx(ca)),
                  pl.BlockSpec((CONV_HALO, CONV_CH), prev_idx(cb)),
                  pl.BlockSpec((tt, CONV_CH), cur(cb)),
                  pl.BlockSpec((CONV_HALO, CONV_CH), next_idx(cb)),
                  full((32, CONV_CH)), full((1, CONV_CH)), full((1, CONV_CH)), full((1, CONV_CH))],
        out_specs=pl.BlockSpec((tt, CONV_CH), lambda b, j: (b * n_tiles + j, 0)),
        scratch_shapes=[pltpu.VMEM((tt + 2 * CONV_HALO, CONV_CH), F32),
                        pltpu.VMEM((SUBLANES, tt + 2 * CONV_HALO - SUBLANES, CONV_CH), F32),
                        pltpu.VMEM((tt, CONV_CH), F32)],
        compiler_params=_cparams(2), name="conv_module",
    )(p1, p1, p1, p1, p1, p1, w_dw, b_dw, ln_g, ln_b)


def _merge_kernel(ona_ref, omla_ref, u_ref, gna_ref, gmla_ref, gcv_ref, wna_ref, wmla_ref, wcv_ref, o_ref):
    sig = lambda r: jax.nn.sigmoid(r[...].astype(F32))
    m = sig(gna_ref) * _dot(ona_ref[...], wna_ref[...])
    m = m + sig(gmla_ref) * _dot(omla_ref[...], wmla_ref[...])
    m = m + sig(gcv_ref) * _dot(u_ref[...], wcv_ref[...])
    o_ref[...] = m.astype(o_ref.dtype)


def _merge(o_na, o_mla, u, p1, w_na_o, w_mla_o, w_conv_o, tm, tn):
    m = o_na.shape[0]
    d = w_na_o.shape[1]
    tm = min(tm, m)
    row = lambda k: pl.BlockSpec((tm, k), lambda i, j: (i, 0))
    gate = lambda off: pl.BlockSpec((tm, tn), lambda i, j: (i, off // tn + j))
    wcol = lambda k: pl.BlockSpec((k, tn), lambda i, j: (0, j))
    return pl.pallas_call(
        _merge_kernel,
        out_shape=jax.ShapeDtypeStruct((m, d), BF16),
        grid=(m // tm, d // tn),
        in_specs=[row(o_na.shape[1]), row(o_mla.shape[1]), row(u.shape[1]),
                  gate(P1_GNA), gate(P1_GMLA), gate(P1_GCV),
                  wcol(w_na_o.shape[0]), wcol(w_mla_o.shape[0]), wcol(w_conv_o.shape[0])],
        out_specs=pl.BlockSpec((tm, tn), lambda i, j: (i, j)),
        compiler_params=_cparams(2), name="gated_merge",
    )(o_na, o_mla, u, p1, p1, p1, w_na_o, w_mla_o, w_conv_o)


def _out_proj_kernel(m_ref, w_ref, x_ref, gate_ref, g2_ref, sc_ref, sh_ref, xo_ref, ho_ref):
    x = x_ref[...] + gate_ref[...] * _dot(m_ref[...], w_ref[...])
    xo_ref[...] = x
    y = x * lax.rsqrt(jnp.mean(x * x, axis=-1, keepdims=True) + EPS) * g2_ref[...]
    ho_ref[...] = (y * (1.0 + sc_ref[...]) + sh_ref[...]).astype(ho_ref.dtype)


def _out_proj(mrg, w_out, x, mod3, mod_row, g2, tm):
    m, d = x.shape
    modspec = lambda k: pl.BlockSpec((None, 1, d), lambda i: (mod_row(i) * 6 + k, 0, 0))
    return pl.pallas_call(
        _out_proj_kernel,
        out_shape=(jax.ShapeDtypeStruct((m, d), F32), jax.ShapeDtypeStruct((m, d), BF16)),
        grid=(m // tm,),
        in_specs=[pl.BlockSpec((tm, d), lambda i: (i, 0)),
                  pl.BlockSpec((d, d), lambda i: (0, 0)),
                  pl.BlockSpec((tm, d), lambda i: (i, 0)),
                  modspec(2),
                  pl.BlockSpec((1, d), lambda i: (0, 0)),
                  modspec(4), modspec(3)],
        out_specs=(pl.BlockSpec((tm, d), lambda i: (i, 0)), pl.BlockSpec((tm, d), lambda i: (i, 0))),
        compiler_params=_cparams(1), name="out_proj_residual",
    )(mrg, w_out, x, mod3, g2.reshape(1, d), mod3, mod3)


def _router_kernel(h_ref, w_ref, b_ref, tri_ref, rank_ref, gate_ref, cnt_ref):
    scores = jax.nn.sigmoid(_dot(h_ref[...], w_ref[...]))
    tm = scores.shape[0]
    lane = lax.broadcasted_iota(jnp.int32, (tm, LANES), 1).astype(F32)
    sel = jnp.where(lane < N_EXPERTS, scores + b_ref[...], -jnp.inf)
    total = jnp.zeros((tm, 1), F32)
    chosen = jnp.zeros((tm, LANES), F32)
    for _ in range(TOP_K):
        mx = sel.max(axis=-1, keepdims=True)
        ix = jnp.where(sel == mx, lane, float(LANES)).min(axis=-1, keepdims=True)
        hit = lane == ix
        total = total + jnp.where(hit, scores, 0.0).sum(axis=-1, keepdims=True)
        sel = jnp.where(hit, -jnp.inf, sel)
        chosen = jnp.where(hit, 1.0, chosen)
    before = _dot(tri_ref[...], chosen.astype(BF16))
    is_chosen = chosen > 0.0
    rank_ref[...] = jnp.where(is_chosen, before, -1.0).T
    gate_ref[...] = (jnp.where(is_chosen, scores, 0.0) / total * ROUTED_SCALE).T
    cnt_ref[...] = chosen.sum(axis=0, keepdims=True)


def _router(h, rw, rb, tm):
    t, d = h.shape
    nt = t // tm
    tri = jnp.asarray(np.tril(np.ones((tm, tm), np.float32), -1), BF16)
    col = pl.BlockSpec((LANES, tm), lambda i: (0, i))
    return pl.pallas_call(
        _router_kernel,
        out_shape=(jax.ShapeDtypeStruct((LANES, t), F32), jax.ShapeDtypeStruct((LANES, t), F32),
                   jax.ShapeDtypeStruct((nt, 1, LANES), F32)),
        grid=(nt,),
        in_specs=[pl.BlockSpec((tm, d), lambda i: (i, 0)),
                  pl.BlockSpec((d, LANES), lambda i: (0, 0)),
                  pl.BlockSpec((1, LANES), lambda i: (0, 0)),
                  pl.BlockSpec((tm, tm), lambda i: (0, 0))],
        out_specs=(col, col, pl.BlockSpec((None, 1, LANES), lambda i: (i, 0, 0))),
        compiler_params=_cparams(1), name="moe_router",
    )(h, rw, rb, tri)


DISPATCH_WIN = 128
GROUP_ALIGN = 16
N_PAIRS = N_EXPERTS // 2
WINDOW_RING = 4


def _selection(rank_row, gate_row, win_off):
    tm = rank_row.shape[1]
    i = lax.broadcasted_iota(jnp.int32, (DISPATCH_WIN, tm), 0).astype(F32) + win_off
    hit = i == jnp.broadcast_to(rank_row, (DISPATCH_WIN, tm))
    if gate_row is None:
        return jnp.where(hit, 1.0, 0.0).astype(BF16)
    return jnp.where(hit, jnp.broadcast_to(gate_row, (DISPATCH_WIN, tm)), 0.0).astype(BF16)


def _dispatch_kernel(base_ref, nchunk_ref, rend_ref, h_ref, rank_ref, xg_ref, obuf, sem):
    j = pl.program_id(0)
    w = DISPATCH_WIN

    def window_copy(slot, half, start):
        return pltpu.make_async_copy(obuf.at[slot, pl.ds(half * w, w)], xg_ref.at[pl.ds(start, w)], sem.at[slot, half])

    @pl.when(j == 0)
    def _zero_region_tails():
        obuf[0] = jnp.zeros(obuf.shape[1:], obuf.dtype)
        n_clear = (MOE_BLOCK + DISPATCH_WIN) // w
        for e in range(N_EXPERTS):
            for i in range(n_clear):
                start = jnp.maximum(rend_ref[e] - (i + 1) * w, 0)
                window_copy(0, i % 2, pl.multiple_of(start, GROUP_ALIGN)).start()
        for e in range(N_EXPERTS):
            for i in range(n_clear):
                window_copy(0, i % 2, 0).wait()

    h = h_ref[...]

    def pair_body(p, carry):
        slot = p % WINDOW_RING

        @pl.when(p >= WINDOW_RING)
        def _():
            window_copy(slot, 0, 0).wait()
            window_copy(slot, 1, 0).wait()

        sel = jnp.concatenate([_selection(rank_ref[pl.ds(2 * p + q, 1), :], None, 0.0) for q in range(2)], axis=0)
        obuf[slot] = _dot(sel, h).astype(obuf.dtype)
        for q in range(2):
            window_copy(slot, q, pl.multiple_of(base_ref[j * N_EXPERTS + 2 * p + q], GROUP_ALIGN)).start()
        return carry

    lax.fori_loop(0, N_PAIRS, pair_body, 0)
    for slot in range(WINDOW_RING):
        for half in range(2):
            window_copy(slot, half, 0).wait()

    def overflow_body(e, carry):
        def chunk_body(c, carry2):
            sel = _selection(rank_ref[pl.ds(e, 1), :], None, (c * w).astype(F32))
            obuf[0, 0:w, :] = _dot(sel, h).astype(obuf.dtype)
            cp = window_copy(0, 0, pl.multiple_of(base_ref[j * N_EXPERTS + e] + c * w, GROUP_ALIGN))
            cp.start()
            cp.wait()
            return carry2

        lax.fori_loop(1, nchunk_ref[j * N_EXPERTS + e], chunk_body, 0)
        return carry

    @pl.when(nchunk_ref[pl.num_programs(0) * N_EXPERTS + j] > 1)
    def _():
        lax.fori_loop(0, N_EXPERTS, overflow_body, 0)


def _dispatch(h, rank_t, base, nchunk, region_end, n_slots, tm):
    t, d = h.shape
    return pl.pallas_call(
        _dispatch_kernel,
        out_shape=jax.ShapeDtypeStruct((n_slots, d), BF16),
        grid_spec=pltpu.PrefetchScalarGridSpec(
            num_scalar_prefetch=3,
            grid=(t // tm,),
            in_specs=[pl.BlockSpec((tm, d), lambda i, *_: (i, 0)),
                      pl.BlockSpec((LANES, tm), lambda i, *_: (0, i))],
            out_specs=pl.BlockSpec(memory_space=pl.ANY),
            scratch_shapes=[pltpu.VMEM((WINDOW_RING, 2 * DISPATCH_WIN, d), BF16), pltpu.SemaphoreType.DMA((WINDOW_RING, 2))]),
        compiler_params=_cparams(1), name="moe_dispatch",
    )(base, nchunk, region_end, h, rank_t)


def _expert_kernel(be_ref, nb_ref, x_ref, wg_ref, wu_ref, wd_ref, o_ref, wg_s, wu_s, wd_s):
    i = pl.program_id(0)
    active = i < nb_ref[0]
    new_expert = (i == 0) | (be_ref[i] != be_ref[jnp.maximum(i - 1, 0)])

    @pl.when(active & new_expert)
    def _():
        wg_s[...] = wg_ref[...].astype(BF16)
        wu_s[...] = wu_ref[...].astype(BF16)
        wd_s[...] = wd_ref[...].astype(BF16)

    @pl.when(active)
    def _():
        x = x_ref[...]
        a = _dot(x, wg_s[...])
        a = a * jax.nn.sigmoid(a) * _dot(x, wu_s[...])
        o_ref[...] = _dot(a.astype(BF16), wd_s[...]).astype(o_ref.dtype)

    @pl.when(jnp.logical_not(active))
    def _():
        o_ref[...] = jnp.zeros(o_ref.shape, o_ref.dtype)


def _expert_ffn(xg, block_exp, n_used, wg, wu, wd, tb):
    n, d = xg.shape
    de = wg.shape[2]
    return pl.pallas_call(
        _expert_kernel,
        out_shape=jax.ShapeDtypeStruct((n, d), BF16),
        grid_spec=pltpu.PrefetchScalarGridSpec(
            num_scalar_prefetch=2,
            grid=(n // tb,),
            in_specs=[pl.BlockSpec((tb, d), lambda i, be, nb: (jnp.minimum(i, nb[0] - 1), 0)),
                      pl.BlockSpec((None, d, de), lambda i, be, nb: (be[i], 0, 0)),
                      pl.BlockSpec((None, d, de), lambda i, be, nb: (be[i], 0, 0)),
                      pl.BlockSpec((None, de, d), lambda i, be, nb: (be[i], 0, 0))],
            out_specs=pl.BlockSpec((tb, d), lambda i, be, nb: (i, 0)),
            scratch_shapes=[pltpu.VMEM((d, de), BF16), pltpu.VMEM((d, de), BF16), pltpu.VMEM((de, d), BF16)]),
        compiler_params=_cparams(1), name="expert_ffn",
    )(block_exp, n_used, xg, wg, wu, wd)


def _moe_combine_kernel(base_ref, nchunk_ref, x_ref, mgate_ref, s_ref, rank_ref, gate_ref, y_ref, o_ref,
                        ybuf, acc_ref, sem, *, tile_off, n_tiles):
    j = pl.program_id(0) + tile_off
    w = DISPATCH_WIN

    def window_copy(slot, half, start):
        return pltpu.make_async_copy(y_ref.at[pl.ds(start, w)], ybuf.at[slot, pl.ds(half * w, w)], sem.at[slot, half])

    def start_pair(p, slot):
        for q in range(2):
            window_copy(slot, q, pl.multiple_of(base_ref[j * N_EXPERTS + 2 * p + q], GROUP_ALIGN)).start()

    ahead = WINDOW_RING - 1
    for p0 in range(ahead):
        start_pair(p0, p0)
    acc_ref[...] = s_ref[...].astype(F32)

    def pair_body(p, carry):
        slot = p % WINDOW_RING
        window_copy(slot, 0, 0).wait()
        window_copy(slot, 1, 0).wait()

        @pl.when(p + ahead < N_PAIRS)
        def _():
            start_pair(p + ahead, (p + ahead) % WINDOW_RING)

        sel = jnp.concatenate([_selection(rank_ref[pl.ds(2 * p + q, 1), :], gate_ref[pl.ds(2 * p + q, 1), :], 0.0)
                               for q in range(2)], axis=0)
        acc_ref[...] += _dot_tn(sel, ybuf[slot])
        return carry

    lax.fori_loop(0, N_PAIRS, pair_body, 0)

    def overflow_body(e, carry):
        def chunk_body(c, carry2):
            cp = window_copy(0, 0, pl.multiple_of(base_ref[j * N_EXPERTS + e] + c * w, GROUP_ALIGN))
            cp.start()
            cp.wait()
            sel = _selection(rank_ref[pl.ds(e, 1), :], gate_ref[pl.ds(e, 1), :], (c * w).astype(F32))
            acc_ref[...] += _dot_tn(sel, ybuf[0, 0:w, :])
            return carry2

        lax.fori_loop(1, nchunk_ref[j * N_EXPERTS + e], chunk_body, 0)
        return carry

    @pl.when(nchunk_ref[n_tiles * N_EXPERTS + j] > 1)
    def _():
        lax.fori_loop(0, N_EXPERTS, overflow_body, 0)

    o_ref[...] = x_ref[...] + mgate_ref[...] * acc_ref[...]


def _moe_combine(x, mod3, mod_row, y, rank_t, gate_t, shared, base, nchunk, tile_off, tm):
    m, d = x.shape
    return pl.pallas_call(
        functools.partial(_moe_combine_kernel, tile_off=tile_off, n_tiles=rank_t.shape[1] // tm),
        out_shape=jax.ShapeDtypeStruct((m, d), F32),
        grid_spec=pltpu.PrefetchScalarGridSpec(
            num_scalar_prefetch=2,
            grid=(m // tm,),
            in_specs=[pl.BlockSpec((tm, d), lambda i, *_: (i, 0)),
                      pl.BlockSpec((None, 1, d), lambda i, *_: (mod_row(i) * 6 + 5, 0, 0)),
                      pl.BlockSpec((tm, d), lambda i, *_: (i + tile_off, 0)),
                      pl.BlockSpec((LANES, tm), lambda i, *_: (0, i + tile_off)),
                      pl.BlockSpec((LANES, tm), lambda i, *_: (0, i + tile_off)),
                      pl.BlockSpec(memory_space=pl.ANY)],
            out_specs=pl.BlockSpec((tm, d), lambda i, *_: (i, 0)),
            scratch_shapes=[pltpu.VMEM((WINDOW_RING, 2 * DISPATCH_WIN, d), BF16), pltpu.VMEM((tm, d), F32),
                            pltpu.SemaphoreType.DMA((WINDOW_RING, 2))]),
        compiler_params=_cparams(1), name="moe_combine",
    )(base, nchunk, x, mod3, shared, rank_t, gate_t, y)


ROUTER_TILE = 512


def _moe(h, rw, rb, wg, wu, wd, sg, su, sd, layer):
    t, d = h.shape
    nt = t // ROUTER_TILE
    rank_t, gate_t, cnt = _router(h, rw, rb, ROUTER_TILE)
    counts = cnt[:, 0, :N_EXPERTS].astype(jnp.int32)
    aligned = (counts + GROUP_ALIGN - 1) // GROUP_ALIGN * GROUP_ALIGN
    region = (aligned.sum(axis=0) + DISPATCH_WIN + MOE_BLOCK - 1) // MOE_BLOCK * MOE_BLOCK
    region_end = jnp.cumsum(region)
    base = ((region_end - region)[None, :] + jnp.cumsum(aligned, axis=0) - aligned).reshape(-1)
    nchunk = (counts + DISPATCH_WIN - 1) // DISPATCH_WIN
    nchunk = jnp.concatenate([nchunk.reshape(-1), nchunk.max(axis=1)])
    n_blocks = -(-(t * TOP_K + (GROUP_ALIGN - 1) * nt * N_EXPERTS + N_EXPERTS * (DISPATCH_WIN + MOE_BLOCK - 1))
                 // MOE_BLOCK)
    block_start = jnp.arange(n_blocks, dtype=jnp.int32) * MOE_BLOCK
    block_exp = jnp.minimum((region_end[None, :] <= block_start[:, None]).sum(axis=1), N_EXPERTS - 1)
    n_used = (region_end[N_EXPERTS - 1] // MOE_BLOCK).astype(jnp.int32).reshape(1)
    xg = _dispatch(h, rank_t, base, nchunk, region_end.astype(jnp.int32), n_blocks * MOE_BLOCK, ROUTER_TILE)
    y = _expert_ffn(xg, block_exp.astype(jnp.int32) + layer * N_EXPERTS, n_used, wg, wu, wd, MOE_BLOCK)
    shared = _expert_ffn(h, jnp.full((t // MOE_BLOCK,), layer, jnp.int32), jnp.full((1,), t // MOE_BLOCK, jnp.int32),
                         sg, su, sd, MOE_BLOCK)
    return y, shared, (rank_t, gate_t, base, nchunk)


def _rope_tables(seq_len, gain):
    nf = MLA_ROPE // 4
    inv = ROPE_THETA ** (-jnp.arange(nf, dtype=F32) / nf)
    pos = jnp.arange(seq_len, dtype=jnp.int32)
    ang_r = (pos // GRID_W).astype(F32)[:, None] * inv
    ang_c = (pos % GRID_W).astype(F32)[:, None] * inv
    cos = jnp.concatenate([jnp.cos(ang_r)] * 2 + [jnp.cos(ang_c)] * 2, axis=-1)
    sin = jnp.concatenate([-jnp.sin(ang_r), jnp.sin(ang_r), -jnp.sin(ang_c), jnp.sin(ang_c)], axis=-1)
    g = gain.astype(F32)
    cg = cos * g[None]
    sg = sin * g[_ROPE_PARTNER][None]
    return jnp.tile(cg, (1, 2)), jnp.tile(sg, (1, 2))


def _rope_partner():
    nf = MLA_ROPE // 4
    idx = np.arange(MLA_ROPE)
    within = idx % (2 * nf)
    return np.where(within < nf, idx + nf, idx - nf)


_ROPE_PARTNER = _rope_partner()


def _in_splits():
    sizes = (NA_W, NA_W, NA_W, MLA_Q_LORA, MLA_KV_LORA, MLA_ROPE, CONV_CH, CONV_CH)
    offs = np.concatenate([[0], np.cumsum(sizes)])
    return {n: int(o) for n, o in zip(("q", "k", "v", "cq", "ckv", "kr", "a", "b", "g"), offs)}


def _prep_layer(l, dmodel, w_in, na_q_g, na_k_g, na_rpb, w_na_o, mla_q_lora_g, mla_w_uq, mla_kv_lora_g, mla_w_ukv,
                mla_q_g, mla_k_g, w_mla_o, conv_w_dw, conv_b_dw, conv_ln_g, conv_ln_b, w_conv_o, w_out,
                router_w, router_b, exp_w_gate, exp_w_up, exp_w_down, sh_w_gate, sh_w_up, sh_w_down, seq_len):
    o = _in_splits()
    w = w_in[l]
    cols = lambda a, n: w[:, a:a + n]
    p = {}
    p["w_p1"] = jnp.concatenate([cols(o["q"], 3 * NA_W), cols(o["a"], 2 * CONV_CH), cols(o["g"], 3 * dmodel)],
                                axis=1).astype(BF16)
    kr = cols(o["kr"], MLA_ROPE)
    krs = kr[:, _ROPE_PARTNER]
    p["w_p2"] = jnp.concatenate([cols(o["cq"], MLA_Q_LORA), kr, kr, krs, krs, cols(o["ckv"], MLA_KV_LORA)],
                                axis=1).astype(BF16)
    qscale = NA_HEAD_DIM ** -0.5 * LOG2E
    p["na_g"] = jnp.concatenate([jnp.tile(na_q_g[l].astype(F32) * qscale, NA_HEADS),
                                 jnp.tile(na_k_g[l].astype(F32), NA_HEADS)]).reshape(1, 2 * NA_W)
    p["na_bias"] = _na_bias_table(na_rpb[l])
    wq = mla_w_uq[l].reshape(MLA_Q_LORA, MLA_HEADS, MLA_QK)
    wq_rope = wq[:, :, MLA_NOPE:]
    flat = lambda a: a.reshape(a.shape[0], -1)
    p["w_uq"] = jnp.concatenate([flat(wq[:, :, :MLA_NOPE]), flat(wq_rope), flat(wq_rope[:, :, _ROPE_PARTNER])],
                                axis=1).astype(BF16)
    wkv = mla_w_ukv[l].reshape(MLA_KV_LORA, MLA_HEADS, MLA_NOPE + MLA_V)
    p["w_ukv"] = jnp.concatenate([flat(wkv[:, :, :MLA_NOPE]), flat(wkv[:, :, MLA_NOPE:])], axis=1).astype(BF16)
    p["q_lora_g"] = mla_q_lora_g[l].astype(F32).reshape(1, -1)
    p["kv_lora_g"] = mla_kv_lora_g[l].astype(F32).reshape(1, -1)
    mscale = MLA_QK ** -0.5 * LOG2E
    p["q_gn"] = (mla_q_g[l][:MLA_NOPE].astype(F32) * mscale).reshape(1, -1)
    p["k_gn"] = mla_k_g[l][:MLA_NOPE].astype(F32).reshape(1, -1)
    cgq, sgq = _rope_tables(seq_len, mla_q_g[l][MLA_NOPE:] * mscale)
    cgk, sgk = _rope_tables(seq_len, mla_k_g[l][MLA_NOPE:])
    p["q_tabs"] = (cgq, sgq)
    p["k_tabs"] = (cgk, sgk)
    p["q_tabs_c"] = (jnp.tile((mla_q_g[l][MLA_NOPE:].astype(F32) * mscale)[None], (1, 2)), jnp.zeros((1, LANES), F32))
    p["k_tabs_c"] = (jnp.tile(mla_k_g[l][MLA_NOPE:].astype(F32)[None], (1, 2)), jnp.zeros((1, LANES), F32))
    p["w_na_o"] = w_na_o[l].astype(BF16)
    p["w_mla_o"] = w_mla_o[l].astype(BF16)
    p["w_conv_o"] = w_conv_o[l].astype(BF16)
    p["w_out"] = w_out[l].astype(BF16)
    p["conv_w"] = jnp.concatenate([conv_w_dw[l].astype(F32), jnp.zeros((1, CONV_CH), F32)], axis=0)
    p["conv_b"] = conv_b_dw[l].astype(F32).reshape(1, -1)
    p["ln_g"] = conv_ln_g[l].astype(F32).reshape(1, -1)
    p["ln_b"] = conv_ln_b[l].astype(F32).reshape(1, -1)
    p["router_w"] = jnp.concatenate([router_w[l], jnp.zeros((dmodel, LANES - N_EXPERTS), router_w.dtype)],
                                    axis=1).astype(BF16)
    p["router_b"] = jnp.concatenate([router_b[l].astype(F32), jnp.zeros((LANES - N_EXPERTS,), F32)]).reshape(1, LANES)
    flat_e = lambda a: a.reshape((-1,) + a.shape[2:])
    p["exp"] = (flat_e(exp_w_gate), flat_e(exp_w_up), flat_e(exp_w_down))
    p["sh"] = (sh_w_gate, sh_w_up, sh_w_down)
    return p


def _group_matrices():
    i = np.arange(LANES)
    bd = (i[:, None] // NA_HEAD_DIM == i[None, :] // NA_HEAD_DIM)
    ones = np.ones((LANES, LANES), bool)
    e0 = np.broadcast_to((i < MLA_ROPE)[:, None], (LANES, LANES))
    e1 = np.broadcast_to((i >= MLA_ROPE)[:, None], (LANES, LANES))
    f = lambda m: jnp.asarray(m, BF16)
    return f(bd), (f(ones), f(e0), f(e1))


def _expand_tabs(tabs, tm):
    return tuple(jnp.broadcast_to(t, (tm, LANES)) if t.shape[0] == 1 else t for t in tabs)


def _mixer_inputs(h, p, mats, bd, seq, ctx):
    tm_mm = min(1024, h.shape[0])
    p1 =_matmul(h, p["w_p1"], BF16, tm_mm, 1024, "in_proj_1")
    p2 = _matmul(h, p["w_p2"], BF16, tm_mm, P2_W // 2, "in_proj_2")
    qk = _headnorm(p1, p["na_g"], bd, min(512, h.shape[0]))
    tr = 256
    qt = _expand_tabs(p["q_tabs_c"], tr) if ctx else p["q_tabs"]
    kt = _expand_tabs(p["k_tabs_c"], tr) if ctx else p["k_tabs"]
    tseq = tr if ctx else seq
    km, vm = _mla_kv(p2, p["kv_lora_g"], p["w_ukv"], p["k_gn"], kt[0], kt[1], mats, tseq, tr)
    return p1, p2, qk, km, vm, (qt, tseq, tr)


def kernel(x, c, ctx, c_ctx, ada_w, ada_b, norm1_g, norm2_g, w_in, na_q_g, na_k_g, na_rpb, w_na_o, mla_q_lora_g, mla_w_uq, mla_kv_lora_g, mla_w_ukv, mla_q_g, mla_k_g, w_mla_o, conv_w_dw, conv_b_dw, conv_ln_g, conv_ln_b, w_conv_o, w_out, router_w, router_b, exp_w_gate, exp_w_up, exp_w_down, sh_w_gate, sh_w_up, sh_w_down):
    batch, seq, d = x.shape
    lc = ctx.shape[1]
    depth = ada_w.shape[0]
    mod_rows = (batch + 1 + 7) // 8 * 8
    cc = jnp.concatenate([c, c_ctx[None], jnp.zeros((mod_rows - batch - 1, d), c.dtype)], axis=0)
    mod_all = _ada(cc, ada_w, ada_b)
    bd, mats = _group_matrices()
    xs = x.reshape(batch * seq, d)
    zs = ctx.reshape(batch * lc, d)
    tm_x = 512
    tm_c = 256
    x_row = lambda i: (i * tm_x) // seq
    c_row = lambda i: batch

    for l in range(depth):
        need_ctx = l < depth - 1
        p = _prep_layer(l, d, w_in, na_q_g, na_k_g, na_rpb, w_na_o, mla_q_lora_g, mla_w_uq, mla_kv_lora_g, mla_w_ukv,
                        mla_q_g, mla_k_g, w_mla_o, conv_w_dw, conv_b_dw, conv_ln_g, conv_ln_b, w_conv_o, w_out,
                        router_w, router_b, exp_w_gate, exp_w_up, exp_w_down, sh_w_gate, sh_w_up, sh_w_down, seq)
        mod3 = mod_all[l].reshape(mod_rows * 6, 1, d)
        hx = _norm_mod(xs, norm1_g[l], mod3, x_row, 1, 0, tm_x)
        hc = _norm_mod(zs, norm1_g[l], mod3, c_row, 1, 0, tm_c)
        p1x, p2x, qkx, kmx, vmx, (qtx, tsx, trx) = _mixer_inputs(hx, p, mats, bd, seq, False)
        p1c, p2c, qkc, kmc, vmc, (qtc, tsc, trc) = _mixer_inputs(hc, p, mats, bd, lc, True)
        o_na = _na_attention(qkx, p1x, qkc, p1c, p["na_bias"], batch)
        qmx = _mla_q(p2x, p["q_lora_g"], p["w_uq"], p["q_gn"], qtx[0], qtx[1], mats, tsx, trx)
        o_mla = _mla_attention(qmx, [(kmc, vmc), (kmx, vmx)], batch, 512)
        u = _conv_module(p1x, p["conv_w"], p["conv_b"], p["ln_g"], p["ln_b"], seq, 256)
        mrg = _merge(o_na, o_mla, u, p1x, p["w_na_o"], p["w_mla_o"], p["w_conv_o"], 1024, 512)
        xs, hx2 = _out_proj(mrg, p["w_out"], xs, mod3, x_row, norm2_g[l], tm_x)
        if need_ctx:
            o_na_c = _na_ctx_attention(qkc, p1c, batch)
            qmc = _mla_q(p2c, p["q_lora_g"], p["w_uq"], p["q_gn"], qtc[0], qtc[1], mats, tsc, trc)
            o_mla_c = _mla_attention(qmc, [(kmc, vmc)], batch, lc)
            u_c = _conv_module(p1c, p["conv_w"], p["conv_b"], p["ln_g"], p["ln_b"], lc, lc)
            mrg_c = _merge(o_na_c, o_mla_c, u_c, p1c, p["w_na_o"], p["w_mla_o"], p["w_conv_o"], 1024, 512)
            zs, hc2 = _out_proj(mrg_c, p["w_out"], zs, mod3, c_row, norm2_g[l], tm_c)
            tokens = jnp.concatenate([hc2, hx2], axis=0)
        else:
            tokens = hx2
        y, shared, (rank_t, gate_t, base, nchunk) = _moe(tokens, p["router_w"], p["router_b"], *p["exp"], *p["sh"], l)
        tm_cmb = ROUTER_TILE
        x_row_cmb = lambda i: (i * tm_cmb) // seq
        if need_ctx:
            zs = _moe_combine(zs, mod3, c_row, y, rank_t, gate_t, shared, base, nchunk, 0, tm_cmb)
            xs = _moe_combine(xs, mod3, x_row_cmb, y, rank_t, gate_t, shared, base, nchunk, batch * lc // tm_cmb,
                              tm_cmb)
        else:
            xs = _moe_combine(xs, mod3, x_row_cmb, y, rank_t, gate_t, shared, base, nchunk, 0, tm_cmb)
    return xs.reshape(batch, seq, d)
```

```python
import functools

import numpy as np
import jax
import jax.numpy as jnp
from jax import lax
from jax.experimental import pallas as pl
from jax.experimental.pallas import tpu as pltpu

F32 = jnp.float32
BF16 = jnp.bfloat16

GRID_W = 64
EPS = 1e-6
NEG_INF = -1e30
NA_HEADS = 16
NA_HEAD_DIM = 64
NA_W = NA_HEADS * NA_HEAD_DIM
NA_WIN_ROWS = 8
NA_WIN_COLS = 16
MLA_HEADS = 16
MLA_NOPE = 128
MLA_ROPE = 64
MLA_QK = MLA_NOPE + MLA_ROPE
MLA_V = 128
MLA_Q_LORA = 768
MLA_KV_LORA = 512
ROPE_THETA = 10000.0
CONV_CH = 1024
CONV_WIDTH = 31
CONV_HALO = 16
CONV_ACC_ROWS = 128
N_EXPERTS = 64
TOP_K = 8
D_EXPERT = 512
ROUTED_SCALE = 2.5
MOE_BLOCK = 256
LANES = 128
SUBLANES = 8
MLA_HEAD_PAD = 2 * LANES
VMEM_LIMIT = 56 * 1024 * 1024

P1_Q, P1_K, P1_V, P1_A, P1_B, P1_GNA, P1_GMLA, P1_GCV = 0, 1024, 2048, 3072, 4096, 5120, 7168, 9216
P1_W = 11264
P2_CQ, P2_KR, P2_KRS, P2_CKV = 0, 768, 896, 1024
P2_W = 1536


def _cparams(n_axes):
    return pltpu.CompilerParams(dimension_semantics=("arbitrary",) * n_axes, vmem_limit_bytes=VMEM_LIMIT)


def _dot(a, b):
    return jnp.dot(a, b, preferred_element_type=F32)


def _dot_nt(a, b):
    return lax.dot_general(a, b, (((1,), (1,)), ((), ())), preferred_element_type=F32)


def _dot_hilo(x, m):
    hi = x.astype(BF16)
    lo = (x - hi.astype(F32)).astype(BF16)
    return _dot(hi, m) + _dot(lo, m)


def _ada_kernel(c_ref, w_ref, b_ref, o_ref):
    c = c_ref[...]
    a = (c * jax.nn.sigmoid(c)).astype(BF16)
    o_ref[...] = _dot(a, w_ref[...].astype(BF16)) + b_ref[...]


def _ada(cc, ada_w, ada_b):
    nl, d, n = ada_w.shape
    r = cc.shape[0]
    tn = 1024
    return pl.pallas_call(
        _ada_kernel,
        out_shape=jax.ShapeDtypeStruct((nl, r, n), F32),
        grid=(nl, n // tn),
        in_specs=[pl.BlockSpec((r, d), lambda l, j: (0, 0)),
                  pl.BlockSpec((None, d, tn), lambda l, j: (l, 0, j)),
                  pl.BlockSpec((None, 1, tn), lambda l, j: (l, 0, j))],
        out_specs=pl.BlockSpec((None, r, tn), lambda l, j: (l, 0, j)),
        compiler_params=_cparams(2), name="ada_mod",
    )(cc, ada_w, ada_b.reshape(nl, 1, n))


def _norm_mod_kernel(x_ref, g_ref, sc_ref, sh_ref, o_ref):
    x = x_ref[...]
    y = x * lax.rsqrt(jnp.mean(x * x, axis=-1, keepdims=True) + EPS) * g_ref[...]
    o_ref[...] = (y * (1.0 + sc_ref[...]) + sh_ref[...]).astype(o_ref.dtype)


def _norm_mod(x, g, mod3, mod_row, k_sc, k_sh, tm):
    m, d = x.shape
    return pl.pallas_call(
        _norm_mod_kernel,
        out_shape=jax.ShapeDtypeStruct((m, d), BF16),
        grid=(m // tm,),
        in_specs=[pl.BlockSpec((tm, d), lambda i: (i, 0)),
                  pl.BlockSpec((1, d), lambda i: (0, 0)),
                  pl.BlockSpec((None, 1, d), lambda i: (mod_row(i) * 6 + k_sc, 0, 0)),
                  pl.BlockSpec((None, 1, d), lambda i: (mod_row(i) * 6 + k_sh, 0, 0))],
        out_specs=pl.BlockSpec((tm, d), lambda i: (i, 0)),
        compiler_params=_cparams(1), name="norm_mod",
    )(x, g.reshape(1, d), mod3, mod3)


def _mm_kernel(a_ref, w_ref, o_ref):
    o_ref[...] = _dot(a_ref[...], w_ref[...]).astype(o_ref.dtype)


def _matmul(a, w, out_dtype, tm, tn, name):
    m, k = a.shape
    n = w.shape[1]
    return pl.pallas_call(
        _mm_kernel,
        out_shape=jax.ShapeDtypeStruct((m, n), out_dtype),
        grid=(m // tm, n // tn),
        in_specs=[pl.BlockSpec((tm, k), lambda i, j: (i, 0)),
                  pl.BlockSpec((k, tn), lambda i, j: (0, j))],
        out_specs=pl.BlockSpec((tm, tn), lambda i, j: (i, j)),
        compiler_params=_cparams(2), name=name,
    )(a, w)


def _headnorm_kernel(x_ref, g_ref, bd_ref, o_ref):
    bd = bd_ref[...]
    for c in range(x_ref.shape[1] // LANES):
        sl = slice(c * LANES, (c + 1) * LANES)
        x = x_ref[:, sl].astype(F32)
        ss = _dot_hilo(x * x, bd)
        y = x * lax.rsqrt(ss * (1.0 / NA_HEAD_DIM) + EPS) * g_ref[:, sl]
        o_ref[:, sl] = y.astype(o_ref.dtype)


def _headnorm(p1, g_row, bd, tm):
    m = p1.shape[0]
    w = g_row.shape[1]
    return pl.pallas_call(
        _headnorm_kernel,
        out_shape=jax.ShapeDtypeStruct((m, w), BF16),
        grid=(m // tm,),
        in_specs=[pl.BlockSpec((tm, w), lambda i: (i, 0)),
                  pl.BlockSpec((1, w), lambda i: (0, 0)),
                  pl.BlockSpec((LANES, LANES), lambda i: (0, 0))],
        out_specs=pl.BlockSpec((tm, w), lambda i: (i, 0)),
        compiler_params=_cparams(1), name="na_headnorm",
    )(p1, g_row, bd)


LOG2E = 1.4426950408889634


def _softmax_pv(scores, values):
    m = scores[0].max(axis=-1, keepdims=True)
    for s in scores[1:]:
        m = jnp.maximum(m, s.max(axis=-1, keepdims=True))
    l = None
    o = None
    for s, v in zip(scores, values):
        p = jnp.exp2(s - m)
        ps = p.sum(axis=-1, keepdims=True)
        po = _dot(p.astype(BF16), v)
        l = ps if l is None else l + ps
        o = po if o is None else o + po
    return o / l


def _dot_tn(a, b):
    return lax.dot_general(a, b, (((0,), (0,)), ((), ())), preferred_element_type=F32)


def _na_kernel(q_ref, kx_ref, vx_ref, kc_ref, vc_ref, bias_ref, o_ref):
    rows = kx_ref.shape[0] // GRID_W
    r = pl.program_id(1)
    r0 = jnp.clip(r - NA_WIN_ROWS // 2, 0, rows - NA_WIN_ROWS)
    kstart = pl.multiple_of(r0 * GRID_W, GRID_W)
    nwin = NA_WIN_ROWS * GRID_W
    lo = lax.broadcasted_iota(jnp.int32, (GRID_W, LANES), 1) < NA_HEAD_DIM
    zero = jnp.zeros((GRID_W, LANES), q_ref.dtype)
    n_pairs = NA_W // LANES
    pair = lambda p: slice(p * LANES, (p + 1) * LANES)
    scores = []
    for p in range(n_pairs):
        q2 = q_ref[:, pair(p)]
        qs = jnp.concatenate([jnp.where(lo, q2, zero), jnp.where(lo, zero, q2)], axis=0)
        s_w = _dot_nt(kx_ref[pl.ds(kstart, nwin), pair(p)], qs) + bias_ref[p]
        s_c = _dot_nt(kc_ref[:, pair(p)], qs)
        scores.append((s_w, s_c))
    probs = []
    for s_w, s_c in scores:
        m = jnp.maximum(s_w.max(axis=0, keepdims=True), s_c.max(axis=0, keepdims=True))
        p_w = jnp.exp2(s_w - m)
        p_c = jnp.exp2(s_c - m)
        inv = 1.0 / (p_w.sum(axis=0, keepdims=True) + p_c.sum(axis=0, keepdims=True))
        probs.append(((p_w * inv).astype(BF16), (p_c * inv).astype(BF16)))
    outs = []
    for p, (p_w, p_c) in enumerate(probs):
        o2 = _dot_tn(p_w, vx_ref[pl.ds(kstart, nwin), pair(p)]) + _dot_tn(p_c, vc_ref[:, pair(p)])
        outs.append(jnp.where(lo, o2[:GRID_W], o2[GRID_W:]).astype(o_ref.dtype))
    o_ref[...] = jnp.concatenate(outs, axis=1)


def _na_attention(qk_x, p1_x, qk_c, p1_c, bias_tab, batch):
    l = qk_x.shape[0] // batch
    lc = qk_c.shape[0] // batch
    rows = l // GRID_W
    nwin = NA_WIN_ROWS * GRID_W

    def bias_idx(b, r):
        r0 = jnp.clip(r - NA_WIN_ROWS // 2, 0, rows - NA_WIN_ROWS)
        return (r - r0, 0, 0, 0)

    return pl.pallas_call(
        _na_kernel,
        out_shape=jax.ShapeDtypeStruct((batch * l, NA_W), BF16),
        grid=(batch, rows),
        in_specs=[pl.BlockSpec((GRID_W, NA_W), lambda b, r: (b * rows + r, 0)),
                  pl.BlockSpec((l, NA_W), lambda b, r: (b, 1)),
                  pl.BlockSpec((l, NA_W), lambda b, r: (b, P1_V // NA_W)),
                  pl.BlockSpec((lc, NA_W), lambda b, r: (b, 1)),
                  pl.BlockSpec((lc, NA_W), lambda b, r: (b, P1_V // NA_W)),
                  pl.BlockSpec((None, NA_HEADS // 2, nwin, LANES), bias_idx)],
        out_specs=pl.BlockSpec((GRID_W, NA_W), lambda b, r: (b * rows + r, 0)),
        compiler_params=_cparams(2), name="na_attention",
    )(qk_x, qk_x, p1_x, qk_c, p1_c, bias_tab)


def _na_ctx_kernel(q_ref, k_ref, v_ref, o_ref):
    n = q_ref.shape[0]
    lo = lax.broadcasted_iota(jnp.int32, (n, LANES), 1) < NA_HEAD_DIM
    zero = jnp.zeros((n, LANES), q_ref.dtype)
    for p in range(NA_W // LANES):
        sl = slice(p * LANES, (p + 1) * LANES)
        q2 = q_ref[:, sl]
        k = k_ref[:, sl]
        v = v_ref[:, sl]
        outs = []
        for hh in range(2):
            qm = jnp.where(lo, q2, zero) if hh == 0 else jnp.where(lo, zero, q2)
            outs.append(_softmax_pv([_dot_nt(qm, k)], [v]))
        o_ref[:, sl] = jnp.where(lo, outs[0], outs[1]).astype(o_ref.dtype)


def _na_ctx_attention(qk_c, p1_c, batch):
    lc = qk_c.shape[0] // batch
    return pl.pallas_call(
        _na_ctx_kernel,
        out_shape=jax.ShapeDtypeStruct((batch * lc, NA_W), BF16),
        grid=(batch,),
        in_specs=[pl.BlockSpec((lc, NA_W), lambda b: (b, 0)),
                  pl.BlockSpec((lc, NA_W), lambda b: (b, 1)),
                  pl.BlockSpec((lc, NA_W), lambda b: (b, P1_V // NA_W))],
        out_specs=pl.BlockSpec((lc, NA_W), lambda b: (b, 0)),
        compiler_params=_cparams(1), name="na_ctx_attention",
    )(qk_c, qk_c, p1_c)


def _na_bias_table(rpb):
    cls = np.arange(NA_WIN_ROWS)[:, None]
    w = np.arange(NA_WIN_ROWS)[None, :]
    qc = np.arange(GRID_W)[:, None]
    kc = np.arange(GRID_W)[None, :]
    c0 = np.clip(qc - NA_WIN_COLS // 2, 0, GRID_W - NA_WIN_COLS)
    ok = (kc >= c0) & (kc < c0 + NA_WIN_COLS)
    dr = w - cls + NA_WIN_ROWS - 1
    dc = np.clip(kc - qc + NA_WIN_COLS - 1, 0, 2 * NA_WIN_COLS - 2)
    row_sel = jnp.asarray(dr[:, :, None] == np.arange(2 * NA_WIN_ROWS - 1), F32)
    col_sel = jnp.asarray(dc[:, :, None] == np.arange(2 * NA_WIN_COLS - 1), F32)
    exact = lax.Precision.HIGHEST
    t1 = jnp.einsum('hab,cwa->hcwb', rpb.astype(F32) * LOG2E, row_sel, precision=exact)
    tab = jnp.einsum('hcwb,qkb->chwkq', t1, col_sel, precision=exact)
    tab = jnp.where(jnp.asarray(ok.T)[None, None, None], tab, NEG_INF)
    tab = tab.reshape(NA_WIN_ROWS, NA_HEADS // 2, 2, NA_WIN_ROWS * GRID_W, GRID_W)
    tab = jnp.transpose(tab, (0, 1, 3, 2, 4))
    return tab.reshape(NA_WIN_ROWS, NA_HEADS // 2, NA_WIN_ROWS * GRID_W, LANES)


def _mla_q_kernel(p2_ref, gl_ref, w_ref, gn_ref, cg_ref, sg_ref, ones_ref, e0_ref, e1_ref, o_ref):
    cq = p2_ref[...].astype(F32)
    cqn = cq * lax.rsqrt(jnp.mean(cq * cq, axis=-1, keepdims=True) + EPS) * gl_ref[...]
    y = _dot(cqn.astype(BF16), w_ref[...])
    tm = y.shape[0]
    lo = lax.broadcasted_iota(jnp.int32, (tm, LANES), 1) < MLA_ROPE
    nope_w = MLA_HEADS * MLA_NOPE
    rope_w = MLA_HEADS * MLA_ROPE
    ones, e0, e1 = ones_ref[...], e0_ref[...], e1_ref[...]
    gn = gn_ref[...]
    for p in range(MLA_HEADS // 2):
        yr = y[:, nope_w + p * LANES: nope_w + (p + 1) * LANES]
        yrs = y[:, nope_w + rope_w + p * LANES: nope_w + rope_w + (p + 1) * LANES]
        rot = yr * cg_ref[...] + yrs * sg_ref[...]
        yr2 = yr * yr
        ssr = (_dot_hilo(yr2, e0), _dot_hilo(yr2, e1))
        for hh in range(2):
            h = 2 * p + hh
            nope = y[:, h * MLA_NOPE:(h + 1) * MLA_NOPE]
            tot = _dot_hilo(nope * nope, ones) + ssr[hh]
            scale = lax.rsqrt(tot * (1.0 / MLA_QK) + EPS)
            o_ref[:, h * MLA_HEAD_PAD: h * MLA_HEAD_PAD + LANES] = (nope * scale * gn).astype(o_ref.dtype)
            rs = rot * scale
            rs = jnp.where(lo, rs, 0.0) if hh == 0 else jnp.where(lo, 0.0, rs)
            o_ref[:, h * MLA_HEAD_PAD + LANES: (h + 1) * MLA_HEAD_PAD] = rs.astype(o_ref.dtype)


def _mla_q(p2, g_lora, w_uq, g_nope, cg, sg, mats, seq, tm):
    m = p2.shape[0]
    nt = seq // tm
    ones, e0, e1 = mats
    full = lambda shape: pl.BlockSpec(shape, lambda i: (0,) * len(shape))
    return pl.pallas_call(
        _mla_q_kernel,
        out_shape=jax.ShapeDtypeStruct((m, MLA_HEADS * MLA_HEAD_PAD), BF16),
        grid=(m // tm,),
        in_specs=[pl.BlockSpec((tm, MLA_Q_LORA), lambda i: (i, 0)),
                  full((1, MLA_Q_LORA)),
                  full(w_uq.shape),
                  full((1, LANES)),
                  pl.BlockSpec((tm, LANES), lambda i: (i % nt, 0)),
                  pl.BlockSpec((tm, LANES), lambda i: (i % nt, 0)),
                  full((LANES, LANES)), full((LANES, LANES)), full((LANES, LANES))],
        out_specs=pl.BlockSpec((tm, MLA_HEADS * MLA_HEAD_PAD), lambda i: (i, 0)),
        compiler_params=_cparams(1), name="mla_q_proj",
    )(p2, g_lora, w_uq, g_nope, cg, sg, ones, e0, e1)


def _mla_kv_kernel(ckv_ref, kr_ref, krs_ref, gl_ref, w_ref, gn_ref, cg_ref, sg_ref, ones_ref, e0_ref, k_ref, v_ref):
    ckv = ckv_ref[...].astype(F32)
    cn = ckv * lax.rsqrt(jnp.mean(ckv * ckv, axis=-1, keepdims=True) + EPS) * gl_ref[...]
    y = _dot(cn.astype(BF16), w_ref[...])
    tm = y.shape[0]
    lo = lax.broadcasted_iota(jnp.int32, (tm, LANES), 1) < MLA_ROPE
    nope_w = MLA_HEADS * MLA_NOPE
    v_ref[...] = y[:, nope_w:].astype(v_ref.dtype)
    kr = kr_ref[...].astype(F32)
    rot = kr * cg_ref[...] + krs_ref[...].astype(F32) * sg_ref[...]
    ssr = _dot_hilo(kr * kr, e0_ref[...])
    ones = ones_ref[...]
    gn = gn_ref[...]
    for h in range(MLA_HEADS):
        nope = y[:, h * MLA_NOPE:(h + 1) * MLA_NOPE]
        tot = _dot_hilo(nope * nope, ones) + ssr
        scale = lax.rsqrt(tot * (1.0 / MLA_QK) + EPS)
        k_ref[:, h * MLA_HEAD_PAD: h * MLA_HEAD_PAD + LANES] = (nope * scale * gn).astype(k_ref.dtype)
        rs = rot * scale
        rs = jnp.where(lo, rs, 0.0) if h % 2 == 0 else jnp.where(lo, 0.0, rs)
        k_ref[:, h * MLA_HEAD_PAD + LANES: (h + 1) * MLA_HEAD_PAD] = rs.astype(k_ref.dtype)


def _mla_kv(p2, g_lora, w_ukv, g_nope, cg, sg, mats, seq, tm):
    m = p2.shape[0]
    nt = seq // tm
    ones, e0, _ = mats
    full = lambda shape: pl.BlockSpec(shape, lambda i: (0,) * len(shape))
    return pl.pallas_call(
        _mla_kv_kernel,
        out_shape=(jax.ShapeDtypeStruct((m, MLA_HEADS * MLA_HEAD_PAD), BF16),
                   jax.ShapeDtypeStruct((m, MLA_HEADS * MLA_V), BF16)),
        grid=(m // tm,),
        in_specs=[pl.BlockSpec((tm, MLA_KV_LORA), lambda i: (i, P2_CKV // MLA_KV_LORA)),
                  pl.BlockSpec((tm, LANES), lambda i: (i, P2_KR // LANES)),
                  pl.BlockSpec((tm, LANES), lambda i: (i, P2_KRS // LANES)),
                  full((1, MLA_KV_LORA)),
                  full(w_ukv.shape),
                  full((1, LANES)),
                  pl.BlockSpec((tm, LANES), lambda i: (i % nt, 0)),
                  pl.BlockSpec((tm, LANES), lambda i: (i % nt, 0)),
                  full((LANES, LANES)), full((LANES, LANES))],
        out_specs=(pl.BlockSpec((tm, MLA_HEADS * MLA_HEAD_PAD), lambda i: (i, 0)),
                   pl.BlockSpec((tm, MLA_HEADS * MLA_V), lambda i: (i, 0))),
        compiler_params=_cparams(1), name="mla_kv_proj",
    )(p2, p2, p2, g_lora, w_ukv, g_nope, cg, sg, ones, e0)


def _mla_attn_kernel(*refs, n_kv):
    q_ref = refs[0]
    kv = refs[1:1 + 2 * n_kv]
    o_ref = refs[1 + 2 * n_kv]
    heads = range(MLA_HEADS_PER_STEP)
    qsl = lambda hh: slice(hh * MLA_HEAD_PAD, (hh + 1) * MLA_HEAD_PAD)
    vsl = lambda hh: slice(hh * MLA_V, (hh + 1) * MLA_V)
    scores = [[_dot_nt(q_ref[:, qsl(hh)], kv[2 * i][:, qsl(hh)]) for i in range(n_kv)] for hh in heads]
    probs = []
    for s_list in scores:
        m = s_list[0].max(axis=-1, keepdims=True)
        for s in s_list[1:]:
            m = jnp.maximum(m, s.max(axis=-1, keepdims=True))
        p_list = [jnp.exp2(s - m) for s in s_list]
        l = p_list[0].sum(axis=-1, keepdims=True)
        for p in p_list[1:]:
            l = l + p.sum(axis=-1, keepdims=True)
        probs.append(([p.astype(BF16) for p in p_list], l))
    outs = []
    for hh, (p_list, l) in zip(heads, probs):
        o = _dot(p_list[0], kv[1][:, vsl(hh)])
        for i in range(1, n_kv):
            o = o + _dot(p_list[i], kv[2 * i + 1][:, vsl(hh)])
        outs.append((o / l).astype(o_ref.dtype))
    o_ref[...] = jnp.concatenate(outs, axis=1)


MLA_HEADS_PER_STEP = 2


def _mla_attention(q, kvs, batch, tq):
    lq = q.shape[0] // batch
    nq = lq // tq
    hs = MLA_HEADS_PER_STEP
    in_specs = [pl.BlockSpec((tq, hs * MLA_HEAD_PAD), lambda b, h, j: (b * nq + j, h))]
    args = [q]
    for k, v in kvs:
        s = k.shape[0] // batch
        in_specs.append(pl.BlockSpec((s, hs * MLA_HEAD_PAD), lambda b, h, j: (b, h)))
        in_specs.append(pl.BlockSpec((s, hs * MLA_V), lambda b, h, j: (b, h)))
        args += [k, v]
    return pl.pallas_call(
        functools.partial(_mla_attn_kernel, n_kv=len(kvs)),
        out_shape=jax.ShapeDtypeStruct((batch * lq, MLA_HEADS * MLA_V), BF16),
        grid=(batch, MLA_HEADS // hs, nq),
        in_specs=in_specs,
        out_specs=pl.BlockSpec((tq, hs * MLA_V), lambda b, h, j: (b * nq + j, h)),
        compiler_params=_cparams(3), name="mla_attention",
    )(*args)


def _conv_kernel(ap_ref, a_ref, an_ref, bp_ref, b_ref, bn_ref, w_ref, cb_ref, g_ref, beta_ref, o_ref, u_scr, sh_scr,
                 y_scr, *, n_tiles):
    j = pl.program_id(1)
    tt = a_ref.shape[0]
    glu = lambda a, b: a[...].astype(F32) * jax.nn.sigmoid(b[...].astype(F32))
    u_scr[0:CONV_HALO, :] = jnp.where(j > 0, glu(ap_ref, bp_ref), 0.0)
    u_scr[CONV_HALO:CONV_HALO + tt, :] = glu(a_ref, b_ref)
    u_scr[CONV_HALO + tt:, :] = jnp.where(j < n_tiles - 1, glu(an_ref, bn_ref), 0.0)
    n_sh = sh_scr.shape[1]
    for b in range(SUBLANES):
        sh_scr[b] = u_scr[b:b + n_sh, :]
    off = CONV_HALO - CONV_WIDTH // 2
    rows = min(tt, CONV_ACC_ROWS)
    for c in range(CONV_CH // LANES):
        sl = slice(c * LANES, (c + 1) * LANES)
        for r0 in range(0, tt, rows):
            acc = jnp.zeros((rows, LANES), F32)
            for k in range(CONV_WIDTH):
                a8, b = (off + k) // SUBLANES * SUBLANES + r0, (off + k) % SUBLANES
                acc = acc + sh_scr[b, a8:a8 + rows, sl] * w_ref[k:k + 1, sl]
            y_scr[r0:r0 + rows, sl] = acc + cb_ref[:, sl]
    y = y_scr[...]
    mu = jnp.mean(y, axis=-1, keepdims=True)
    d = y - mu
    var = jnp.mean(d * d, axis=-1, keepdims=True)
    z = d * lax.rsqrt(var + EPS) * g_ref[...] + beta_ref[...]
    o_ref[...] = (z * jax.nn.sigmoid(z)).astype(o_ref.dtype)


def _conv_module(p1, w_dw, b_dw, ln_g, ln_b, seq, tt):
    m = p1.shape[0]
    batch = m // seq
    n_tiles = seq // tt
    hb = tt // CONV_HALO
    n_hblk = m // CONV_HALO
    ca, cb = P1_A // CONV_CH, P1_B // CONV_CH

    def prev_idx(c):
        return lambda b, j: (jnp.maximum((b * n_tiles + j) * hb - 1, 0), c)

    def next_idx(c):
        return lambda b, j: (jnp.minimum((b * n_tiles + j + 1) * hb, n_hblk - 1), c)

    cur = lambda c: (lambda b, j: (b * n_tiles + j, c))
    full = lambda shape: pl.BlockSpec(shape, lambda b, j: (0,) * len(shape))
    return pl.pallas_call(
        functools.partial(_conv_kernel, n_tiles=n_tiles),
        out_shape=jax.ShapeDtypeStruct((m, CONV_CH), BF16),
        grid=(batch, n_tiles),
        in_specs=[pl.BlockSpec((CONV_HALO, CONV_CH), prev_idx(ca)),
                  pl.BlockSpec((tt, CONV_CH), cur(ca)),
                  pl.BlockSpec((CONV_HALO, CONV_CH), next_idx(ca)),
                  pl.BlockSpec((CONV_HALO, CONV_CH), prev_idx(cb)),
                  pl.BlockSpec((tt, CONV_CH), cur(cb)),
                  pl.BlockSpec((CONV_HALO, CONV_CH), next_idx(cb)),
                  full((32, CONV_CH)), full((1, CONV_CH)), full((1, CONV_CH)), full((1, CONV_CH))],
        out_specs=pl.BlockSpec((tt, CONV_CH), lambda b, j: (b * n_tiles + j, 0)),
        scratch_shapes=[pltpu.VMEM((tt + 2 * CONV_HALO, CONV_CH), F32),
                        pltpu.VMEM((SUBLANES, tt + 2 * CONV_HALO - SUBLANES, CONV_CH), F32),
                        pltpu.VMEM((tt, CONV_CH), F32)],
        compiler_params=_cparams(2), name="conv_module",
    )(p1, p1, p1, p1, p1, p1, w_dw, b_dw, ln_g, ln_b)


def _merge_kernel(ona_ref, omla_ref, u_ref, gna_ref, gmla_ref, gcv_ref, wna_ref, wmla_ref, wcv_ref, o_ref):
    sig = lambda r: jax.nn.sigmoid(r[...].astype(F32))
    m = sig(gna_ref) * _dot(ona_ref[...], wna_ref[...])
    m = m + sig(gmla_ref) * _dot(omla_ref[...], wmla_ref[...])
    m = m + sig(gcv_ref) * _dot(u_ref[...], wcv_ref[...])
    o_ref[...] = m.astype(o_ref.dtype)


def _merge(o_na, o_mla, u, p1, w_na_o, w_mla_o, w_conv_o, tm, tn):
    m = o_na.shape[0]
    d = w_na_o.shape[1]
    tm = min(tm, m)
    row = lambda k: pl.BlockSpec((tm, k), lambda i, j: (i, 0))
    gate = lambda off: pl.BlockSpec((tm, tn), lambda i, j: (i, off // tn + j))
    wcol = lambda k: pl.BlockSpec((k, tn), lambda i, j: (0, j))
    return pl.pallas_call(
        _merge_kernel,
        out_shape=jax.ShapeDtypeStruct((m, d), BF16),
        grid=(m // tm, d // tn),
        in_specs=[row(o_na.shape[1]), row(o_mla.shape[1]), row(u.shape[1]),
                  gate(P1_GNA), gate(P1_GMLA), gate(P1_GCV),
                  wcol(w_na_o.shape[0]), wcol(w_mla_o.shape[0]), wcol(w_conv_o.shape[0])],
        out_specs=pl.BlockSpec((tm, tn), lambda i, j: (i, j)),
        compiler_params=_cparams(2), name="gated_merge",
    )(o_na, o_mla, u, p1, p1, p1, w_na_o, w_mla_o, w_conv_o)


def _out_proj_kernel(m_ref, w_ref, x_ref, gate_ref, g2_ref, sc_ref, sh_ref, xo_ref, ho_ref):
    x = x_ref[...] + gate_ref[...] * _dot(m_ref[...], w_ref[...])
    xo_ref[...] = x
    y = x * lax.rsqrt(jnp.mean(x * x, axis=-1, keepdims=True) + EPS) * g2_ref[...]
    ho_ref[...] = (y * (1.0 + sc_ref[...]) + sh_ref[...]).astype(ho_ref.dtype)


def _out_proj(mrg, w_out, x, mod3, mod_row, g2, tm):
    m, d = x.shape
    modspec = lambda k: pl.BlockSpec((None, 1, d), lambda i: (mod_row(i) * 6 + k, 0, 0))
    return pl.pallas_call(
        _out_proj_kernel,
        out_shape=(jax.ShapeDtypeStruct((m, d), F32), jax.ShapeDtypeStruct((m, d), BF16)),
        grid=(m // tm,),
        in_specs=[pl.BlockSpec((tm, d), lambda i: (i, 0)),
                  pl.BlockSpec((d, d), lambda i: (0, 0)),
                  pl.BlockSpec((tm, d), lambda i: (i, 0)),
                  modspec(2),
                  pl.BlockSpec((1, d), lambda i: (0, 0)),
                  modspec(4), modspec(3)],
        out_specs=(pl.BlockSpec((tm, d), lambda i: (i, 0)), pl.BlockSpec((tm, d), lambda i: (i, 0))),
        compiler_params=_cparams(1), name="out_proj_residual",
    )(mrg, w_out, x, mod3, g2.reshape(1, d), mod3, mod3)


def _router_kernel(h_ref, w_ref, b_ref, tri_ref, rank_ref, gate_ref, cnt_ref):
    scores = jax.nn.sigmoid(_dot(h_ref[...], w_ref[...]))
    tm = scores.shape[0]
    lane = lax.broadcasted_iota(jnp.int32, (tm, LANES), 1).astype(F32)
    sel = jnp.where(lane < N_EXPERTS, scores + b_ref[...], -jnp.inf)
    total = jnp.zeros((tm, 1), F32)
    chosen = jnp.zeros((tm, LANES), F32)
    for _ in range(TOP_K):
        mx = sel.max(axis=-1, keepdims=True)
        ix = jnp.where(sel == mx, lane, float(LANES)).min(axis=-1, keepdims=True)
        hit = lane == ix
        total = total + jnp.where(hit, scores, 0.0).sum(axis=-1, keepdims=True)
        sel = jnp.where(hit, -jnp.inf, sel)
        chosen = jnp.where(hit, 1.0, chosen)
    before = _dot(tri_ref[...], chosen.astype(BF16))
    is_chosen = chosen > 0.0
    rank_ref[...] = jnp.where(is_chosen, before, -1.0).T
    gate_ref[...] = (jnp.where(is_chosen, scores, 0.0) / total * ROUTED_SCALE).T
    cnt_ref[...] = chosen.sum(axis=0, keepdims=True)


def _router(h, rw, rb, tm):
    t, d = h.shape
    nt = t // tm
    tri = jnp.asarray(np.tril(np.ones((tm, tm), np.float32), -1), BF16)
    col = pl.BlockSpec((LANES, tm), lambda i: (0, i))
    return pl.pallas_call(
        _router_kernel,
        out_shape=(jax.ShapeDtypeStruct((LANES, t), F32), jax.ShapeDtypeStruct((LANES, t), F32),
                   jax.ShapeDtypeStruct((nt, 1, LANES), F32)),
        grid=(nt,),
        in_specs=[pl.BlockSpec((tm, d), lambda i: (i, 0)),
                  pl.BlockSpec((d, LANES), lambda i: (0, 0)),
                  pl.BlockSpec((1, LANES), lambda i: (0, 0)),
                  pl.BlockSpec((tm, tm), lambda i: (0, 0))],
        out_specs=(col, col, pl.BlockSpec((None, 1, LANES), lambda i: (i, 0, 0))),
        compiler_params=_cparams(1), name="moe_router",
    )(h, rw, rb, tri)


DISPATCH_WIN = 96
COMBINE_WIN = 128
GROUP_ALIGN = 16
N_PAIRS = N_EXPERTS // 2
WINDOW_RING = 4


def _selection(rank_row, gate_row, win_off, win):
    tm = rank_row.shape[1]
    i = lax.broadcasted_iota(jnp.int32, (win, tm), 0).astype(F32) + win_off
    hit = i == jnp.broadcast_to(rank_row, (win, tm))
    if gate_row is None:
        return jnp.where(hit, 1.0, 0.0).astype(BF16)
    return jnp.where(hit, jnp.broadcast_to(gate_row, (win, tm)), 0.0).astype(BF16)


def _dispatch_kernel(base_ref, nchunk_ref, rend_ref, h_ref, rank_ref, xg_ref, obuf, sem):
    j = pl.program_id(0)
    w = DISPATCH_WIN

    def window_copy(slot, half, start):
        return pltpu.make_async_copy(obuf.at[slot, pl.ds(half * w, w)], xg_ref.at[pl.ds(start, w)], sem.at[slot, half])

    @pl.when(j == 0)
    def _zero_region_tails():
        obuf[0] = jnp.zeros(obuf.shape[1:], obuf.dtype)
        n_clear = -(-(MOE_BLOCK + COMBINE_WIN) // w)
        for e in range(N_EXPERTS):
            for i in range(n_clear):
                start = jnp.maximum(rend_ref[e] - (i + 1) * w, 0)
                window_copy(0, i % 2, pl.multiple_of(start, GROUP_ALIGN)).start()
        for e in range(N_EXPERTS):
            for i in range(n_clear):
                window_copy(0, i % 2, 0).wait()

    h = h_ref[...]

    def pair_body(p, carry):
        slot = p % WINDOW_RING

        @pl.when(p >= WINDOW_RING)
        def _():
            window_copy(slot, 0, 0).wait()
            window_copy(slot, 1, 0).wait()

        sel = jnp.concatenate([_selection(rank_ref[pl.ds(2 * p + q, 1), :], None, 0.0, w) for q in range(2)], axis=0)
        obuf[slot] = _dot(sel, h).astype(obuf.dtype)
        for q in range(2):
            window_copy(slot, q, pl.multiple_of(base_ref[j * N_EXPERTS + 2 * p + q], GROUP_ALIGN)).start()
        return carry

    lax.fori_loop(0, N_PAIRS, pair_body, 0)
    for slot in range(WINDOW_RING):
        for half in range(2):
            window_copy(slot, half, 0).wait()

    def overflow_body(e, carry):
        def chunk_body(c, carry2):
            sel = _selection(rank_ref[pl.ds(e, 1), :], None, (c * w).astype(F32), w)
            obuf[0, 0:w, :] = _dot(sel, h).astype(obuf.dtype)
            cp = window_copy(0, 0, pl.multiple_of(base_ref[j * N_EXPERTS + e] + c * w, GROUP_ALIGN))
            cp.start()
            cp.wait()
            return carry2

        lax.fori_loop(1, nchunk_ref[j * N_EXPERTS + e], chunk_body, 0)
        return carry

    @pl.when(nchunk_ref[pl.num_programs(0) * N_EXPERTS + j] > 1)
    def _():
        lax.fori_loop(0, N_EXPERTS, overflow_body, 0)


def _dispatch(h, rank_t, base, nchunk, region_end, n_slots, tm):
    t, d = h.shape
    return pl.pallas_call(
        _dispatch_kernel,
        out_shape=jax.ShapeDtypeStruct((n_slots, d), BF16),
        grid_spec=pltpu.PrefetchScalarGridSpec(
            num_scalar_prefetch=3,
            grid=(t // tm,),
            in_specs=[pl.BlockSpec((tm, d), lambda i, *_: (i, 0)),
                      pl.BlockSpec((LANES, tm), lambda i, *_: (0, i))],
            out_specs=pl.BlockSpec(memory_space=pl.ANY),
            scratch_shapes=[pltpu.VMEM((WINDOW_RING, 2 * DISPATCH_WIN, d), BF16), pltpu.SemaphoreType.DMA((WINDOW_RING, 2))]),
        compiler_params=_cparams(1), name="moe_dispatch",
    )(base, nchunk, region_end, h, rank_t)


def _expert_kernel(be_ref, nb_ref, x_ref, wg_ref, wu_ref, wd_ref, o_ref, wg_s, wu_s, wd_s):
    i = pl.program_id(0)
    active = i < nb_ref[0]
    new_expert = (i == 0) | (be_ref[i] != be_ref[jnp.maximum(i - 1, 0)])

    @pl.when(active & new_expert)
    def _():
        wg_s[...] = wg_ref[...].astype(BF16)
        wu_s[...] = wu_ref[...].astype(BF16)
        wd_s[...] = wd_ref[...].astype(BF16)

    @pl.when(active)
    def _():
        x = x_ref[...]
        a = _dot(x, wg_s[...])
        a = a * jax.nn.sigmoid(a) * _dot(x, wu_s[...])
        o_ref[...] = _dot(a.astype(BF16), wd_s[...]).astype(o_ref.dtype)

    @pl.when(jnp.logical_not(active))
    def _():
        o_ref[...] = jnp.zeros(o_ref.shape, o_ref.dtype)


def _expert_ffn(xg, block_exp, n_used, wg, wu, wd, tb):
    n, d = xg.shape
    de = wg.shape[2]
    return pl.pallas_call(
        _expert_kernel,
        out_shape=jax.ShapeDtypeStruct((n, d), BF16),
        grid_spec=pltpu.PrefetchScalarGridSpec(
            num_scalar_prefetch=2,
            grid=(n // tb,),
            in_specs=[pl.BlockSpec((tb, d), lambda i, be, nb: (jnp.minimum(i, nb[0] - 1), 0)),
                      pl.BlockSpec((None, d, de), lambda i, be, nb: (be[i], 0, 0)),
                      pl.BlockSpec((None, d, de), lambda i, be, nb: (be[i], 0, 0)),
                      pl.BlockSpec((None, de, d), lambda i, be, nb: (be[i], 0, 0))],
            out_specs=pl.BlockSpec((tb, d), lambda i, be, nb: (i, 0)),
            scratch_shapes=[pltpu.VMEM((d, de), BF16), pltpu.VMEM((d, de), BF16), pltpu.VMEM((de, d), BF16)]),
        compiler_params=_cparams(1), name="expert_ffn",
    )(block_exp, n_used, xg, wg, wu, wd)


def _moe_combine_kernel(base_ref, nchunk_ref, x_ref, mgate_ref, s_ref, rank_ref, gate_ref, y_ref, o_ref,
                        ybuf, acc_ref, sem, *, tile_off, n_tiles):
    j = pl.program_id(0) + tile_off
    w = COMBINE_WIN

    def window_copy(slot, half, start):
        return pltpu.make_async_copy(y_ref.at[pl.ds(start, w)], ybuf.at[slot, pl.ds(half * w, w)], sem.at[slot, half])

    def start_pair(p, slot):
        for q in range(2):
            window_copy(slot, q, pl.multiple_of(base_ref[j * N_EXPERTS + 2 * p + q], GROUP_ALIGN)).start()

    ahead = WINDOW_RING - 1
    for p0 in range(ahead):
        start_pair(p0, p0)
    acc_ref[...] = s_ref[...].astype(F32)

    def pair_body(p, carry):
        slot = p % WINDOW_RING
        window_copy(slot, 0, 0).wait()
        window_copy(slot, 1, 0).wait()

        @pl.when(p + ahead < N_PAIRS)
        def _():
            start_pair(p + ahead, (p + ahead) % WINDOW_RING)

        sel = jnp.concatenate([_selection(rank_ref[pl.ds(2 * p + q, 1), :], gate_ref[pl.ds(2 * p + q, 1), :], 0.0, w)
                               for q in range(2)], axis=0)
        acc_ref[...] += _dot_tn(sel, ybuf[slot])
        return carry

    lax.fori_loop(0, N_PAIRS, pair_body, 0)

    def overflow_body(e, carry):
        def chunk_body(c, carry2):
            cp = window_copy(0, 0, pl.multiple_of(base_ref[j * N_EXPERTS + e] + c * w, GROUP_ALIGN))
            cp.start()
            cp.wait()
            sel = _selection(rank_ref[pl.ds(e, 1), :], gate_ref[pl.ds(e, 1), :], (c * w).astype(F32), w)
            acc_ref[...] += _dot_tn(sel, ybuf[0, 0:w, :])
            return carry2

        lax.fori_loop(1, nchunk_ref[j * N_EXPERTS + e], chunk_body, 0)
        return carry

    @pl.when(nchunk_ref[n_tiles * N_EXPERTS + j] > 1)
    def _():
        lax.fori_loop(0, N_EXPERTS, overflow_body, 0)

    o_ref[...] = x_ref[...] + mgate_ref[...] * acc_ref[...]


def _moe_combine(x, mod3, mod_row, y, rank_t, gate_t, shared, base, nchunk, tile_off, tm):
    m, d = x.shape
    return pl.pallas_call(
        functools.partial(_moe_combine_kernel, tile_off=tile_off, n_tiles=rank_t.shape[1] // tm),
        out_shape=jax.ShapeDtypeStruct((m, d), F32),
        grid_spec=pltpu.PrefetchScalarGridSpec(
            num_scalar_prefetch=2,
            grid=(m // tm,),
            in_specs=[pl.BlockSpec((tm, d), lambda i, *_: (i, 0)),
                      pl.BlockSpec((None, 1, d), lambda i, *_: (mod_row(i) * 6 + 5, 0, 0)),
                      pl.BlockSpec((tm, d), lambda i, *_: (i + tile_off, 0)),
                      pl.BlockSpec((LANES, tm), lambda i, *_: (0, i + tile_off)),
                      pl.BlockSpec((LANES, tm), lambda i, *_: (0, i + tile_off)),
                      pl.BlockSpec(memory_space=pl.ANY)],
            out_specs=pl.BlockSpec((tm, d), lambda i, *_: (i, 0)),
            scratch_shapes=[pltpu.VMEM((WINDOW_RING, 2 * COMBINE_WIN, d), BF16), pltpu.VMEM((tm, d), F32),
                            pltpu.SemaphoreType.DMA((WINDOW_RING, 2))]),
        compiler_params=_cparams(1), name="moe_combine",
    )(base, nchunk, x, mod3, shared, rank_t, gate_t, y)


ROUTER_TILE = 512


def _moe(h, rw, rb, wg, wu, wd, sg, su, sd, layer):
    t, d = h.shape
    nt = t // ROUTER_TILE
    rank_t, gate_t, cnt = _router(h, rw, rb, ROUTER_TILE)
    counts = cnt[:, 0, :N_EXPERTS].astype(jnp.int32)
    aligned = (counts + GROUP_ALIGN - 1) // GROUP_ALIGN * GROUP_ALIGN
    region = (aligned.sum(axis=0) + COMBINE_WIN + MOE_BLOCK - 1) // MOE_BLOCK * MOE_BLOCK
    region_end = jnp.cumsum(region)
    base = ((region_end - region)[None, :] + jnp.cumsum(aligned, axis=0) - aligned).reshape(-1)

    def window_counts(win):
        n = (counts + win - 1) // win
        return jnp.concatenate([n.reshape(-1), n.max(axis=1)])

    n_blocks = -(-(t * TOP_K + (GROUP_ALIGN - 1) * nt * N_EXPERTS + N_EXPERTS * (COMBINE_WIN + MOE_BLOCK - 1))
                 // MOE_BLOCK)
    block_start = jnp.arange(n_blocks, dtype=jnp.int32) * MOE_BLOCK
    block_exp = jnp.minimum((region_end[None, :] <= block_start[:, None]).sum(axis=1), N_EXPERTS - 1)
    n_used = (region_end[N_EXPERTS - 1] // MOE_BLOCK).astype(jnp.int32).reshape(1)
    xg = _dispatch(h, rank_t, base, window_counts(DISPATCH_WIN), region_end.astype(jnp.int32), n_blocks * MOE_BLOCK,
                   ROUTER_TILE)
    nchunk = window_counts(COMBINE_WIN)
    y = _expert_ffn(xg, block_exp.astype(jnp.int32) + layer * N_EXPERTS, n_used, wg, wu, wd, MOE_BLOCK)
    shared = _expert_ffn(h, jnp.full((t // MOE_BLOCK,), layer, jnp.int32), jnp.full((1,), t // MOE_BLOCK, jnp.int32),
                         sg, su, sd, MOE_BLOCK)
    return y, shared, (rank_t, gate_t, base, nchunk)


def _rope_tables(seq_len, gain):
    nf = MLA_ROPE // 4
    inv = ROPE_THETA ** (-jnp.arange(nf, dtype=F32) / nf)
    pos = jnp.arange(seq_len, dtype=jnp.int32)
    ang_r = (pos // GRID_W).astype(F32)[:, None] * inv
    ang_c = (pos % GRID_W).astype(F32)[:, None] * inv
    cos = jnp.concatenate([jnp.cos(ang_r)] * 2 + [jnp.cos(ang_c)] * 2, axis=-1)
    sin = jnp.concatenate([-jnp.sin(ang_r), jnp.sin(ang_r), -jnp.sin(ang_c), jnp.sin(ang_c)], axis=-1)
    g = gain.astype(F32)
    cg = cos * g[None]
    sg = sin * g[_ROPE_PARTNER][None]
    return jnp.tile(cg, (1, 2)), jnp.tile(sg, (1, 2))


def _rope_partner():
    nf = MLA_ROPE // 4
    idx = np.arange(MLA_ROPE)
    within = idx % (2 * nf)
    return np.where(within < nf, idx + nf, idx - nf)


_ROPE_PARTNER = _rope_partner()


def _in_splits():
    sizes = (NA_W, NA_W, NA_W, MLA_Q_LORA, MLA_KV_LORA, MLA_ROPE, CONV_CH, CONV_CH)
    offs = np.concatenate([[0], np.cumsum(sizes)])
    return {n: int(o) for n, o in zip(("q", "k", "v", "cq", "ckv", "kr", "a", "b", "g"), offs)}


def _prep_layer(l, dmodel, w_in, na_q_g, na_k_g, na_rpb, w_na_o, mla_q_lora_g, mla_w_uq, mla_kv_lora_g, mla_w_ukv,
                mla_q_g, mla_k_g, w_mla_o, conv_w_dw, conv_b_dw, conv_ln_g, conv_ln_b, w_conv_o, w_out,
                router_w, router_b, exp_w_gate, exp_w_up, exp_w_down, sh_w_gate, sh_w_up, sh_w_down, seq_len):
    o = _in_splits()
    w = w_in[l]
    cols = lambda a, n: w[:, a:a + n]
    p = {}
    p["w_p1"] = jnp.concatenate([cols(o["q"], 3 * NA_W), cols(o["a"], 2 * CONV_CH), cols(o["g"], 3 * dmodel)],
                                axis=1).astype(BF16)
    kr = cols(o["kr"], MLA_ROPE)
    krs = kr[:, _ROPE_PARTNER]
    p["w_p2"] = jnp.concatenate([cols(o["cq"], MLA_Q_LORA), kr, kr, krs, krs, cols(o["ckv"], MLA_KV_LORA)],
                                axis=1).astype(BF16)
    qscale = NA_HEAD_DIM ** -0.5 * LOG2E
    p["na_g"] = jnp.concatenate([jnp.tile(na_q_g[l].astype(F32) * qscale, NA_HEADS),
                                 jnp.tile(na_k_g[l].astype(F32), NA_HEADS)]).reshape(1, 2 * NA_W)
    p["na_bias"] = _na_bias_table(na_rpb[l])
    wq = mla_w_uq[l].reshape(MLA_Q_LORA, MLA_HEADS, MLA_QK)
    wq_rope = wq[:, :, MLA_NOPE:]
    flat = lambda a: a.reshape(a.shape[0], -1)
    p["w_uq"] = jnp.concatenate([flat(wq[:, :, :MLA_NOPE]), flat(wq_rope), flat(wq_rope[:, :, _ROPE_PARTNER])],
                                axis=1).astype(BF16)
    wkv = mla_w_ukv[l].reshape(MLA_KV_LORA, MLA_HEADS, MLA_NOPE + MLA_V)
    p["w_ukv"] = jnp.concatenate([flat(wkv[:, :, :MLA_NOPE]), flat(wkv[:, :, MLA_NOPE:])], axis=1).astype(BF16)
    p["q_lora_g"] = mla_q_lora_g[l].astype(F32).reshape(1, -1)
    p["kv_lora_g"] = mla_kv_lora_g[l].astype(F32).reshape(1, -1)
    mscale = MLA_QK ** -0.5 * LOG2E
    p["q_gn"] = (mla_q_g[l][:MLA_NOPE].astype(F32) * mscale).reshape(1, -1)
    p["k_gn"] = mla_k_g[l][:MLA_NOPE].astype(F32).reshape(1, -1)
    cgq, sgq = _rope_tables(seq_len, mla_q_g[l][MLA_NOPE:] * mscale)
    cgk, sgk = _rope_tables(seq_len, mla_k_g[l][MLA_NOPE:])
    p["q_tabs"] = (cgq, sgq)
    p["k_tabs"] = (cgk, sgk)
    p["q_tabs_c"] = (jnp.tile((mla_q_g[l][MLA_NOPE:].astype(F32) * mscale)[None], (1, 2)), jnp.zeros((1, LANES), F32))
    p["k_tabs_c"] = (jnp.tile(mla_k_g[l][MLA_NOPE:].astype(F32)[None], (1, 2)), jnp.zeros((1, LANES), F32))
    p["w_na_o"] = w_na_o[l].astype(BF16)
    p["w_mla_o"] = w_mla_o[l].astype(BF16)
    p["w_conv_o"] = w_conv_o[l].astype(BF16)
    p["w_out"] = w_out[l].astype(BF16)
    p["conv_w"] = jnp.concatenate([conv_w_dw[l].astype(F32), jnp.zeros((1, CONV_CH), F32)], axis=0)
    p["conv_b"] = conv_b_dw[l].astype(F32).reshape(1, -1)
    p["ln_g"] = conv_ln_g[l].astype(F32).reshape(1, -1)
    p["ln_b"] = conv_ln_b[l].astype(F32).reshape(1, -1)
    p["router_w"] = jnp.concatenate([router_w[l], jnp.zeros((dmodel, LANES - N_EXPERTS), router_w.dtype)],
                                    axis=1).astype(BF16)
    p["router_b"] = jnp.concatenate([router_b[l].astype(F32), jnp.zeros((LANES - N_EXPERTS,), F32)]).reshape(1, LANES)
    flat_e = lambda a: a.reshape((-1,) + a.shape[2:])
    p["exp"] = (flat_e(exp_w_gate), flat_e(exp_w_up), flat_e(exp_w_down))
    p["sh"] = (sh_w_gate, sh_w_up, sh_w_down)
    return p


def _group_matrices():
    i = np.arange(LANES)
    bd = (i[:, None] // NA_HEAD_DIM == i[None, :] // NA_HEAD_DIM)
    ones = np.ones((LANES, LANES), bool)
    e0 = np.broadcast_to((i < MLA_ROPE)[:, None], (LANES, LANES))
    e1 = np.broadcast_to((i >= MLA_ROPE)[:, None], (LANES, LANES))
    f = lambda m: jnp.asarray(m, BF16)
    return f(bd), (f(ones), f(e0), f(e1))


def _expand_tabs(tabs, tm):
    return tuple(jnp.broadcast_to(t, (tm, LANES)) if t.shape[0] == 1 else t for t in tabs)


def _mixer_inputs(h, p, mats, bd, seq, ctx):
    tm_mm = min(1024, h.shape[0])
    p1 =_matmul(h, p["w_p1"], BF16, tm_mm, 1024, "in_proj_1")
    p2 = _matmul(h, p["w_p2"], BF16, tm_mm, P2_W // 2, "in_proj_2")
    qk = _headnorm(p1, p["na_g"], bd, min(512, h.shape[0]))
    tr = 256
    qt = _expand_tabs(p["q_tabs_c"], tr) if ctx else p["q_tabs"]
    kt = _expand_tabs(p["k_tabs_c"], tr) if ctx else p["k_tabs"]
    tseq = tr if ctx else seq
    km, vm = _mla_kv(p2, p["kv_lora_g"], p["w_ukv"], p["k_gn"], kt[0], kt[1], mats, tseq, tr)
    return p1, p2, qk, km, vm, (qt, tseq, tr)


def kernel(x, c, ctx, c_ctx, ada_w, ada_b, norm1_g, norm2_g, w_in, na_q_g, na_k_g, na_rpb, w_na_o, mla_q_lora_g, mla_w_uq, mla_kv_lora_g, mla_w_ukv, mla_q_g, mla_k_g, w_mla_o, conv_w_dw, conv_b_dw, conv_ln_g, conv_ln_b, w_conv_o, w_out, router_w, router_b, exp_w_gate, exp_w_up, exp_w_down, sh_w_gate, sh_w_up, sh_w_down):
    batch, seq, d = x.shape
    lc = ctx.shape[1]
    depth = ada_w.shape[0]
    mod_rows = (batch + 1 + 7) // 8 * 8
    cc = jnp.concatenate([c, c_ctx[None], jnp.zeros((mod_rows - batch - 1, d), c.dtype)], axis=0)
    mod_all = _ada(cc, ada_w, ada_b)
    bd, mats = _group_matrices()
    xs = x.reshape(batch * seq, d)
    zs = ctx.reshape(batch * lc, d)
    tm_x = 512
    tm_c = 256
    x_row = lambda i: (i * tm_x) // seq
    c_row = lambda i: batch

    for l in range(depth):
        need_ctx = l < depth - 1
        p = _prep_layer(l, d, w_in, na_q_g, na_k_g, na_rpb, w_na_o, mla_q_lora_g, mla_w_uq, mla_kv_lora_g, mla_w_ukv,
                        mla_q_g, mla_k_g, w_mla_o, conv_w_dw, conv_b_dw, conv_ln_g, conv_ln_b, w_conv_o, w_out,
                        router_w, router_b, exp_w_gate, exp_w_up, exp_w_down, sh_w_gate, sh_w_up, sh_w_down, seq)
        mod3 = mod_all[l].reshape(mod_rows * 6, 1, d)
        hx = _norm_mod(xs, norm1_g[l], mod3, x_row, 1, 0, tm_x)
        hc = _norm_mod(zs, norm1_g[l], mod3, c_row, 1, 0, tm_c)
        p1x, p2x, qkx, kmx, vmx, (qtx, tsx, trx) = _mixer_inputs(hx, p, mats, bd, seq, False)
        p1c, p2c, qkc, kmc, vmc, (qtc, tsc, trc) = _mixer_inputs(hc, p, mats, bd, lc, True)
        o_na = _na_attention(qkx, p1x, qkc, p1c, p["na_bias"], batch)
        qmx = _mla_q(p2x, p["q_lora_g"], p["w_uq"], p["q_gn"], qtx[0], qtx[1], mats, tsx, trx)
        o_mla = _mla_attention(qmx, [(kmc, vmc), (kmx, vmx)], batch, 512)
        u = _conv_module(p1x, p["conv_w"], p["conv_b"], p["ln_g"], p["ln_b"], seq, 256)
        mrg = _merge(o_na, o_mla, u, p1x, p["w_na_o"], p["w_mla_o"], p["w_conv_o"], 1024, 512)
        xs, hx2 = _out_proj(mrg, p["w_out"], xs, mod3, x_row, norm2_g[l], tm_x)
        if need_ctx:
            o_na_c = _na_ctx_attention(qkc, p1c, batch)
            qmc = _mla_q(p2c, p["q_lora_g"], p["w_uq"], p["q_gn"], qtc[0], qtc[1], mats, tsc, trc)
            o_mla_c = _mla_attention(qmc, [(kmc, vmc)], batch, lc)
            u_c = _conv_module(p1c, p["conv_w"], p["conv_b"], p["ln_g"], p["ln_b"], lc, lc)
            mrg_c = _merge(o_na_c, o_mla_c, u_c, p1c, p["w_na_o"], p["w_mla_o"], p["w_conv_o"], 1024, 512)
            zs, hc2 = _out_proj(mrg_c, p["w_out"], zs, mod3, c_row, norm2_g[l], tm_c)
            tokens = jnp.concatenate([hc2, hx2], axis=0)
        else:
            tokens = hx2
        y, shared, (rank_t, gate_t, base, nchunk) = _moe(tokens, p["router_w"], p["router_b"], *p["exp"], *p["sh"], l)
        tm_cmb = ROUTER_TILE
        x_row_cmb = lambda i: (i * tm_cmb) // seq
        if need_ctx:
            zs = _moe_combine(zs, mod3, c_row, y, rank_t, gate_t, shared, base, nchunk, 0, tm_cmb)
            xs = _moe_combine(xs, mod3, x_row_cmb, y, rank_t, gate_t, shared, base, nchunk, batch * lc // tm_cmb,
                              tm_cmb)
        else:
            xs = _moe_combine(xs, mod3, x_row_cmb, y, rank_t, gate_t, shared, base, nchunk, 0, tm_cmb)
    return xs.reshape(batch, seq, d)
```

```python
import functools

import numpy as np
import jax
import jax.numpy as jnp
from jax import lax
from jax.experimental import pallas as pl
from jax.experimental.pallas import tpu as pltpu

F32 = jnp.float32
BF16 = jnp.bfloat16

GRID_W = 64
EPS = 1e-6
NEG_INF = -1e30
NA_HEADS = 16
NA_HEAD_DIM = 64
NA_W = NA_HEADS * NA_HEAD_DIM
NA_WIN_ROWS = 8
NA_WIN_COLS = 16
MLA_HEADS = 16
MLA_NOPE = 128
MLA_ROPE = 64
MLA_QK = MLA_NOPE + MLA_ROPE
MLA_V = 128
MLA_Q_LORA = 768
MLA_KV_LORA = 512
ROPE_THETA = 10000.0
CONV_CH = 1024
CONV_WIDTH = 31
CONV_HALO = 16
CONV_ACC_ROWS = 128
N_EXPERTS = 64
TOP_K = 8
D_EXPERT = 512
ROUTED_SCALE = 2.5
MOE_BLOCK = 512
LANES = 128
SUBLANES = 8
MLA_HEAD_PAD = 2 * LANES
VMEM_LIMIT = 56 * 1024 * 1024

P1_Q, P1_K, P1_V, P1_A, P1_B, P1_GNA, P1_GMLA, P1_GCV = 0, 1024, 2048, 3072, 4096, 5120, 7168, 9216
P1_W = 11264
P2_CQ, P2_KR, P2_KRS, P2_CKV = 0, 768, 896, 1024
P2_W = 1536


def _cparams(n_axes):
    return pltpu.CompilerParams(dimension_semantics=("arbitrary",) * n_axes, vmem_limit_bytes=VMEM_LIMIT)


def _dot(a, b):
    return jnp.dot(a, b, preferred_element_type=F32)


def _dot_nt(a, b):
    return lax.dot_general(a, b, (((1,), (1,)), ((), ())), preferred_element_type=F32)


def _dot_hilo(x, m):
    hi = x.astype(BF16)
    lo = (x - hi.astype(F32)).astype(BF16)
    return _dot(hi, m) + _dot(lo, m)


def _ada_kernel(c_ref, w_ref, b_ref, o_ref):
    c = c_ref[...]
    a = (c * jax.nn.sigmoid(c)).astype(BF16)
    o_ref[...] = _dot(a, w_ref[...].astype(BF16)) + b_ref[...]


def _ada(cc, ada_w, ada_b):
    nl, d, n = ada_w.shape
    r = cc.shape[0]
    tn = 1024
    return pl.pallas_call(
        _ada_kernel,
        out_shape=jax.ShapeDtypeStruct((nl, r, n), F32),
        grid=(nl, n // tn),
        in_specs=[pl.BlockSpec((r, d), lambda l, j: (0, 0)),
                  pl.BlockSpec((None, d, tn), lambda l, j: (l, 0, j)),
                  pl.BlockSpec((None, 1, tn), lambda l, j: (l, 0, j))],
        out_specs=pl.BlockSpec((None, r, tn), lambda l, j: (l, 0, j)),
        compiler_params=_cparams(2), name="ada_mod",
    )(cc, ada_w, ada_b.reshape(nl, 1, n))


def _norm_mod_kernel(x_ref, g_ref, sc_ref, sh_ref, o_ref):
    x = x_ref[...]
    y = x * lax.rsqrt(jnp.mean(x * x, axis=-1, keepdims=True) + EPS) * g_ref[...]
    o_ref[...] = (y * (1.0 + sc_ref[...]) + sh_ref[...]).astype(o_ref.dtype)


def _norm_mod(x, g, mod3, mod_row, k_sc, k_sh, tm):
    m, d = x.shape
    return pl.pallas_call(
        _norm_mod_kernel,
        out_shape=jax.ShapeDtypeStruct((m, d), BF16),
        grid=(m // tm,),
        in_specs=[pl.BlockSpec((tm, d), lambda i: (i, 0)),
                  pl.BlockSpec((1, d), lambda i: (0, 0)),
                  pl.BlockSpec((None, 1, d), lambda i: (mod_row(i) * 6 + k_sc, 0, 0)),
                  pl.BlockSpec((None, 1, d), lambda i: (mod_row(i) * 6 + k_sh, 0, 0))],
        out_specs=pl.BlockSpec((tm, d), lambda i: (i, 0)),
        compiler_params=_cparams(1), name="norm_mod",
    )(x, g.reshape(1, d), mod3, mod3)


def _mm_kernel(a_ref, w_ref, o_ref):
    o_ref[...] = _dot(a_ref[...], w_ref[...]).astype(o_ref.dtype)


def _matmul(a, w, out_dtype, tm, tn, name):
    m, k = a.shape
    n = w.shape[1]
    return pl.pallas_call(
        _mm_kernel,
        out_shape=jax.ShapeDtypeStruct((m, n), out_dtype),
        grid=(m // tm, n // tn),
        in_specs=[pl.BlockSpec((tm, k), lambda i, j: (i, 0)),
                  pl.BlockSpec((k, tn), lambda i, j: (0, j))],
        out_specs=pl.BlockSpec((tm, tn), lambda i, j: (i, j)),
        compiler_params=_cparams(2), name=name,
    )(a, w)


def _headnorm_kernel(x_ref, g_ref, bd_ref, o_ref):
    bd = bd_ref[...]
    for c in range(x_ref.shape[1] // LANES):
        sl = slice(c * LANES, (c + 1) * LANES)
        x = x_ref[:, sl].astype(F32)
        ss = _dot_hilo(x * x, bd)
        y = x * lax.rsqrt(ss * (1.0 / NA_HEAD_DIM) + EPS) * g_ref[:, sl]
        o_ref[:, sl] = y.astype(o_ref.dtype)


def _headnorm(p1, g_row, bd, tm):
    m = p1.shape[0]
    w = g_row.shape[1]
    return pl.pallas_call(
        _headnorm_kernel,
        out_shape=jax.ShapeDtypeStruct((m, w), BF16),
        grid=(m // tm,),
        in_specs=[pl.BlockSpec((tm, w), lambda i: (i, 0)),
                  pl.BlockSpec((1, w), lambda i: (0, 0)),
                  pl.BlockSpec((LANES, LANES), lambda i: (0, 0))],
        out_specs=pl.BlockSpec((tm, w), lambda i: (i, 0)),
        compiler_params=_cparams(1), name="na_headnorm",
    )(p1, g_row, bd)


LOG2E = 1.4426950408889634


def _softmax_pv(scores, values):
    m = scores[0].max(axis=-1, keepdims=True)
    for s in scores[1:]:
        m = jnp.maximum(m, s.max(axis=-1, keepdims=True))
    l = None
    o = None
    for s, v in zip(scores, values):
        p = jnp.exp2(s - m)
        ps = p.sum(axis=-1, keepdims=True)
        po = _dot(p.astype(BF16), v)
        l = ps if l is None else l + ps
        o = po if o is None else o + po
    return o / l


def _dot_tn(a, b):
    return lax.dot_general(a, b, (((0,), (0,)), ((), ())), preferred_element_type=F32)


def _na_kernel(q_ref, kx_ref, vx_ref, kc_ref, vc_ref, bias_ref, o_ref):
    rows = kx_ref.shape[0] // GRID_W
    r = pl.program_id(1)
    r0 = jnp.clip(r - NA_WIN_ROWS // 2, 0, rows - NA_WIN_ROWS)
    kstart = pl.multiple_of(r0 * GRID_W, GRID_W)
    nwin = NA_WIN_ROWS * GRID_W
    lo = lax.broadcasted_iota(jnp.int32, (GRID_W, LANES), 1) < NA_HEAD_DIM
    zero = jnp.zeros((GRID_W, LANES), q_ref.dtype)
    n_pairs = NA_W // LANES
    pair = lambda p: slice(p * LANES, (p + 1) * LANES)
    scores = []
    for p in range(n_pairs):
        q2 = q_ref[:, pair(p)]
        qs = jnp.concatenate([jnp.where(lo, q2, zero), jnp.where(lo, zero, q2)], axis=0)
        s_w = _dot_nt(kx_ref[pl.ds(kstart, nwin), pair(p)], qs) + bias_ref[p]
        s_c = _dot_nt(kc_ref[:, pair(p)], qs)
        scores.append((s_w, s_c))
    probs = []
    for s_w, s_c in scores:
        m = jnp.maximum(s_w.max(axis=0, keepdims=True), s_c.max(axis=0, keepdims=True))
        p_w = jnp.exp2(s_w - m)
        p_c = jnp.exp2(s_c - m)
        inv = 1.0 / (p_w.sum(axis=0, keepdims=True) + p_c.sum(axis=0, keepdims=True))
        probs.append(((p_w * inv).astype(BF16), (p_c * inv).astype(BF16)))
    outs = []
    for p, (p_w, p_c) in enumerate(probs):
        o2 = _dot_tn(p_w, vx_ref[pl.ds(kstart, nwin), pair(p)]) + _dot_tn(p_c, vc_ref[:, pair(p)])
        outs.append(jnp.where(lo, o2[:GRID_W], o2[GRID_W:]).astype(o_ref.dtype))
    o_ref[...] = jnp.concatenate(outs, axis=1)


def _na_attention(qk_x, p1_x, qk_c, p1_c, bias_tab, batch):
    l = qk_x.shape[0] // batch
    lc = qk_c.shape[0] // batch
    rows = l // GRID_W
    nwin = NA_WIN_ROWS * GRID_W

    def bias_idx(b, r):
        r0 = jnp.clip(r - NA_WIN_ROWS // 2, 0, rows - NA_WIN_ROWS)
        return (r - r0, 0, 0, 0)

    return pl.pallas_call(
        _na_kernel,
        out_shape=jax.ShapeDtypeStruct((batch * l, NA_W), BF16),
        grid=(batch, rows),
        in_specs=[pl.BlockSpec((GRID_W, NA_W), lambda b, r: (b * rows + r, 0)),
                  pl.BlockSpec((l, NA_W), lambda b, r: (b, 1)),
                  pl.BlockSpec((l, NA_W), lambda b, r: (b, P1_V // NA_W)),
                  pl.BlockSpec((lc, NA_W), lambda b, r: (b, 1)),
                  pl.BlockSpec((lc, NA_W), lambda b, r: (b, P1_V // NA_W)),
                  pl.BlockSpec((None, NA_HEADS // 2, nwin, LANES), bias_idx)],
        out_specs=pl.BlockSpec((GRID_W, NA_W), lambda b, r: (b * rows + r, 0)),
        compiler_params=_cparams(2), name="na_attention",
    )(qk_x, qk_x, p1_x, qk_c, p1_c, bias_tab)


def _na_ctx_kernel(q_ref, k_ref, v_ref, o_ref):
    n = q_ref.shape[0]
    lo = lax.broadcasted_iota(jnp.int32, (n, LANES), 1) < NA_HEAD_DIM
    zero = jnp.zeros((n, LANES), q_ref.dtype)
    for p in range(NA_W // LANES):
        sl = slice(p * LANES, (p + 1) * LANES)
        q2 = q_ref[:, sl]
        k = k_ref[:, sl]
        v = v_ref[:, sl]
        outs = []
        for hh in range(2):
            qm = jnp.where(lo, q2, zero) if hh == 0 else jnp.where(lo, zero, q2)
            outs.append(_softmax_pv([_dot_nt(qm, k)], [v]))
        o_ref[:, sl] = jnp.where(lo, outs[0], outs[1]).astype(o_ref.dtype)


def _na_ctx_attention(qk_c, p1_c, batch):
    lc = qk_c.shape[0] // batch
    return pl.pallas_call(
        _na_ctx_kernel,
        out_shape=jax.ShapeDtypeStruct((batch * lc, NA_W), BF16),
        grid=(batch,),
        in_specs=[pl.BlockSpec((lc, NA_W), lambda b: (b, 0)),
                  pl.BlockSpec((lc, NA_W), lambda b: (b, 1)),
                  pl.BlockSpec((lc, NA_W), lambda b: (b, P1_V // NA_W))],
        out_specs=pl.BlockSpec((lc, NA_W), lambda b: (b, 0)),
        compiler_params=_cparams(1), name="na_ctx_attention",
    )(qk_c, qk_c, p1_c)


def _na_bias_table(rpb):
    cls = np.arange(NA_WIN_ROWS)[:, None]
    w = np.arange(NA_WIN_ROWS)[None, :]
    qc = np.arange(GRID_W)[:, None]
    kc = np.arange(GRID_W)[None, :]
    c0 = np.clip(qc - NA_WIN_COLS // 2, 0, GRID_W - NA_WIN_COLS)
    ok = (kc >= c0) & (kc < c0 + NA_WIN_COLS)
    dr = w - cls + NA_WIN_ROWS - 1
    dc = np.clip(kc - qc + NA_WIN_COLS - 1, 0, 2 * NA_WIN_COLS - 2)
    row_sel = jnp.asarray(dr[:, :, None] == np.arange(2 * NA_WIN_ROWS - 1), F32)
    col_sel = jnp.asarray(dc[:, :, None] == np.arange(2 * NA_WIN_COLS - 1), F32)
    exact = lax.Precision.HIGHEST
    t1 = jnp.einsum('hab,cwa->hcwb', rpb.astype(F32) * LOG2E, row_sel, precision=exact)
    tab = jnp.einsum('hcwb,qkb->chwkq', t1, col_sel, precision=exact)
    tab = jnp.where(jnp.asarray(ok.T)[None, None, None], tab, NEG_INF)
    tab = tab.reshape(NA_WIN_ROWS, NA_HEADS // 2, 2, NA_WIN_ROWS * GRID_W, GRID_W)
    tab = jnp.transpose(tab, (0, 1, 3, 2, 4))
    return tab.reshape(NA_WIN_ROWS, NA_HEADS // 2, NA_WIN_ROWS * GRID_W, LANES)


def _mla_q_kernel(p2_ref, gl_ref, w_ref, gn_ref, cg_ref, sg_ref, ones_ref, e0_ref, e1_ref, o_ref):
    cq = p2_ref[...].astype(F32)
    cqn = cq * lax.rsqrt(jnp.mean(cq * cq, axis=-1, keepdims=True) + EPS) * gl_ref[...]
    y = _dot(cqn.astype(BF16), w_ref[...])
    tm = y.shape[0]
    lo = lax.broadcasted_iota(jnp.int32, (tm, LANES), 1) < MLA_ROPE
    nope_w = MLA_HEADS * MLA_NOPE
    rope_w = MLA_HEADS * MLA_ROPE
    ones, e0, e1 = ones_ref[...], e0_ref[...], e1_ref[...]
    gn = gn_ref[...]
    for p in range(MLA_HEADS // 2):
        yr = y[:, nope_w + p * LANES: nope_w + (p + 1) * LANES]
        yrs = y[:, nope_w + rope_w + p * LANES: nope_w + rope_w + (p + 1) * LANES]
        rot = yr * cg_ref[...] + yrs * sg_ref[...]
        yr2 = yr * yr
        ssr = (_dot_hilo(yr2, e0), _dot_hilo(yr2, e1))
        for hh in range(2):
            h = 2 * p + hh
            nope = y[:, h * MLA_NOPE:(h + 1) * MLA_NOPE]
            tot = _dot_hilo(nope * nope, ones) + ssr[hh]
            scale = lax.rsqrt(tot * (1.0 / MLA_QK) + EPS)
            o_ref[:, h * MLA_HEAD_PAD: h * MLA_HEAD_PAD + LANES] = (nope * scale * gn).astype(o_ref.dtype)
            rs = rot * scale
            rs = jnp.where(lo, rs, 0.0) if hh == 0 else jnp.where(lo, 0.0, rs)
            o_ref[:, h * MLA_HEAD_PAD + LANES: (h + 1) * MLA_HEAD_PAD] = rs.astype(o_ref.dtype)


def _mla_q(p2, g_lora, w_uq, g_nope, cg, sg, mats, seq, tm):
    m = p2.shape[0]
    nt = seq // tm
    ones, e0, e1 = mats
    full = lambda shape: pl.BlockSpec(shape, lambda i: (0,) * len(shape))
    return pl.pallas_call(
        _mla_q_kernel,
        out_shape=jax.ShapeDtypeStruct((m, MLA_HEADS * MLA_HEAD_PAD), BF16),
        grid=(m // tm,),
        in_specs=[pl.BlockSpec((tm, MLA_Q_LORA), lambda i: (i, 0)),
                  full((1, MLA_Q_LORA)),
                  full(w_uq.shape),
                  full((1, LANES)),
                  pl.BlockSpec((tm, LANES), lambda i: (i % nt, 0)),
                  pl.BlockSpec((tm, LANES), lambda i: (i % nt, 0)),
                  full((LANES, LANES)), full((LANES, LANES)), full((LANES, LANES))],
        out_specs=pl.BlockSpec((tm, MLA_HEADS * MLA_HEAD_PAD), lambda i: (i, 0)),
        compiler_params=_cparams(1), name="mla_q_proj",
    )(p2, g_lora, w_uq, g_nope, cg, sg, ones, e0, e1)


def _mla_kv_kernel(ckv_ref, kr_ref, krs_ref, gl_ref, w_ref, gn_ref, cg_ref, sg_ref, ones_ref, e0_ref, k_ref, v_ref):
    ckv = ckv_ref[...].astype(F32)
    cn = ckv * lax.rsqrt(jnp.mean(ckv * ckv, axis=-1, keepdims=True) + EPS) * gl_ref[...]
    y = _dot(cn.astype(BF16), w_ref[...])
    tm = y.shape[0]
    lo = lax.broadcasted_iota(jnp.int32, (tm, LANES), 1) < MLA_ROPE
    nope_w = MLA_HEADS * MLA_NOPE
    v_ref[...] = y[:, nope_w:].astype(v_ref.dtype)
    kr = kr_ref[...].astype(F32)
    rot = kr * cg_ref[...] + krs_ref[...].astype(F32) * sg_ref[...]
    ssr = _dot_hilo(kr * kr, e0_ref[...])
    ones = ones_ref[...]
    gn = gn_ref[...]
    for h in range(MLA_HEADS):
        nope = y[:, h * MLA_NOPE:(h + 1) * MLA_NOPE]
        tot = _dot_hilo(nope * nope, ones) + ssr
        scale = lax.rsqrt(tot * (1.0 / MLA_QK) + EPS)
        k_ref[:, h * MLA_HEAD_PAD: h * MLA_HEAD_PAD + LANES] = (nope * scale * gn).astype(k_ref.dtype)
        rs = rot * scale
        rs = jnp.where(lo, rs, 0.0) if h % 2 == 0 else jnp.where(lo, 0.0, rs)
        k_ref[:, h * MLA_HEAD_PAD + LANES: (h + 1) * MLA_HEAD_PAD] = rs.astype(k_ref.dtype)


def _mla_kv(p2, g_lora, w_ukv, g_nope, cg, sg, mats, seq, tm):
    m = p2.shape[0]
    nt = seq // tm
    ones, e0, _ = mats
    full = lambda shape: pl.BlockSpec(shape, lambda i: (0,) * len(shape))
    return pl.pallas_call(
        _mla_kv_kernel,
        out_shape=(jax.ShapeDtypeStruct((m, MLA_HEADS * MLA_HEAD_PAD), BF16),
                   jax.ShapeDtypeStruct((m, MLA_HEADS * MLA_V), BF16)),
        grid=(m // tm,),
        in_specs=[pl.BlockSpec((tm, MLA_KV_LORA), lambda i: (i, P2_CKV // MLA_KV_LORA)),
                  pl.BlockSpec((tm, LANES), lambda i: (i, P2_KR // LANES)),
                  pl.BlockSpec((tm, LANES), lambda i: (i, P2_KRS // LANES)),
                  full((1, MLA_KV_LORA)),
                  full(w_ukv.shape),
                  full((1, LANES)),
                  pl.BlockSpec((tm, LANES), lambda i: (i % nt, 0)),
                  pl.BlockSpec((tm, LANES), lambda i: (i % nt, 0)),
                  full((LANES, LANES)), full((LANES, LANES))],
        out_specs=(pl.BlockSpec((tm, MLA_HEADS * MLA_HEAD_PAD), lambda i: (i, 0)),
                   pl.BlockSpec((tm, MLA_HEADS * MLA_V), lambda i: (i, 0))),
        compiler_params=_cparams(1), name="mla_kv_proj",
    )(p2, p2, p2, g_lora, w_ukv, g_nope, cg, sg, ones, e0)


def _mla_attn_kernel(*refs, n_kv):
    q_ref = refs[0]
    kv = refs[1:1 + 2 * n_kv]
    o_ref = refs[1 + 2 * n_kv]
    heads = range(MLA_HEADS_PER_STEP)
    qsl = lambda hh: slice(hh * MLA_HEAD_PAD, (hh + 1) * MLA_HEAD_PAD)
    vsl = lambda hh: slice(hh * MLA_V, (hh + 1) * MLA_V)
    scores = [[_dot_nt(q_ref[:, qsl(hh)], kv[2 * i][:, qsl(hh)]) for i in range(n_kv)] for hh in heads]
    probs = []
    for s_list in scores:
        m = s_list[0].max(axis=-1, keepdims=True)
        for s in s_list[1:]:
            m = jnp.maximum(m, s.max(axis=-1, keepdims=True))
        p_list = [jnp.exp2(s - m) for s in s_list]
        l = p_list[0].sum(axis=-1, keepdims=True)
        for p in p_list[1:]:
            l = l + p.sum(axis=-1, keepdims=True)
        probs.append(([p.astype(BF16) for p in p_list], l))
    outs = []
    for hh, (p_list, l) in zip(heads, probs):
        o = _dot(p_list[0], kv[1][:, vsl(hh)])
        for i in range(1, n_kv):
            o = o + _dot(p_list[i], kv[2 * i + 1][:, vsl(hh)])
        outs.append((o / l).astype(o_ref.dtype))
    o_ref[...] = jnp.concatenate(outs, axis=1)


MLA_HEADS_PER_STEP = 2


def _mla_attention(q, kvs, batch, tq):
    lq = q.shape[0] // batch
    nq = lq // tq
    hs = MLA_HEADS_PER_STEP
    in_specs = [pl.BlockSpec((tq, hs * MLA_HEAD_PAD), lambda b, h, j: (b * nq + j, h))]
    args = [q]
    for k, v in kvs:
        s = k.shape[0] // batch
        in_specs.append(pl.BlockSpec((s, hs * MLA_HEAD_PAD), lambda b, h, j: (b, h)))
        in_specs.append(pl.BlockSpec((s, hs * MLA_V), lambda b, h, j: (b, h)))
        args += [k, v]
    return pl.pallas_call(
        functools.partial(_mla_attn_kernel, n_kv=len(kvs)),
        out_shape=jax.ShapeDtypeStruct((batch * lq, MLA_HEADS * MLA_V), BF16),
        grid=(batch, MLA_HEADS // hs, nq),
        in_specs=in_specs,
        out_specs=pl.BlockSpec((tq, hs * MLA_V), lambda b, h, j: (b * nq + j, h)),
        compiler_params=_cparams(3), name="mla_attention",
    )(*args)


def _conv_kernel(ap_ref, a_ref, an_ref, bp_ref, b_ref, bn_ref, w_ref, cb_ref, g_ref, beta_ref, o_ref, u_scr, sh_scr,
                 y_scr, *, n_tiles):
    j = pl.program_id(1)
    tt = a_ref.shape[0]
    glu = lambda a, b: a[...].astype(F32) * jax.nn.sigmoid(b[...].astype(F32))
    u_scr[0:CONV_HALO, :] = jnp.where(j > 0, glu(ap_ref, bp_ref), 0.0)
    u_scr[CONV_HALO:CONV_HALO + tt, :] = glu(a_ref, b_ref)
    u_scr[CONV_HALO + tt:, :] = jnp.where(j < n_tiles - 1, glu(an_ref, bn_ref), 0.0)
    n_sh = sh_scr.shape[1]
    for b in range(SUBLANES):
        sh_scr[b] = u_scr[b:b + n_sh, :]
    off = CONV_HALO - CONV_WIDTH // 2
    rows = min(tt, CONV_ACC_ROWS)
    for c in range(CONV_CH // LANES):
        sl = slice(c * LANES, (c + 1) * LANES)
        for r0 in range(0, tt, rows):
            acc = jnp.zeros((rows, LANES), F32)
            for k in range(CONV_WIDTH):
                a8, b = (off + k) // SUBLANES * SUBLANES + r0, (off + k) % SUBLANES
                acc = acc + sh_scr[b, a8:a8 + rows, sl] * w_ref[k:k + 1, sl]
            y_scr[r0:r0 + rows, sl] = acc + cb_ref[:, sl]
    y = y_scr[...]
    mu = jnp.mean(y, axis=-1, keepdims=True)
    d = y - mu
    var = jnp.mean(d * d, axis=-1, keepdims=True)
    z = d * lax.rsqrt(var + EPS) * g_ref[...] + beta_ref[...]
    o_ref[...] = (z * jax.nn.sigmoid(z)).astype(o_ref.dtype)


def _conv_module(p1, w_dw, b_dw, ln_g, ln_b, seq, tt):
    m = p1.shape[0]
    batch = m // seq
    n_tiles = seq // tt
    hb = tt // CONV_HALO
    n_hblk = m // CONV_HALO
    ca, cb = P1_A // CONV_CH, P1_B // CONV_CH

    def prev_idx(c):
        return lambda b, j: (jnp.maximum((b * n_tiles + j) * hb - 1, 0), c)

    def next_idx(c):
        return lambda b, j: (jnp.minimum((b * n_tiles + j + 1) * hb, n_hblk - 1), c)

    cur = lambda c: (lambda b, j: (b * n_tiles + j, c))
    full = lambda shape: pl.BlockSpec(shape, lambda b, j: (0,) * len(shape))
    return pl.pallas_call(
        functools.partial(_conv_kernel, n_tiles=n_tiles),
        out_shape=jax.ShapeDtypeStruct((m, CONV_CH), BF16),
        grid=(batch, n_tiles),
        in_specs=[pl.BlockSpec((CONV_HALO, CONV_CH), prev_idx(ca)),
                  pl.BlockSpec((tt, CONV_CH), cur(ca)),
                  pl.BlockSpec((CONV_HALO, CONV_CH), next_idx(ca)),
                  pl.BlockSpec((CONV_HALO, CONV_CH), prev_idx(cb)),
                  pl.BlockSpec((tt, CONV_CH), cur(cb)),
                  pl.BlockSpec((CONV_HALO, CONV_CH), next_idx(cb)),
                  full((32, CONV_CH)), full((1, CONV_CH)), full((1, CONV_CH)), full((1, CONV_CH))],
        out_specs=pl.BlockSpec((tt, CONV_CH), lambda b, j: (b * n_tiles + j, 0)),
        scratch_shapes=[pltpu.VMEM((tt + 2 * CONV_HALO, CONV_CH), F32),
                        pltpu.VMEM((SUBLANES, tt + 2 * CONV_HALO - SUBLANES, CONV_CH), F32),
                        pltpu.VMEM((tt, CONV_CH), F32)],
        compiler_params=_cparams(2), name="conv_module",
    )(p1, p1, p1, p1, p1, p1, w_dw, b_dw, ln_g, ln_b)


def _merge_kernel(ona_ref, omla_ref, u_ref, gna_ref, gmla_ref, gcv_ref, wna_ref, wmla_ref, wcv_ref, o_ref):
    sig = lambda r: jax.nn.sigmoid(r[...].astype(F32))
    m = sig(gna_ref) * _dot(ona_ref[...], wna_ref[...])
    m = m + sig(gmla_ref) * _dot(omla_ref[...], wmla_ref[...])
    m = m + sig(gcv_ref) * _dot(u_ref[...], wcv_ref[...])
    o_ref[...] = m.astype(o_ref.dtype)


def _merge(o_na, o_mla, u, p1, w_na_o, w_mla_o, w_conv_o, tm, tn):
    m = o_na.shape[0]
    d = w_na_o.shape[1]
    tm = min(tm, m)
    row = lambda k: pl.BlockSpec((tm, k), lambda i, j: (i, 0))
    gate = lambda off: pl.BlockSpec((tm, tn), lambda i, j: (i, off // tn + j))
    wcol = lambda k: pl.BlockSpec((k, tn), lambda i, j: (0, j))
    return pl.pallas_call(
        _merge_kernel,
        out_shape=jax.ShapeDtypeStruct((m, d), BF16),
        grid=(m // tm, d // tn),
        in_specs=[row(o_na.shape[1]), row(o_mla.shape[1]), row(u.shape[1]),
                  gate(P1_GNA), gate(P1_GMLA), gate(P1_GCV),
                  wcol(w_na_o.shape[0]), wcol(w_mla_o.shape[0]), wcol(w_conv_o.shape[0])],
        out_specs=pl.BlockSpec((tm, tn), lambda i, j: (i, j)),
        compiler_params=_cparams(2), name="gated_merge",
    )(o_na, o_mla, u, p1, p1, p1, w_na_o, w_mla_o, w_conv_o)


def _out_proj_kernel(m_ref, w_ref, x_ref, gate_ref, g2_ref, sc_ref, sh_ref, xo_ref, ho_ref):
    x = x_ref[...] + gate_ref[...] * _dot(m_ref[...], w_ref[...])
    xo_ref[...] = x
    y = x * lax.rsqrt(jnp.mean(x * x, axis=-1, keepdims=True) + EPS) * g2_ref[...]
    ho_ref[...] = (y * (1.0 + sc_ref[...]) + sh_ref[...]).astype(ho_ref.dtype)


def _out_proj(mrg, w_out, x, mod3, mod_row, g2, tm):
    m, d = x.shape
    modspec = lambda k: pl.BlockSpec((None, 1, d), lambda i: (mod_row(i) * 6 + k, 0, 0))
    return pl.pallas_call(
        _out_proj_kernel,
        out_shape=(jax.ShapeDtypeStruct((m, d), F32), jax.ShapeDtypeStruct((m, d), BF16)),
        grid=(m // tm,),
        in_specs=[pl.BlockSpec((tm, d), lambda i: (i, 0)),
                  pl.BlockSpec((d, d), lambda i: (0, 0)),
                  pl.BlockSpec((tm, d), lambda i: (i, 0)),
                  modspec(2),
                  pl.BlockSpec((1, d), lambda i: (0, 0)),
                  modspec(4), modspec(3)],
        out_specs=(pl.BlockSpec((tm, d), lambda i: (i, 0)), pl.BlockSpec((tm, d), lambda i: (i, 0))),
        compiler_params=_cparams(1), name="out_proj_residual",
    )(mrg, w_out, x, mod3, g2.reshape(1, d), mod3, mod3)


def _router_kernel(h_ref, w_ref, b_ref, tri_ref, rank_ref, gate_ref, cnt_ref):
    scores = jax.nn.sigmoid(_dot(h_ref[...], w_ref[...]))
    tm = scores.shape[0]
    lane = lax.broadcasted_iota(jnp.int32, (tm, LANES), 1).astype(F32)
    sel = jnp.where(lane < N_EXPERTS, scores + b_ref[...], -jnp.inf)
    total = jnp.zeros((tm, 1), F32)
    chosen = jnp.zeros((tm, LANES), F32)
    for _ in range(TOP_K):
        mx = sel.max(axis=-1, keepdims=True)
        ix = jnp.where(sel == mx, lane, float(LANES)).min(axis=-1, keepdims=True)
        hit = lane == ix
        total = total + jnp.where(hit, scores, 0.0).sum(axis=-1, keepdims=True)
        sel = jnp.where(hit, -jnp.inf, sel)
        chosen = jnp.where(hit, 1.0, chosen)
    before = _dot(tri_ref[...], chosen.astype(BF16))
    is_chosen = chosen > 0.0
    rank_ref[...] = jnp.where(is_chosen, before, -1.0).T
    gate_ref[...] = (jnp.where(is_chosen, scores, 0.0) / total * ROUTED_SCALE).T
    cnt_ref[...] = chosen.sum(axis=0, keepdims=True)


def _router(h, rw, rb, tm):
    t, d = h.shape
    nt = t // tm
    tri = jnp.asarray(np.tril(np.ones((tm, tm), np.float32), -1), BF16)
    col = pl.BlockSpec((LANES, tm), lambda i: (0, i))
    return pl.pallas_call(
        _router_kernel,
        out_shape=(jax.ShapeDtypeStruct((LANES, t), F32), jax.ShapeDtypeStruct((LANES, t), F32),
                   jax.ShapeDtypeStruct((nt, 1, LANES), F32)),
        grid=(nt,),
        in_specs=[pl.BlockSpec((tm, d), lambda i: (i, 0)),
                  pl.BlockSpec((d, LANES), lambda i: (0, 0)),
                  pl.BlockSpec((1, LANES), lambda i: (0, 0)),
                  pl.BlockSpec((tm, tm), lambda i: (0, 0))],
        out_specs=(col, col, pl.BlockSpec((None, 1, LANES), lambda i: (i, 0, 0))),
        compiler_params=_cparams(1), name="moe_router",
    )(h, rw, rb, tri)


DISPATCH_WIN = 96
COMBINE_WIN = 128
GROUP_ALIGN = 16
N_PAIRS = N_EXPERTS // 2
WINDOW_RING = 4


def _selection(rank_row, gate_row, win_off, win):
    tm = rank_row.shape[1]
    i = lax.broadcasted_iota(jnp.int32, (win, tm), 0).astype(F32) + win_off
    hit = i == jnp.broadcast_to(rank_row, (win, tm))
    if gate_row is None:
        return jnp.where(hit, 1.0, 0.0).astype(BF16)
    return jnp.where(hit, jnp.broadcast_to(gate_row, (win, tm)), 0.0).astype(BF16)


def _dispatch_kernel(base_ref, nchunk_ref, rend_ref, h_ref, rank_ref, xg_ref, obuf, sem):
    j = pl.program_id(0)
    w = DISPATCH_WIN

    def window_copy(slot, half, start):
        return pltpu.make_async_copy(obuf.at[slot, pl.ds(half * w, w)], xg_ref.at[pl.ds(start, w)], sem.at[slot, half])

    @pl.when(j == 0)
    def _zero_region_tails():
        obuf[0] = jnp.zeros(obuf.shape[1:], obuf.dtype)
        n_clear = -(-(MOE_BLOCK + COMBINE_WIN) // w)
        for e in range(N_EXPERTS):
            for i in range(n_clear):
                start = jnp.maximum(rend_ref[e] - (i + 1) * w, 0)
                window_copy(0, i % 2, pl.multiple_of(start, GROUP_ALIGN)).start()
        for e in range(N_EXPERTS):
            for i in range(n_clear):
                window_copy(0, i % 2, 0).wait()

    h = h_ref[...]

    def pair_body(p, carry):
        slot = p % WINDOW_RING

        @pl.when(p >= WINDOW_RING)
        def _():
            window_copy(slot, 0, 0).wait()
            window_copy(slot, 1, 0).wait()

        sel = jnp.concatenate([_selection(rank_ref[pl.ds(2 * p + q, 1), :], None, 0.0, w) for q in range(2)], axis=0)
        obuf[slot] = _dot(sel, h).astype(obuf.dtype)
        for q in range(2):
            window_copy(slot, q, pl.multiple_of(base_ref[j * N_EXPERTS + 2 * p + q], GROUP_ALIGN)).start()
        return carry

    lax.fori_loop(0, N_PAIRS, pair_body, 0)
    for slot in range(WINDOW_RING):
        for half in range(2):
            window_copy(slot, half, 0).wait()

    def overflow_body(e, carry):
        def chunk_body(c, carry2):
            sel = _selection(rank_ref[pl.ds(e, 1), :], None, (c * w).astype(F32), w)
            obuf[0, 0:w, :] = _dot(sel, h).astype(obuf.dtype)
            cp = window_copy(0, 0, pl.multiple_of(base_ref[j * N_EXPERTS + e] + c * w, GROUP_ALIGN))
            cp.start()
            cp.wait()
            return carry2

        lax.fori_loop(1, nchunk_ref[j * N_EXPERTS + e], chunk_body, 0)
        return carry

    @pl.when(nchunk_ref[pl.num_programs(0) * N_EXPERTS + j] > 1)
    def _():
        lax.fori_loop(0, N_EXPERTS, overflow_body, 0)


def _dispatch(h, rank_t, base, nchunk, region_end, n_slots, tm):
    t, d = h.shape
    return pl.pallas_call(
        _dispatch_kernel,
        out_shape=jax.ShapeDtypeStruct((n_slots, d), BF16),
        grid_spec=pltpu.PrefetchScalarGridSpec(
            num_scalar_prefetch=3,
            grid=(t // tm,),
            in_specs=[pl.BlockSpec((tm, d), lambda i, *_: (i, 0)),
                      pl.BlockSpec((LANES, tm), lambda i, *_: (0, i))],
            out_specs=pl.BlockSpec(memory_space=pl.ANY),
            scratch_shapes=[pltpu.VMEM((WINDOW_RING, 2 * DISPATCH_WIN, d), BF16), pltpu.SemaphoreType.DMA((WINDOW_RING, 2))]),
        compiler_params=_cparams(1), name="moe_dispatch",
    )(base, nchunk, region_end, h, rank_t)


def _expert_kernel(be_ref, nb_ref, x_ref, wg_ref, wu_ref, wd_ref, o_ref, wg_s, wu_s, wd_s):
    i = pl.program_id(0)
    active = i < nb_ref[0]
    new_expert = (i == 0) | (be_ref[i] != be_ref[jnp.maximum(i - 1, 0)])

    @pl.when(active & new_expert)
    def _():
        wg_s[...] = wg_ref[...].astype(BF16)
        wu_s[...] = wu_ref[...].astype(BF16)
        wd_s[...] = wd_ref[...].astype(BF16)

    @pl.when(active)
    def _():
        x = x_ref[...]
        a = _dot(x, wg_s[...])
        a = a * jax.nn.sigmoid(a) * _dot(x, wu_s[...])
        o_ref[...] = _dot(a.astype(BF16), wd_s[...]).astype(o_ref.dtype)

    @pl.when(jnp.logical_not(active))
    def _():
        o_ref[...] = jnp.zeros(o_ref.shape, o_ref.dtype)


def _expert_ffn(xg, block_exp, n_used, wg, wu, wd, tb):
    n, d = xg.shape
    de = wg.shape[2]
    return pl.pallas_call(
        _expert_kernel,
        out_shape=jax.ShapeDtypeStruct((n, d), BF16),
        grid_spec=pltpu.PrefetchScalarGridSpec(
            num_scalar_prefetch=2,
            grid=(n // tb,),
            in_specs=[pl.BlockSpec((tb, d), lambda i, be, nb: (jnp.minimum(i, nb[0] - 1), 0)),
                      pl.BlockSpec((None, d, de), lambda i, be, nb: (be[i], 0, 0)),
                      pl.BlockSpec((None, d, de), lambda i, be, nb: (be[i], 0, 0)),
                      pl.BlockSpec((None, de, d), lambda i, be, nb: (be[i], 0, 0))],
            out_specs=pl.BlockSpec((tb, d), lambda i, be, nb: (i, 0)),
            scratch_shapes=[pltpu.VMEM((d, de), BF16), pltpu.VMEM((d, de), BF16), pltpu.VMEM((de, d), BF16)]),
        compiler_params=_cparams(1), name="expert_ffn",
    )(block_exp, n_used, xg, wg, wu, wd)


def _moe_combine_kernel(base_ref, nchunk_ref, x_ref, mgate_ref, s_ref, rank_ref, gate_ref, y_ref, o_ref,
                        ybuf, acc_ref, sem, *, tile_off, n_tiles):
    j = pl.program_id(0) + tile_off
    w = COMBINE_WIN

    def window_copy(slot, half, start):
        return pltpu.make_async_copy(y_ref.at[pl.ds(start, w)], ybuf.at[slot, pl.ds(half * w, w)], sem.at[slot, half])

    def start_pair(p, slot):
        for q in range(2):
            window_copy(slot, q, pl.multiple_of(base_ref[j * N_EXPERTS + 2 * p + q], GROUP_ALIGN)).start()

    ahead = WINDOW_RING - 1
    for p0 in range(ahead):
        start_pair(p0, p0)
    acc_ref[...] = s_ref[...].astype(F32)

    def pair_body(p, carry):
        slot = p % WINDOW_RING
        window_copy(slot, 0, 0).wait()
        window_copy(slot, 1, 0).wait()

        @pl.when(p + ahead < N_PAIRS)
        def _():
            start_pair(p + ahead, (p + ahead) % WINDOW_RING)

        sel = jnp.concatenate([_selection(rank_ref[pl.ds(2 * p + q, 1), :], gate_ref[pl.ds(2 * p + q, 1), :], 0.0, w)
                               for q in range(2)], axis=0)
        acc_ref[...] += _dot_tn(sel, ybuf[slot])
        return carry

    lax.fori_loop(0, N_PAIRS, pair_body, 0)

    def overflow_body(e, carry):
        def chunk_body(c, carry2):
            cp = window_copy(0, 0, pl.multiple_of(base_ref[j * N_EXPERTS + e] + c * w, GROUP_ALIGN))
            cp.start()
            cp.wait()
            sel = _selection(rank_ref[pl.ds(e, 1), :], gate_ref[pl.ds(e, 1), :], (c * w).astype(F32), w)
            acc_ref[...] += _dot_tn(sel, ybuf[0, 0:w, :])
            return carry2

        lax.fori_loop(1, nchunk_ref[j * N_EXPERTS + e], chunk_body, 0)
        return carry

    @pl.when(nchunk_ref[n_tiles * N_EXPERTS + j] > 1)
    def _():
        lax.fori_loop(0, N_EXPERTS, overflow_body, 0)

    o_ref[...] = x_ref[...] + mgate_ref[...] * acc_ref[...]


def _moe_combine(x, mod3, mod_row, y, rank_t, gate_t, shared, base, nchunk, tile_off, tm):
    m, d = x.shape
    return pl.pallas_call(
        functools.partial(_moe_combine_kernel, tile_off=tile_off, n_tiles=rank_t.shape[1] // tm),
        out_shape=jax.ShapeDtypeStruct((m, d), F32),
        grid_spec=pltpu.PrefetchScalarGridSpec(
            num_scalar_prefetch=2,
            grid=(m // tm,),
            in_specs=[pl.BlockSpec((tm, d), lambda i, *_: (i, 0)),
                      pl.BlockSpec((None, 1, d), lambda i, *_: (mod_row(i) * 6 + 5, 0, 0)),
                      pl.BlockSpec((tm, d), lambda i, *_: (i + tile_off, 0)),
                      pl.BlockSpec((LANES, tm), lambda i, *_: (0, i + tile_off)),
                      pl.BlockSpec((LANES, tm), lambda i, *_: (0, i + tile_off)),
                      pl.BlockSpec(memory_space=pl.ANY)],
            out_specs=pl.BlockSpec((tm, d), lambda i, *_: (i, 0)),
            scratch_shapes=[pltpu.VMEM((WINDOW_RING, 2 * COMBINE_WIN, d), BF16), pltpu.VMEM((tm, d), F32),
                            pltpu.SemaphoreType.DMA((WINDOW_RING, 2))]),
        compiler_params=_cparams(1), name="moe_combine",
    )(base, nchunk, x, mod3, shared, rank_t, gate_t, y)


ROUTER_TILE = 512


def _moe(h, rw, rb, wg, wu, wd, sg, su, sd, layer):
    t, d = h.shape
    nt = t // ROUTER_TILE
    rank_t, gate_t, cnt = _router(h, rw, rb, ROUTER_TILE)
    counts = cnt[:, 0, :N_EXPERTS].astype(jnp.int32)
    aligned = (counts + GROUP_ALIGN - 1) // GROUP_ALIGN * GROUP_ALIGN
    region = (aligned.sum(axis=0) + COMBINE_WIN + MOE_BLOCK - 1) // MOE_BLOCK * MOE_BLOCK
    region_end = jnp.cumsum(region)
    base = ((region_end - region)[None, :] + jnp.cumsum(aligned, axis=0) - aligned).reshape(-1)

    def window_counts(win):
        n = (counts + win - 1) // win
        return jnp.concatenate([n.reshape(-1), n.max(axis=1)])

    n_blocks = -(-(t * TOP_K + (GROUP_ALIGN - 1) * nt * N_EXPERTS + N_EXPERTS * (COMBINE_WIN + MOE_BLOCK - 1))
                 // MOE_BLOCK)
    block_start = jnp.arange(n_blocks, dtype=jnp.int32) * MOE_BLOCK
    block_exp = jnp.minimum((region_end[None, :] <= block_start[:, None]).sum(axis=1), N_EXPERTS - 1)
    n_used = (region_end[N_EXPERTS - 1] // MOE_BLOCK).astype(jnp.int32).reshape(1)
    xg = _dispatch(h, rank_t, base, window_counts(DISPATCH_WIN), region_end.astype(jnp.int32), n_blocks * MOE_BLOCK,
                   ROUTER_TILE)
    nchunk = window_counts(COMBINE_WIN)
    y = _expert_ffn(xg, block_exp.astype(jnp.int32) + layer * N_EXPERTS, n_used, wg, wu, wd, MOE_BLOCK)
    shared = _expert_ffn(h, jnp.full((t // MOE_BLOCK,), layer, jnp.int32), jnp.full((1,), t // MOE_BLOCK, jnp.int32),
                         sg, su, sd, MOE_BLOCK)
    return y, shared, (rank_t, gate_t, base, nchunk)


def _rope_tables(seq_len, gain):
    nf = MLA_ROPE // 4
    inv = ROPE_THETA ** (-jnp.arange(nf, dtype=F32) / nf)
    pos = jnp.arange(seq_len, dtype=jnp.int32)
    ang_r = (pos // GRID_W).astype(F32)[:, None] * inv
    ang_c = (pos % GRID_W).astype(F32)[:, None] * inv
    cos = jnp.concatenate([jnp.cos(ang_r)] * 2 + [jnp.cos(ang_c)] * 2, axis=-1)
    sin = jnp.concatenate([-jnp.sin(ang_r), jnp.sin(ang_r), -jnp.sin(ang_c), jnp.sin(ang_c)], axis=-1)
    g = gain.astype(F32)
    cg = cos * g[None]
    sg = sin * g[_ROPE_PARTNER][None]
    return jnp.tile(cg, (1, 2)), jnp.tile(sg, (1, 2))


def _rope_partner():
    nf = MLA_ROPE // 4
    idx = np.arange(MLA_ROPE)
    within = idx % (2 * nf)
    return np.where(within < nf, idx + nf, idx - nf)


_ROPE_PARTNER = _rope_partner()


def _in_splits():
    sizes = (NA_W, NA_W, NA_W, MLA_Q_LORA, MLA_KV_LORA, MLA_ROPE, CONV_CH, CONV_CH)
    offs = np.concatenate([[0], np.cumsum(sizes)])
    return {n: int(o) for n, o in zip(("q", "k", "v", "cq", "ckv", "kr", "a", "b", "g"), offs)}


def _prep_layer(l, dmodel, w_in, na_q_g, na_k_g, na_rpb, w_na_o, mla_q_lora_g, mla_w_uq, mla_kv_lora_g, mla_w_ukv,
                mla_q_g, mla_k_g, w_mla_o, conv_w_dw, conv_b_dw, conv_ln_g, conv_ln_b, w_conv_o, w_out,
                router_w, router_b, exp_w_gate, exp_w_up, exp_w_down, sh_w_gate, sh_w_up, sh_w_down, seq_len):
    o = _in_splits()
    w = w_in[l]
    cols = lambda a, n: w[:, a:a + n]
    p = {}
    p["w_p1"] = jnp.concatenate([cols(o["q"], 3 * NA_W), cols(o["a"], 2 * CONV_CH), cols(o["g"], 3 * dmodel)],
                                axis=1).astype(BF16)
    kr = cols(o["kr"], MLA_ROPE)
    krs = kr[:, _ROPE_PARTNER]
    p["w_p2"] = jnp.concatenate([cols(o["cq"], MLA_Q_LORA), kr, kr, krs, krs, cols(o["ckv"], MLA_KV_LORA)],
                                axis=1).astype(BF16)
    qscale = NA_HEAD_DIM ** -0.5 * LOG2E
    p["na_g"] = jnp.concatenate([jnp.tile(na_q_g[l].astype(F32) * qscale, NA_HEADS),
                                 jnp.tile(na_k_g[l].astype(F32), NA_HEADS)]).reshape(1, 2 * NA_W)
    p["na_bias"] = _na_bias_table(na_rpb[l])
    wq = mla_w_uq[l].reshape(MLA_Q_LORA, MLA_HEADS, MLA_QK)
    wq_rope = wq[:, :, MLA_NOPE:]
    flat = lambda a: a.reshape(a.shape[0], -1)
    p["w_uq"] = jnp.concatenate([flat(wq[:, :, :MLA_NOPE]), flat(wq_rope), flat(wq_rope[:, :, _ROPE_PARTNER])],
                                axis=1).astype(BF16)
    wkv = mla_w_ukv[l].reshape(MLA_KV_LORA, MLA_HEADS, MLA_NOPE + MLA_V)
    p["w_ukv"] = jnp.concatenate([flat(wkv[:, :, :MLA_NOPE]), flat(wkv[:, :, MLA_NOPE:])], axis=1).astype(BF16)
    p["q_lora_g"] = mla_q_lora_g[l].astype(F32).reshape(1, -1)
    p["kv_lora_g"] = mla_kv_lora_g[l].astype(F32).reshape(1, -1)
    mscale = MLA_QK ** -0.5 * LOG2E
    p["q_gn"] = (mla_q_g[l][:MLA_NOPE].astype(F32) * mscale).reshape(1, -1)
    p["k_gn"] = mla_k_g[l][:MLA_NOPE].astype(F32).reshape(1, -1)
    cgq, sgq = _rope_tables(seq_len, mla_q_g[l][MLA_NOPE:] * mscale)
    cgk, sgk = _rope_tables(seq_len, mla_k_g[l][MLA_NOPE:])
    p["q_tabs"] = (cgq, sgq)
    p["k_tabs"] = (cgk, sgk)
    p["q_tabs_c"] = (jnp.tile((mla_q_g[l][MLA_NOPE:].astype(F32) * mscale)[None], (1, 2)), jnp.zeros((1, LANES), F32))
    p["k_tabs_c"] = (jnp.tile(mla_k_g[l][MLA_NOPE:].astype(F32)[None], (1, 2)), jnp.zeros((1, LANES), F32))
    p["w_na_o"] = w_na_o[l].astype(BF16)
    p["w_mla_o"] = w_mla_o[l].astype(BF16)
    p["w_conv_o"] = w_conv_o[l].astype(BF16)
    p["w_out"] = w_out[l].astype(BF16)
    p["conv_w"] = jnp.concatenate([conv_w_dw[l].astype(F32), jnp.zeros((1, CONV_CH), F32)], axis=0)
    p["conv_b"] = conv_b_dw[l].astype(F32).reshape(1, -1)
    p["ln_g"] = conv_ln_g[l].astype(F32).reshape(1, -1)
    p["ln_b"] = conv_ln_b[l].astype(F32).reshape(1, -1)
    p["router_w"] = jnp.concatenate([router_w[l], jnp.zeros((dmodel, LANES - N_EXPERTS), router_w.dtype)],
                                    axis=1).astype(BF16)
    p["router_b"] = jnp.concatenate([router_b[l].astype(F32), jnp.zeros((LANES - N_EXPERTS,), F32)]).reshape(1, LANES)
    flat_e = lambda a: a.reshape((-1,) + a.shape[2:])
    p["exp"] = (flat_e(exp_w_gate), flat_e(exp_w_up), flat_e(exp_w_down))
    p["sh"] = (sh_w_gate, sh_w_up, sh_w_down)
    return p


def _group_matrices():
    i = np.arange(LANES)
    bd = (i[:, None] // NA_HEAD_DIM == i[None, :] // NA_HEAD_DIM)
    ones = np.ones((LANES, LANES), bool)
    e0 = np.broadcast_to((i < MLA_ROPE)[:, None], (LANES, LANES))
    e1 = np.broadcast_to((i >= MLA_ROPE)[:, None], (LANES, LANES))
    f = lambda m: jnp.asarray(m, BF16)
    return f(bd), (f(ones), f(e0), f(e1))


def _expand_tabs(tabs, tm):
    return tuple(jnp.broadcast_to(t, (tm, LANES)) if t.shape[0] == 1 else t for t in tabs)


def _mixer_inputs(h, p, mats, bd, seq, ctx):
    tm_mm = min(1024, h.shape[0])
    p1 =_matmul(h, p["w_p1"], BF16, tm_mm, 1024, "in_proj_1")
    p2 = _matmul(h, p["w_p2"], BF16, tm_mm, P2_W // 2, "in_proj_2")
    qk = _headnorm(p1, p["na_g"], bd, min(512, h.shape[0]))
    tr = 256
    qt = _expand_tabs(p["q_tabs_c"], tr) if ctx else p["q_tabs"]
    kt = _expand_tabs(p["k_tabs_c"], tr) if ctx else p["k_tabs"]
    tseq = tr if ctx else seq
    km, vm = _mla_kv(p2, p["kv_lora_g"], p["w_ukv"], p["k_gn"], kt[0], kt[1], mats, tseq, tr)
    return p1, p2, qk, km, vm, (qt, tseq, tr)


def kernel(x, c, ctx, c_ctx, ada_w, ada_b, norm1_g, norm2_g, w_in, na_q_g, na_k_g, na_rpb, w_na_o, mla_q_lora_g, mla_w_uq, mla_kv_lora_g, mla_w_ukv, mla_q_g, mla_k_g, w_mla_o, conv_w_dw, conv_b_dw, conv_ln_g, conv_ln_b, w_conv_o, w_out, router_w, router_b, exp_w_gate, exp_w_up, exp_w_down, sh_w_gate, sh_w_up, sh_w_down):
    batch, seq, d = x.shape
    lc = ctx.shape[1]
    depth = ada_w.shape[0]
    mod_rows = (batch + 1 + 7) // 8 * 8
    cc = jnp.concatenate([c, c_ctx[None], jnp.zeros((mod_rows - batch - 1, d), c.dtype)], axis=0)
    mod_all = _ada(cc, ada_w, ada_b)
    bd, mats = _group_matrices()
    xs = x.reshape(batch * seq, d)
    zs = ctx.reshape(batch * lc, d)
    tm_x = 512
    tm_c = 256
    x_row = lambda i: (i * tm_x) // seq
    c_row = lambda i: batch

    for l in range(depth):
        need_ctx = l < depth - 1
        p = _prep_layer(l, d, w_in, na_q_g, na_k_g, na_rpb, w_na_o, mla_q_lora_g, mla_w_uq, mla_kv_lora_g, mla_w_ukv,
                        mla_q_g, mla_k_g, w_mla_o, conv_w_dw, conv_b_dw, conv_ln_g, conv_ln_b, w_conv_o, w_out,
                        router_w, router_b, exp_w_gate, exp_w_up, exp_w_down, sh_w_gate, sh_w_up, sh_w_down, seq)
        mod3 = mod_all[l].reshape(mod_rows * 6, 1, d)
        hx = _norm_mod(xs, norm1_g[l], mod3, x_row, 1, 0, tm_x)
        hc = _norm_mod(zs, norm1_g[l], mod3, c_row, 1, 0, tm_c)
        p1x, p2x, qkx, kmx, vmx, (qtx, tsx, trx) = _mixer_inputs(hx, p, mats, bd, seq, False)
        p1c, p2c, qkc, kmc, vmc, (qtc, tsc, trc) = _mixer_inputs(hc, p, mats, bd, lc, True)
        o_na = _na_attention(qkx, p1x, qkc, p1c, p["na_bias"], batch)
        qmx = _mla_q(p2x, p["q_lora_g"], p["w_uq"], p["q_gn"], qtx[0], qtx[1], mats, tsx, trx)
        o_mla = _mla_attention(qmx, [(kmc, vmc), (kmx, vmx)], batch, 512)
        u = _conv_module(p1x, p["conv_w"], p["conv_b"], p["ln_g"], p["ln_b"], seq, 256)
        mrg = _merge(o_na, o_mla, u, p1x, p["w_na_o"], p["w_mla_o"], p["w_conv_o"], 1024, 512)
        xs, hx2 = _out_proj(mrg, p["w_out"], xs, mod3, x_row, norm2_g[l], tm_x)
        if need_ctx:
            o_na_c = _na_ctx_attention(qkc, p1c, batch)
            qmc = _mla_q(p2c, p["q_lora_g"], p["w_uq"], p["q_gn"], qtc[0], qtc[1], mats, tsc, trc)
            o_mla_c = _mla_attention(qmc, [(kmc, vmc)], batch, lc)
            u_c = _conv_module(p1c, p["conv_w"], p["conv_b"], p["ln_g"], p["ln_b"], lc, lc)
            mrg_c = _merge(o_na_c, o_mla_c, u_c, p1c, p["w_na_o"], p["w_mla_o"], p["w_conv_o"], 1024, 512)
            zs, hc2 = _out_proj(mrg_c, p["w_out"], zs, mod3, c_row, norm2_g[l], tm_c)
            tokens = jnp.concatenate([hc2, hx2], axis=0)
        else:
            tokens = hx2
        y, shared, (rank_t, gate_t, base, nchunk) = _moe(tokens, p["router_w"], p["router_b"], *p["exp"], *p["sh"], l)
        tm_cmb = ROUTER_TILE
        x_row_cmb = lambda i: (i * tm_cmb) // seq
        if need_ctx:
            zs = _moe_combine(zs, mod3, c_row, y, rank_t, gate_t, shared, base, nchunk, 0, tm_cmb)
            xs = _moe_combine(xs, mod3, x_row_cmb, y, rank_t, gate_t, shared, base, nchunk, batch * lc // tm_cmb,
                              tm_cmb)
        else:
            xs = _moe_combine(xs, mod3, x_row_cmb, y, rank_t, gate_t, shared, base, nchunk, 0, tm_cmb)
    return xs.reshape(batch, seq, d)
```
